```python
import math
import jax, jax.numpy as jnp
from jax import lax
import numpy as np

D_MODEL = 1024
BATCH = 8
SEQ = 4096
DEPTH = 4

CHUNK = 64
Q_BLOCK = 128
N_A_LAYERS = DEPTH // 2
N_B_LAYERS = DEPTH - N_A_LAYERS
SSM_GROUP = 16
SSM_GROUPS = D_MODEL // SSM_GROUP
SSM_STATE = 64
DT_MIN = 0.001
DT_MAX = 0.1
N_HEADS = 16
QK_NOPE_DIM = 64
QK_ROPE_DIM = 32
V_HEAD_DIM = 64
Q_LORA_RANK = 256
KV_LORA_RANK = 256
ROPE_THETA = 10000.0
ATTN_SCALE = 1.0 / math.sqrt(QK_NOPE_DIM + QK_ROPE_DIM)
D_FF = ((8 * D_MODEL + 3 * 256 - 1) // (3 * 256)) * 256
EPS = 1e-6

kernel_name = "yoco_s5_mla_adaln_encoder"


def rms_norm(x, g):
    xf = x.astype(jnp.float32)
    y = xf * lax.rsqrt(jnp.mean(xf * xf, axis=-1, keepdims=True) + EPS)
    return (y * g.astype(jnp.float32)).astype(x.dtype)


def modulate(h, shift, scale):
    return h * (1.0 + scale[:, None, :]) + shift[:, None, :]


def rope_cos_sin(positions):
    inv = 1.0 / (ROPE_THETA ** (jnp.arange(0, QK_ROPE_DIM, 2, dtype=jnp.float32) / QK_ROPE_DIM))
    ang = positions.astype(jnp.float32)[..., None] * inv
    return jnp.cos(ang), jnp.sin(ang)


def apply_rope(x, cos, sin):
    shape = cos.shape[:2] + (1,) * (x.ndim - 3) + cos.shape[-1:]
    cos = cos.reshape(shape)
    sin = sin.reshape(shape)
    x1, x2 = jnp.split(x.astype(jnp.float32), 2, axis=-1)
    return jnp.concatenate([x1 * cos - x2 * sin, x1 * sin + x2 * cos], axis=-1).astype(x.dtype)


def _complex_linear_combine(e1, e2):
    a1r, a1i, b1r, b1i = e1
    a2r, a2i, b2r, b2i = e2
    ar = a1r * a2r - a1i * a2i
    ai = a1r * a2i + a1i * a2r
    br = a2r * b1r - a2i * b1i + b2r
    bi = a2r * b1i + a2i * b1r + b2i
    return (ar, ai, br, bi)


def s5_mixer(h, lam_re, lam_im, log_dt, b_re, b_im, c_re, c_im, d_skip, w_glu, b_glu):
    bsz, s_len, d = h.shape
    f32 = jnp.float32
    lr = lam_re.astype(f32)
    li = lam_im.astype(f32)
    dt = jnp.exp(log_dt.astype(f32))[:, None]
    mag = jnp.exp(lr * dt)
    ab_re = mag * jnp.cos(li * dt)
    ab_im = mag * jnp.sin(li * dt)
    den = lr * lr + li * li
    nr = ab_re - 1.0
    ni = ab_im
    f_re = (nr * lr + ni * li) / den
    f_im = (ni * lr - nr * li) / den
    br = b_re.astype(f32)
    bi = b_im.astype(f32)
    bb_re = f_re[..., None] * br - f_im[..., None] * bi
    bb_im = f_re[..., None] * bi + f_im[..., None] * br
    cr = c_re.astype(f32)
    ci = c_im.astype(f32)

    n_chunks = s_len // CHUNK
    u = h.astype(f32).reshape(bsz, n_chunks, CHUNK, SSM_GROUPS, SSM_GROUP).transpose(1, 0, 2, 3, 4)

    def chunk_step(carry, u_c):
        s_re, s_im = carry
        bu_re = jnp.einsum('blgp,gnp->blgn', u_c, bb_re)
        bu_im = jnp.einsum('blgp,gnp->blgn', u_c, bb_im)
        a_re = jnp.broadcast_to(ab_re, bu_re.shape)
        a_im = jnp.broadcast_to(ab_im, bu_im.shape)
        pa_re, pa_im, loc_re, loc_im = lax.associative_scan(
            _complex_linear_combine, (a_re, a_im, bu_re, bu_im), axis=1)
        st_re = loc_re + pa_re * s_re[:, None] - pa_im * s_im[:, None]
        st_im = loc_im + pa_re * s_im[:, None] + pa_im * s_re[:, None]
        y = jnp.einsum('blgn,gpn->blgp', st_re, cr) - jnp.einsum('blgn,gpn->blgp', st_im, ci)
        return (st_re[:, -1], st_im[:, -1]), y

    init = (jnp.zeros((bsz, SSM_GROUPS, SSM_STATE), f32), jnp.zeros((bsz, SSM_GROUPS, SSM_STATE), f32))
    _, y = lax.scan(chunk_step, init, u)
    y = y.transpose(1, 0, 2, 3, 4).reshape(bsz, s_len, d)
    y = (y + d_skip.astype(f32) * h.astype(f32)).astype(h.dtype)
    g = jax.nn.gelu(y)
    return g * jax.nn.sigmoid(g @ w_glu + b_glu)


def shared_mla_kv(hk, cos, sin, w_kv_a, kv_a_norm_g, w_kv_b, k_nope_norm_g, k_rope_norm_g):
    bsz, s_len, _ = hk.shape
    kv_a = hk @ w_kv_a
    c_kv, k_rope = jnp.split(kv_a, [KV_LORA_RANK], axis=-1)
    c_kv = rms_norm(c_kv, kv_a_norm_g)
    kv = (c_kv @ w_kv_b).reshape(bsz, s_len, N_HEADS, QK_NOPE_DIM + V_HEAD_DIM)
    k_nope, v = jnp.split(kv, [QK_NOPE_DIM], axis=-1)
    k_nope = rms_norm(k_nope, k_nope_norm_g)
    k_rope = apply_rope(rms_norm(k_rope, k_rope_norm_g), cos, sin)
    return k_nope, k_rope, v


def mla_attention(h, cos, sin, k_nope, k_rope, v, w_dq, q_norm_g, w_uq, q_nope_norm_g, q_rope_norm_g, w_o):
    bsz, s_len, _ = h.shape
    q = rms_norm(h @ w_dq, q_norm_g) @ w_uq
    q = q.reshape(bsz, s_len, N_HEADS, QK_NOPE_DIM + QK_ROPE_DIM)
    q_nope, q_rope = jnp.split(q, [QK_NOPE_DIM], axis=-1)
    q_nope = rms_norm(q_nope, q_nope_norm_g)
    q_rope = apply_rope(rms_norm(q_rope, q_rope_norm_g), cos, sin)
    n_blocks = s_len // Q_BLOCK
    qn_b = q_nope.reshape(bsz, n_blocks, Q_BLOCK, N_HEADS, QK_NOPE_DIM).transpose(1, 0, 2, 3, 4)
    qr_b = q_rope.reshape(bsz, n_blocks, Q_BLOCK, N_HEADS, QK_ROPE_DIM).transpose(1, 0, 2, 3, 4)
    key_chunk = jnp.arange(s_len) // CHUNK

    def block_attn(args):
        qn, qr, blk = args
        s = (jnp.einsum('bqhd,bkhd->bhqk', qn, k_nope, preferred_element_type=jnp.float32)
             + jnp.einsum('bqhr,bkr->bhqk', qr, k_rope, preferred_element_type=jnp.float32)) * ATTN_SCALE
        q_chunk = (blk * Q_BLOCK + jnp.arange(Q_BLOCK)) // CHUNK
        mask = q_chunk[:, None] >= key_chunk[None, :]
        s = jnp.where(mask[None, None], s, -1e30)
        p = jax.nn.softmax(s, axis=-1)
        return jnp.einsum('bhqk,bkhd->bqhd', p.astype(v.dtype), v)

    o = lax.map(block_attn, (qn_b, qr_b, jnp.arange(n_blocks)))
    o = o.transpose(1, 0, 2, 3, 4).reshape(bsz, s_len, N_HEADS * V_HEAD_DIM)
    return o @ w_o


def swiglu(h, w_gate, w_up, w_down):
    return (jax.nn.silu(h @ w_gate) * (h @ w_up)) @ w_down


def setup_inputs(seed: int = 0) -> dict:
    key = jax.random.key(seed)
    ks = iter(jax.random.split(key, 48))
    f32 = jnp.float32
    D, F, G, N, P = D_MODEL, D_FF, SSM_GROUPS, SSM_STATE, SSM_GROUP
    NA, NB = N_A_LAYERS, N_B_LAYERS

    def nrm(shape, scale):
        return jax.random.normal(next(ks), shape, f32) * scale

    def gain(shape):
        return 1.0 + 0.02 * jax.random.normal(next(ks), shape, f32)

    x = jax.random.normal(next(ks), (BATCH, SEQ, D), f32)
    c = jax.random.normal(next(ks), (BATCH, D), f32)
    offsets = jax.random.randint(next(ks), (BATCH, 1), 0, 4096, dtype=jnp.int32)
    positions = offsets + jnp.arange(SEQ, dtype=jnp.int32)[None, :]

    n_idx = jnp.arange(N, dtype=f32)
    inputs = {
        "x": x, "c": c, "positions": positions,
        "ada_w": nrm((DEPTH, D, 6 * D), 0.5 * D ** -0.5),
        "ada_b": nrm((DEPTH, 6 * D), 0.02),
        "norm1_g": gain((DEPTH, D)),
        "norm2_g": gain((DEPTH, D)),
        "ffn_w_gate": nrm((DEPTH, D, F), D ** -0.5),
        "ffn_w_up": nrm((DEPTH, D, F), D ** -0.5),
        "ffn_w_down": nrm((DEPTH, F, D), F ** -0.5),
        "s5_lam_re": -0.5 + 0.01 * jax.random.normal(next(ks), (NA, G, N), f32),
        "s5_lam_im": math.pi * n_idx[None, None, :] + 0.01 * jax.random.normal(next(ks), (NA, G, N), f32),
        "s5_log_dt": jax.random.uniform(next(ks), (NA, G), f32, math.log(DT_MIN), math.log(DT_MAX)),
        "s5_b_re": nrm((NA, G, N, P), P ** -0.5),
        "s5_b_im": nrm((NA, G, N, P), P ** -0.5),
        "s5_c_re": nrm((NA, G, P, N), N ** -0.5),
        "s5_c_im": nrm((NA, G, P, N), N ** -0.5),
        "s5_d": nrm((NA, D), 1.0),
        "s5_w_glu": nrm((NA, D, D), D ** -0.5),
        "s5_b_glu": nrm((NA, D), 0.02),
        "kv_ada_w": nrm((D, 2 * D), 0.5 * D ** -0.5),
        "kv_ada_b": nrm((2 * D,), 0.02),
        "kv_norm_g": gain((D,)),
        "w_kv_a": nrm((D, KV_LORA_RANK + QK_ROPE_DIM), D ** -0.5),
        "kv_a_norm_g": gain((KV_LORA_RANK,)),
        "w_kv_b": nrm((KV_LORA_RANK, N_HEADS * (QK_NOPE_DIM + V_HEAD_DIM)), KV_LORA_RANK ** -0.5),
        "k_nope_norm_g": gain((QK_NOPE_DIM,)),
        "k_rope_norm_g": gain((QK_ROPE_DIM,)),
        "mla_w_dq": nrm((NB, D, Q_LORA_RANK), D ** -0.5),
        "mla_q_norm_g": gain((NB, Q_LORA_RANK)),
        "mla_w_uq": nrm((NB, Q_LORA_RANK, N_HEADS * (QK_NOPE_DIM + QK_ROPE_DIM)), Q_LORA_RANK ** -0.5),
        "mla_q_nope_norm_g": gain((NB, QK_NOPE_DIM)),
        "mla_q_rope_norm_g": gain((NB, QK_ROPE_DIM)),
        "mla_w_o": nrm((NB, N_HEADS * V_HEAD_DIM, D), (N_HEADS * V_HEAD_DIM) ** -0.5),
    }
    return inputs


def reference(x, c, positions, ada_w, ada_b, norm1_g, norm2_g, ffn_w_gate, ffn_w_up, ffn_w_down,
              s5_lam_re, s5_lam_im, s5_log_dt, s5_b_re, s5_b_im, s5_c_re, s5_c_im, s5_d, s5_w_glu, s5_b_glu,
              kv_ada_w, kv_ada_b, kv_norm_g, w_kv_a, kv_a_norm_g, w_kv_b, k_nope_norm_g, k_rope_norm_g,
              mla_w_dq, mla_q_norm_g, mla_w_uq, mla_q_nope_norm_g, mla_q_rope_norm_g, mla_w_o):
    cos, sin = rope_cos_sin(positions)
    c_act = jax.nn.silu(c)
    k_nope = k_rope = v = None
    for l in range(DEPTH):
        shift1, scale1, gate1, shift2, scale2, gate2 = jnp.split(c_act @ ada_w[l] + ada_b[l], 6, axis=-1)
        if l == N_A_LAYERS:
            k_shift, k_scale = jnp.split(c_act @ kv_ada_w + kv_ada_b, 2, axis=-1)
            hk = modulate(rms_norm(x, kv_norm_g), k_shift, k_scale)
            k_nope, k_rope, v = shared_mla_kv(hk, cos, sin, w_kv_a, kv_a_norm_g, w_kv_b,
                                              k_nope_norm_g, k_rope_norm_g)
        h = modulate(rms_norm(x, norm1_g[l]), shift1, scale1)
        if l < N_A_LAYERS:
            mix = s5_mixer(h, s5_lam_re[l], s5_lam_im[l], s5_log_dt[l], s5_b_re[l], s5_b_im[l],
                           s5_c_re[l], s5_c_im[l], s5_d[l], s5_w_glu[l], s5_b_glu[l])
        else:
            j = l - N_A_LAYERS
            mix = mla_attention(h, cos, sin, k_nope, k_rope, v, mla_w_dq[j], mla_q_norm_g[j], mla_w_uq[j],
                                mla_q_nope_norm_g[j], mla_q_rope_norm_g[j], mla_w_o[j])
        x = x + gate1[:, None, :] * mix
        h = modulate(rms_norm(x, norm2_g[l]), shift2, scale2)
        x = x + gate2[:, None, :] * swiglu(h, ffn_w_gate[l], ffn_w_up[l], ffn_w_down[l])
    return x
```

```python
import functools
import math

import jax
import jax.numpy as jnp
from jax import lax
from jax.experimental import pallas as pl
from jax.experimental.pallas import tpu as pltpu

F32 = jnp.float32
BF16 = jnp.bfloat16

CHUNK = 64
SSM_GROUP = 16
SSM_STATE = 64
N_HEADS = 16
QK_NOPE_DIM = 64
QK_ROPE_DIM = 32
V_HEAD_DIM = 64
ROPE_THETA = 10000.0
ATTN_SCALE = 1.0 / math.sqrt(QK_NOPE_DIM + QK_ROPE_DIM)
EPS = 1e-6
MASK_VALUE = -1e30

LANES = 128
SUBLANES = 8
HEAD_LANES = 128
SLAB_GROUPS = LANES // SSM_GROUP
VMEM_LIMIT = 56 * 1024 * 1024

ROW_TILE = 512
S5_TIME_TILE = 32
SCAN_LANES = 512
FFN_CHUNK = 256
ATTN_TILE = 256
HEADS_PER_STEP = 2


def _params(*sem):
    return pltpu.CompilerParams(dimension_semantics=sem, vmem_limit_bytes=VMEM_LIMIT)


def _rms(x, g):
    return x * lax.rsqrt(jnp.mean(x * x, axis=-1, keepdims=True) + EPS) * g


def _rowwise(fn, a, *mods):
    r = mods[0].shape[0]
    if r == 1:
        return fn(a, *mods)
    rows, d = a.shape
    out = fn(a.reshape(rows // r, r, d), *[m[None] for m in mods])
    return out.reshape(rows, d)


def _modulate(h, shift, scale):
    return _rowwise(lambda a, sh, sc: a * (1.0 + sc) + sh, h, shift, scale)


def _gated_add(x, gate, upd):
    return x + _rowwise(lambda a, g: a * g, upd, gate)


def _dot(a, b):
    return jnp.dot(a, b, preferred_element_type=F32)


def _resident(a):
    zeros = (0,) * a.ndim
    return pl.BlockSpec(a.shape, lambda *_: zeros, pipeline_mode=pl.Buffered(1))


def _lane_iota(shape):
    return lax.broadcasted_iota(jnp.int32, shape, len(shape) - 1)


def _mods_kernel(c_ref, w_ref, b_ref, o_ref):
    c = c_ref[...]
    ca = c * jax.nn.sigmoid(c)
    o_ref[0] = jnp.dot(ca, w_ref[0], preferred_element_type=F32,
                       precision=lax.Precision.HIGHEST) + b_ref[0]


def _mods(c, w, b, tn=2048):
    nl, d, n = w.shape
    bsz = c.shape[0]
    return pl.pallas_call(
        _mods_kernel,
        grid=(nl, n // tn),
        in_specs=[pl.BlockSpec((bsz, d), lambda l, j: (0, 0)),
                  pl.BlockSpec((1, d, tn), lambda l, j: (l, 0, j)),
                  pl.BlockSpec((1, 1, tn), lambda l, j: (l, 0, j))],
        out_specs=pl.BlockSpec((1, bsz, tn), lambda l, j: (l, 0, j)),
        out_shape=jax.ShapeDtypeStruct((nl, bsz, n), F32),
        compiler_params=_params("parallel", "parallel"),
        name="adaln_mods",
    )(c, w, b.reshape(nl, 1, n))


def _rope_kernel(pos_ref, inv_ref, cos_ref, sin_ref):
    ang = inv_ref[...] * pos_ref[...]
    cos_ref[...] = jnp.cos(ang)
    sin_ref[...] = jnp.sin(ang)


def _rope_tables(positions, tn=4096):
    t = positions.size
    half = QK_ROPE_DIM // 2
    inv = 1.0 / (ROPE_THETA ** (jnp.arange(0, QK_ROPE_DIM, 2, dtype=F32) / QK_ROPE_DIM))
    pos = positions.astype(F32).reshape(1, t)
    cos, sin = pl.pallas_call(
        _rope_kernel,
        grid=(t // tn,),
        in_specs=[pl.BlockSpec((1, tn), lambda i: (0, i)),
                  pl.BlockSpec((half, 1), lambda i: (0, 0))],
        out_specs=[pl.BlockSpec((half, tn), lambda i: (0, i))] * 2,
        out_shape=[jax.ShapeDtypeStruct((half, t), F32)] * 2,
        compiler_params=_params("parallel"),
        name="rope_tables",
    )(pos, inv.reshape(half, 1))
    reps = LANES // half
    return jnp.tile(cos.T, (1, reps)), jnp.tile(sin.T, (1, reps))


def _rope_sign(shape):
    lane = _lane_iota(shape)
    return jnp.where((lane % QK_ROPE_DIM) < (QK_ROPE_DIM // 2), -1.0, 1.0).astype(F32)


def _s5_disc_kernel(lr_ref, li_ref, ldt_ref, br_ref, bi_ref, abr_ref, abi_ref, bbr_ref, bbi_ref):
    lr = lr_ref[0]
    li = li_ref[0]
    dt = jnp.exp(ldt_ref[0])
    mag = jnp.exp(lr * dt)
    ab_re = mag * jnp.cos(li * dt)
    ab_im = mag * jnp.sin(li * dt)
    den = lr * lr + li * li
    nr = ab_re - 1.0
    ni = ab_im
    f_re = (nr * lr + ni * li) / den
    f_im = (ni * lr - nr * li) / den
    br = br_ref[0]
    bi = bi_ref[0]
    abr_ref[0] = ab_re
    abi_ref[0] = ab_im
    bbr_ref[0] = f_re[None] * br - f_im[None] * bi
    bbi_ref[0] = f_re[None] * bi + f_im[None] * br


def _s5_discretise(lam_re, lam_im, log_dt, b_re, b_im):
    na, g, n = lam_re.shape
    p = b_re.shape[-1]
    bt_re = b_re.transpose(0, 3, 1, 2)
    bt_im = b_im.transpose(0, 3, 1, 2)
    m2 = pl.BlockSpec((1, g, n), lambda l: (l, 0, 0))
    m3 = pl.BlockSpec((1, p, g, n), lambda l: (l, 0, 0, 0))
    return pl.pallas_call(
        _s5_disc_kernel,
        grid=(na,),
        in_specs=[m2, m2, pl.BlockSpec((1, g, 1), lambda l: (l, 0, 0)), m3, m3],
        out_specs=[m2, m2, m3, m3],
        out_shape=[jax.ShapeDtypeStruct((na, g, n), F32)] * 2
        + [jax.ShapeDtypeStruct((na, p, g, n), F32)] * 2,
        compiler_params=_params("parallel"),
        name="s5_discretise",
    )(lam_re, lam_im, log_dt.reshape(na, g, 1), bt_re, bt_im)


def _slab_block_diag(w_gab):
    g, a, b = w_gab.shape
    ns = g // SLAB_GROUPS
    w = w_gab.reshape(ns, SLAB_GROUPS, a, b)
    eye = jnp.eye(SLAB_GROUPS, dtype=w.dtype)
    return jnp.einsum("kgab,gh->kgahb", w, eye).reshape(ns, SLAB_GROUPS * a, SLAB_GROUPS * b)


def _s5_mixer_kernel(x_ref, shift_ref, scale_ref, gate_ref, g_ref, bblk_ref, cre_ref, cim_ref,
                     are_ref, aim_ref, d_ref, wglu_ref, bglu_ref, o_ref,
                     bu_re, bu_im, st_re, st_im, y_scr, *, ts, n_slabs, slab_state):
    @pl.when(pl.program_id(0) == 0)
    def _():
        st_re[...] = jnp.zeros_like(st_re)
        st_im[...] = jnp.zeros_like(st_im)

    x = x_ref[...]
    h = _modulate(_rms(x, g_ref[...]), shift_ref[...], scale_ref[...])
    hb = h.astype(BF16)

    for k in range(n_slabs):
        bu = _dot(hb[:, k * LANES:(k + 1) * LANES], bblk_ref[k])
        bu_re[:, k * slab_state:(k + 1) * slab_state] = bu[:, :slab_state]
        bu_im[:, k * slab_state:(k + 1) * slab_state] = bu[:, slab_state:]

    n_state = n_slabs * slab_state
    for cb in range(n_state // SCAN_LANES):
        sl = slice(cb * SCAN_LANES, (cb + 1) * SCAN_LANES)
        ar = jnp.broadcast_to(are_ref[:, sl], (SUBLANES, SCAN_LANES))
        ai = jnp.broadcast_to(aim_ref[:, sl], (SUBLANES, SCAN_LANES))

        def step(i, carry, sl=sl, ar=ar, ai=ai):
            sr, si = carry
            r0 = pl.multiple_of(i * SUBLANES, SUBLANES)
            nr = ar * sr - ai * si + bu_re[pl.ds(r0, SUBLANES), sl]
            ni = ar * si + ai * sr + bu_im[pl.ds(r0, SUBLANES), sl]
            bu_re[pl.ds(r0, SUBLANES), sl] = nr
            bu_im[pl.ds(r0, SUBLANES), sl] = ni
            return nr, ni

        sr, si = lax.fori_loop(0, ts, step, (st_re[:, sl], st_im[:, sl]), unroll=4)
        st_re[:, sl] = sr
        st_im[:, sl] = si

    for k in range(n_slabs):
        ssl = slice(k * slab_state, (k + 1) * slab_state)
        yk = _dot(bu_re[:, ssl].astype(BF16), cre_ref[k]) - _dot(bu_im[:, ssl].astype(BF16), cim_ref[k])
        y_scr[:, k * LANES:(k + 1) * LANES] = yk

    y = y_scr[...] + d_ref[...] * h
    gl = jax.nn.gelu(y)
    z = _dot(gl.astype(BF16), wglu_ref[...]) + bglu_ref[...]
    mix = gl * jax.nn.sigmoid(z)
    o_ref[...] = _gated_add(x, gate_ref[...], mix)


def _s5_mixer(x_tb, shift, scale, gate, norm_g, bblk, cre, cim, a_re, a_im, d_skip, w_glu, b_glu, bsz):
    t, d = x_tb.shape
    ts = S5_TIME_TILE
    rows = ts * bsz
    n_slabs, _, two_state = bblk.shape
    slab_state = two_state // 2
    n_state = n_slabs * slab_state
    full = _resident
    row = lambda a: a.reshape(1, -1)
    args = (x_tb, shift, scale, gate, row(norm_g), bblk, cre, cim, row(a_re), row(a_im),
            row(d_skip), w_glu, row(b_glu))
    return pl.pallas_call(
        functools.partial(_s5_mixer_kernel, ts=ts, n_slabs=n_slabs, slab_state=slab_state),
        grid=(t // rows,),
        in_specs=[pl.BlockSpec((rows, d), lambda i: (i, 0))] + [full(a) for a in args[1:]],
        out_specs=pl.BlockSpec((rows, d), lambda i: (i, 0)),
        out_shape=jax.ShapeDtypeStruct((t, d), F32),
        scratch_shapes=[pltpu.VMEM((rows, n_state), F32), pltpu.VMEM((rows, n_state), F32),
                        pltpu.VMEM((bsz, n_state), F32), pltpu.VMEM((bsz, n_state), F32),
                        pltpu.VMEM((rows, d), F32)],
        compiler_params=_params("arbitrary"),
        name="s5_mixer",
    )(*args)


def _ffn_kernel(*refs, n_chunks, with_attn):
    if with_attn:
        (x_ref, o_in_ref, wo_ref, gate1_ref, shift_ref, scale_ref, gate2_ref, g_ref,
         wg_ref, wu_ref, wd_ref, out_ref, h_scr, acc_scr) = refs
        x = x_ref[...] + gate1_ref[0] * _dot(o_in_ref[...], wo_ref[...])
        shift, scale, gate2 = shift_ref[0], scale_ref[0], gate2_ref[0]
    else:
        (x_ref, shift_ref, scale_ref, gate2_ref, g_ref,
         wg_ref, wu_ref, wd_ref, out_ref, h_scr, acc_scr) = refs
        x = x_ref[...]
        shift, scale, gate2 = shift_ref[...], scale_ref[...], gate2_ref[...]

    h_scr[...] = _modulate(_rms(x, g_ref[...]), shift, scale).astype(BF16)
    acc_scr[...] = jnp.zeros_like(acc_scr)

    def chunk(f, carry):
        hb = h_scr[...]
        gt = _dot(hb, wg_ref[f])
        up = _dot(hb, wu_ref[f])
        act = (gt * jax.nn.sigmoid(gt) * up).astype(BF16)
        acc_scr[...] += _dot(act, wd_ref[f])
        return carry

    lax.fori_loop(0, n_chunks, chunk, 0)
    out_ref[...] = _gated_add(x, gate2, acc_scr[...])


def _ffn(x, shift, scale, gate2, norm_g, wg, wu, wd, attn=None):
    t, d = x.shape
    n_chunks = wg.shape[0]
    tm = ROW_TILE
    full = _resident
    xspec = pl.BlockSpec((tm, d), lambda i: (i, 0))
    g2 = norm_g.reshape(1, d)
    if attn is None:
        args = (x, shift, scale, gate2, g2, wg, wu, wd)
        in_specs = [xspec] + [full(a) for a in args[1:]]
    else:
        o_in, w_o, gate1 = attn
        tiles_per_batch = (t // gate1.shape[0]) // tm
        per_b = pl.BlockSpec((1, 1, d), lambda i: (i // tiles_per_batch, 0, 0))
        args = (x, o_in, w_o, gate1, shift, scale, gate2, g2, wg, wu, wd)
        in_specs = [xspec, xspec, full(w_o), per_b, per_b, per_b, per_b, full(g2), full(wg), full(wu), full(wd)]
    return pl.pallas_call(
        functools.partial(_ffn_kernel, n_chunks=n_chunks, with_attn=attn is not None),
        grid=(t // tm,),
        in_specs=in_specs,
        out_specs=xspec,
        out_shape=jax.ShapeDtypeStruct((t, d), F32),
        scratch_shapes=[pltpu.VMEM((tm, d), BF16), pltpu.VMEM((tm, d), F32)],
        compiler_params=_params("parallel"),
        name="ffn_attn_out" if attn is not None else "ffn",
    )(*args)


def _head_tile(nope, rope_at_64, h):
    blk = nope[:, (h // 2) * LANES:(h // 2 + 1) * LANES]
    if h % 2:
        blk = pltpu.roll(blk, QK_NOPE_DIM, 1)
    return jnp.where(_lane_iota(blk.shape) < QK_NOPE_DIM, blk, rope_at_64)


def _kv_kernel(x_ref, shift_ref, scale_ref, g_ref, wa_ref, ga_ref, wkn_ref, wv_ref, gkn_ref,
               gkr_ref, gkrs_ref, seg_ref, ct_ref, st_ref, k_ref, v_ref):
    x = x_ref[...]
    hk = _modulate(_rms(x, g_ref[...]), shift_ref[0], scale_ref[0])
    kva = _dot(hk.astype(BF16), wa_ref[...])
    rank = ga_ref.shape[1]
    cb = _rms(kva[:, :rank], ga_ref[...]).astype(BF16)

    kn = _dot(cb, wkn_ref[...])
    ss = _dot((kn * kn).astype(BF16), seg_ref[...])
    kn = kn * lax.rsqrt(ss * (1.0 / QK_NOPE_DIM) + EPS) * gkn_ref[...]

    a = kva[:, rank:rank + LANES]
    asw = kva[:, rank + LANES:rank + 2 * LANES]
    rstd = lax.rsqrt(jnp.sum(a * a, axis=-1, keepdims=True) * (1.0 / QK_ROPE_DIM) + EPS)
    sgn = _rope_sign(a.shape)
    kr = rstd * (a * gkr_ref[...] * ct_ref[...] + asw * gkrs_ref[...] * (st_ref[...] * sgn))
    kr64 = pltpu.roll(kr, QK_NOPE_DIM, 1)

    v = _dot(cb, wv_ref[...])
    ones_col = (_lane_iota((1, LANES)) == V_HEAD_DIM).astype(F32)
    for h in range(N_HEADS):
        k_ref[0, h] = _head_tile(kn, kr64, h).astype(BF16)
        v_ref[0, h] = (v[:, h * LANES:(h + 1) * LANES] + ones_col).astype(BF16)


def _shared_kv(x, bsz, k_shift, k_scale, kv_norm_g, w_kv_a, kv_a_norm_g, w_kv_b, k_nope_g, k_rope_g,
               seg64, ct, st):
    t, d = x.shape
    s_len = t // bsz
    tm = ROW_TILE
    tiles = s_len // tm
    rank = kv_a_norm_g.shape[0]
    half = QK_ROPE_DIM // 2
    pad = jnp.zeros((d, LANES - QK_ROPE_DIM), F32)
    w_rope = w_kv_a[:, rank:]
    w_rope_sw = jnp.concatenate([w_rope[:, half:], w_rope[:, :half]], axis=1)
    wa = jnp.concatenate([w_kv_a[:, :rank], w_rope, pad, w_rope_sw, pad], axis=1).astype(BF16)
    wkv = w_kv_b.reshape(rank, N_HEADS, QK_NOPE_DIM + V_HEAD_DIM)
    wkn = wkv[:, :, :QK_NOPE_DIM].reshape(rank, N_HEADS * QK_NOPE_DIM).astype(BF16)
    wv = jnp.pad(wkv[:, :, QK_NOPE_DIM:], ((0, 0), (0, 0), (0, HEAD_LANES - V_HEAD_DIM)))
    wv = wv.reshape(rank, N_HEADS * HEAD_LANES).astype(BF16)
    gkn = jnp.tile(k_nope_g, N_HEADS).reshape(1, -1)
    lane_pad = lambda g: jnp.pad(g, (0, LANES - g.shape[0])).reshape(1, LANES)
    gkr = lane_pad(k_rope_g)
    gkrs = lane_pad(jnp.concatenate([k_rope_g[half:], k_rope_g[:half]]))
    full = _resident
    xspec = pl.BlockSpec((tm, d), lambda b, i: (b * tiles + i, 0))
    tab = pl.BlockSpec((tm, LANES), lambda b, i: (b * tiles + i, 0))
    per_b = pl.BlockSpec((1, 1, d), lambda b, i: (b, 0, 0))
    hspec = pl.BlockSpec((1, N_HEADS, tm, HEAD_LANES), lambda b, i: (b, 0, i, 0))
    args = (x, k_shift, k_scale, kv_norm_g.reshape(1, d), wa, kv_a_norm_g.reshape(1, rank), wkn, wv,
            gkn, gkr, gkrs, seg64, ct, st)
    in_specs = [xspec, per_b, per_b] + [full(a) for a in args[3:12]] + [tab, tab]
    return pl.pallas_call(
        _kv_kernel,
        grid=(bsz, tiles),
        in_specs=in_specs,
        out_specs=[hspec, hspec],
        out_shape=[jax.ShapeDtypeStruct((bsz, N_HEADS, s_len, HEAD_LANES), BF16)] * 2,
        compiler_params=_params("parallel", "parallel"),
        name="mla_shared_kv",
    )(*args)


def _q_kernel(x_ref, shift_ref, scale_ref, g_ref, wdq_ref, gq_ref, wn_ref, wr_ref, wrs_ref,
              gn_ref, gr_ref, grs_ref, seg64_ref, seg32_ref, ct_ref, st_ref, q_ref):
    x = x_ref[...]
    h = _modulate(_rms(x, g_ref[...]), shift_ref[0], scale_ref[0])
    cq = _rms(_dot(h.astype(BF16), wdq_ref[...]), gq_ref[...]).astype(BF16)

    qn = _dot(cq, wn_ref[...])
    ssn = _dot((qn * qn).astype(BF16), seg64_ref[...])
    qn = qn * lax.rsqrt(ssn * (1.0 / QK_NOPE_DIM) + EPS) * (gn_ref[...] * ATTN_SCALE)

    a = _dot(cq, wr_ref[...])
    asw = _dot(cq, wrs_ref[...])
    ssr = _dot((a * a).astype(BF16), seg32_ref[...])
    rstd = lax.rsqrt(ssr * (1.0 / QK_ROPE_DIM) + EPS) * ATTN_SCALE
    ct = ct_ref[...]
    st = st_ref[...] * _rope_sign(ct.shape)
    heads_per_blk = LANES // QK_ROPE_DIM
    for blk in range(N_HEADS // heads_per_blk):
        sl = slice(blk * LANES, (blk + 1) * LANES)
        qr = rstd[:, sl] * (a[:, sl] * gr_ref[:, sl] * ct + asw[:, sl] * grs_ref[:, sl] * st)
        lane = _lane_iota(qr.shape)
        for j in range(heads_per_blk):
            h_idx = blk * heads_per_blk + j
            shift = (QK_NOPE_DIM - j * QK_ROPE_DIM) % LANES
            r = pltpu.roll(qr, shift, 1) if shift else qr
            r = jnp.where(lane < QK_NOPE_DIM + QK_ROPE_DIM, r, 0.0)
            q_ref[0, h_idx] = _head_tile(qn, r, h_idx).astype(BF16)


def _queries(x, bsz, shift, scale, norm_g, w_dq, q_norm_g, w_uq, q_nope_g, q_rope_g, seg64, seg32, ct, st):
    t, d = x.shape
    s_len = t // bsz
    tm = ROW_TILE
    tiles = s_len // tm
    rank = w_dq.shape[1]
    half = QK_ROPE_DIM // 2
    wq = w_uq.reshape(rank, N_HEADS, QK_NOPE_DIM + QK_ROPE_DIM)
    wn = wq[:, :, :QK_NOPE_DIM].reshape(rank, -1).astype(BF16)
    wr = wq[:, :, QK_NOPE_DIM:].reshape(rank, -1).astype(BF16)
    wrs = jnp.concatenate([wq[:, :, QK_NOPE_DIM + half:], wq[:, :, QK_NOPE_DIM:QK_NOPE_DIM + half]], axis=2)
    wrs = wrs.reshape(rank, -1).astype(BF16)
    gn = jnp.tile(q_nope_g, N_HEADS).reshape(1, -1)
    gr = jnp.tile(q_rope_g, N_HEADS).reshape(1, -1)
    grs = jnp.tile(jnp.concatenate([q_rope_g[half:], q_rope_g[:half]]), N_HEADS).reshape(1, -1)
    full = _resident
    xspec = pl.BlockSpec((tm, d), lambda b, i: (b * tiles + i, 0))
    tab = pl.BlockSpec((tm, LANES), lambda b, i: (b * tiles + i, 0))
    per_b = pl.BlockSpec((1, 1, d), lambda b, i: (b, 0, 0))
    hspec = pl.BlockSpec((1, N_HEADS, tm, HEAD_LANES), lambda b, i: (b, 0, i, 0))
    args = (x, shift, scale, norm_g.reshape(1, d), w_dq.astype(BF16), q_norm_g.reshape(1, rank), wn, wr, wrs,
            gn, gr, grs, seg64, seg32, ct, st)
    in_specs = [xspec, per_b, per_b] + [full(a) for a in args[3:14]] + [tab, tab]
    return pl.pallas_call(
        _q_kernel,
        grid=(bsz, tiles),
        in_specs=in_specs,
        out_specs=hspec,
        out_shape=jax.ShapeDtypeStruct((bsz, N_HEADS, s_len, HEAD_LANES), BF16),
        compiler_params=_params("parallel", "parallel"),
        name="mla_queries",
    )(*args)


def _attn_kernel(q_ref, k_ref, v_ref, o_ref, *, tile):
    i = pl.program_id(2)
    nt_dims = (((1,), (1,)), ((), ()))
    row_chunk = lax.broadcasted_iota(jnp.int32, (tile, tile), 0) // CHUNK
    col_chunk = lax.broadcasted_iota(jnp.int32, (tile, tile), 1) // CHUNK
    diag_mask = row_chunk >= col_chunk
    outs = []
    for hh in range(HEADS_PER_STEP):
        q = q_ref[0, hh]

        def scores(j, hh=hh, q=q):
            k = k_ref[0, hh, pl.ds(pl.multiple_of(j * tile, tile), tile), :]
            return lax.dot_general(q, k, nt_dims, preferred_element_type=F32)

        def values(j, hh=hh):
            return v_ref[0, hh, pl.ds(pl.multiple_of(j * tile, tile), tile), :]

        s = jnp.where(diag_mask, scores(i), MASK_VALUE)
        m = jnp.max(s, axis=-1, keepdims=True)
        p = jnp.exp(s - m)
        acc = _dot(p.astype(BF16), values(i))

        def body(j, carry):
            m, acc = carry
            s = scores(j)
            m_new = jnp.maximum(m, jnp.max(s, axis=-1, keepdims=True))
            p = jnp.exp(s - m_new)
            acc = acc * jnp.exp(m - m_new) + _dot(p.astype(BF16), values(j))
            return m_new, acc

        m, acc = lax.fori_loop(0, i, body, (m, acc))
        outs.append(acc / acc[:, V_HEAD_DIM:V_HEAD_DIM + 1])
    lane = _lane_iota(outs[0].shape)
    o_ref[0] = jnp.where(lane < V_HEAD_DIM, outs[0], pltpu.roll(outs[1], V_HEAD_DIM, 1)).astype(o_ref.dtype)


def _attention(q, k, v):
    bsz, nh, s_len, hl = q.shape
    tile = ATTN_TILE
    hp = HEADS_PER_STEP
    return pl.pallas_call(
        functools.partial(_attn_kernel, tile=tile),
        grid=(bsz, nh // hp, s_len // tile),
        in_specs=[pl.BlockSpec((1, hp, tile, hl), lambda b, h, i: (b, h, i, 0)),
                  pl.BlockSpec((1, hp, s_len, hl), lambda b, h, i: (b, h, 0, 0)),
                  pl.BlockSpec((1, hp, s_len, hl), lambda b, h, i: (b, h, 0, 0))],
        out_specs=pl.BlockSpec((1, tile, hp * V_HEAD_DIM), lambda b, h, i: (b, i, h)),
        out_shape=jax.ShapeDtypeStruct((bsz, s_len, nh * V_HEAD_DIM), BF16),
        compiler_params=_params("parallel", "parallel", "arbitrary"),
        name="mla_attention",
    )(q, k, v)


def _seg_ones(n, seg):
    idx = jnp.arange(n) // seg
    return (idx[:, None] == idx[None, :]).astype(BF16)


def _ffn_weights(w_gate, w_up, w_down):
    d, f = w_gate.shape
    nf = f // FFN_CHUNK
    wg = w_gate.reshape(d, nf, FFN_CHUNK).transpose(1, 0, 2).astype(BF16)
    wu = w_up.reshape(d, nf, FFN_CHUNK).transpose(1, 0, 2).astype(BF16)
    wd = w_down.reshape(nf, FFN_CHUNK, d).astype(BF16)
    return wg, wu, wd


def kernel(x, c, positions, ada_w, ada_b, norm1_g, norm2_g, ffn_w_gate, ffn_w_up, ffn_w_down, s5_lam_re, s5_lam_im, s5_log_dt, s5_b_re, s5_b_im, s5_c_re, s5_c_im, s5_d, s5_w_glu, s5_b_glu, kv_ada_w, kv_ada_b, kv_norm_g, w_kv_a, kv_a_norm_g, w_kv_b, k_nope_norm_g, k_rope_norm_g, mla_w_dq, mla_q_norm_g, mla_w_uq, mla_q_nope_norm_g, mla_q_rope_norm_g, mla_w_o):
    bsz, s_len, d = x.shape
    depth = ada_w.shape[0]
    n_a = s5_lam_re.shape[0]
    t = bsz * s_len

    mods = _mods(c, ada_w, ada_b).reshape(depth, bsz, 6, d)
    kv_mods = _mods(c, kv_ada_w[None], kv_ada_b[None]).reshape(bsz, 2, d)
    cos_t, sin_t = _rope_tables(positions)

    ab_re, ab_im, bb_re, bb_im = _s5_discretise(s5_lam_re, s5_lam_im, s5_log_dt, s5_b_re, s5_b_im)

    xt = x.transpose(1, 0, 2).reshape(t, d)
    for l in range(n_a):
        m = mods[l]
        bre = _slab_block_diag(bb_re[l].transpose(1, 0, 2))
        bim = _slab_block_diag(bb_im[l].transpose(1, 0, 2))
        bblk = jnp.concatenate([bre, bim], axis=2).astype(BF16)
        cre = _slab_block_diag(s5_c_re[l].transpose(0, 2, 1)).astype(BF16)
        cim = _slab_block_diag(s5_c_im[l].transpose(0, 2, 1)).astype(BF16)
        xt = _s5_mixer(xt, m[:, 0], m[:, 1], m[:, 2], norm1_g[l], bblk, cre, cim,
                       ab_re[l].reshape(-1), ab_im[l].reshape(-1), s5_d[l],
                       s5_w_glu[l].astype(BF16), s5_b_glu[l], bsz)
        wg, wu, wd = _ffn_weights(ffn_w_gate[l], ffn_w_up[l], ffn_w_down[l])
        xt = _ffn(xt, m[:, 3], m[:, 4], m[:, 5], norm2_g[l], wg, wu, wd)
    xb = xt.reshape(s_len, bsz, d).transpose(1, 0, 2).reshape(t, d)

    seg64 = _seg_ones(N_HEADS * QK_NOPE_DIM, QK_NOPE_DIM)
    seg32 = _seg_ones(N_HEADS * QK_ROPE_DIM, QK_ROPE_DIM)
    per_b = lambda v: v.reshape(bsz, 1, d)
    k_all, v_all = _shared_kv(xb, bsz, per_b(kv_mods[:, 0]), per_b(kv_mods[:, 1]), kv_norm_g, w_kv_a,
                              kv_a_norm_g, w_kv_b, k_nope_norm_g, k_rope_norm_g, seg64, cos_t, sin_t)
    for l in range(n_a, depth):
        j = l - n_a
        m = mods[l]
        q = _queries(xb, bsz, per_b(m[:, 0]), per_b(m[:, 1]), norm1_g[l], mla_w_dq[j], mla_q_norm_g[j],
                     mla_w_uq[j], mla_q_nope_norm_g[j], mla_q_rope_norm_g[j], seg64, seg32, cos_t, sin_t)
        o = _attention(q, k_all, v_all).reshape(t, N_HEADS * V_HEAD_DIM)
        wg, wu, wd = _ffn_weights(ffn_w_gate[l], ffn_w_up[l], ffn_w_down[l])
        xb = _ffn(xb, per_b(m[:, 3]), per_b(m[:, 4]), per_b(m[:, 5]), norm2_g[l], wg, wu, wd,
                  attn=(o, mla_w_o[j].astype(BF16), per_b(m[:, 2])))
    return xb.reshape(bsz, s_len, d)
```

```python
import functools
import math

import jax
import jax.numpy as jnp
from jax import lax
from jax.experimental import pallas as pl
from jax.experimental.pallas import tpu as pltpu

F32 = jnp.float32
BF16 = jnp.bfloat16

CHUNK = 64
SSM_GROUP = 16
SSM_STATE = 64
N_HEADS = 16
QK_NOPE_DIM = 64
QK_ROPE_DIM = 32
V_HEAD_DIM = 64
ROPE_THETA = 10000.0
ATTN_SCALE = 1.0 / math.sqrt(QK_NOPE_DIM + QK_ROPE_DIM)
Q_SCALE = ATTN_SCALE * math.log2(math.e)
EPS = 1e-6
MASK_VALUE = -1e30

LANES = 128
SUBLANES = 8
HEAD_LANES = 128
SLAB_GROUPS = LANES // SSM_GROUP
VMEM_LIMIT = 56 * 1024 * 1024

ROW_TILE = 512
S5_TIME_TILE = 32
SCAN_LANES = 512
FFN_CHUNK = 256
ATTN_TILE = 512
HEADS_PER_STEP = 2


def _params(*sem):
    return pltpu.CompilerParams(dimension_semantics=sem, vmem_limit_bytes=VMEM_LIMIT)


def _rms(x, g):
    return x * lax.rsqrt(jnp.mean(x * x, axis=-1, keepdims=True) + EPS) * g


def _rowwise(fn, a, *mods):
    r = mods[0].shape[0]
    if r == 1:
        return fn(a, *mods)
    rows, d = a.shape
    out = fn(a.reshape(rows // r, r, d), *[m[None] for m in mods])
    return out.reshape(rows, d)


def _modulate(h, shift, scale):
    return _rowwise(lambda a, sh, sc: a * (1.0 + sc) + sh, h, shift, scale)


def _gated_add(x, gate, upd):
    return x + _rowwise(lambda a, g: a * g, upd, gate)


def _dot(a, b):
    return jnp.dot(a, b, preferred_element_type=F32)


def _resident(a):
    zeros = (0,) * a.ndim
    return pl.BlockSpec(a.shape, lambda *_: zeros, pipeline_mode=pl.Buffered(1))


def _lane_iota(shape):
    return lax.broadcasted_iota(jnp.int32, shape, len(shape) - 1)


def _mods_kernel(c_ref, w_ref, b_ref, o_ref):
    c = c_ref[...]
    ca = c * jax.nn.sigmoid(c)
    o_ref[0] = jnp.dot(ca, w_ref[0], preferred_element_type=F32,
                       precision=lax.Precision.HIGHEST) + b_ref[0]


def _mods(c, w, b, tn=2048):
    nl, d, n = w.shape
    bsz = c.shape[0]
    return pl.pallas_call(
        _mods_kernel,
        grid=(nl, n // tn),
        in_specs=[pl.BlockSpec((bsz, d), lambda l, j: (0, 0)),
                  pl.BlockSpec((1, d, tn), lambda l, j: (l, 0, j)),
                  pl.BlockSpec((1, 1, tn), lambda l, j: (l, 0, j))],
        out_specs=pl.BlockSpec((1, bsz, tn), lambda l, j: (l, 0, j)),
        out_shape=jax.ShapeDtypeStruct((nl, bsz, n), F32),
        compiler_params=_params("parallel", "parallel"),
        name="adaln_mods",
    )(c, w, b.reshape(nl, 1, n))


def _rope_kernel(pos_ref, inv_ref, cos_ref, sin_ref):
    ang = inv_ref[...] * pos_ref[...]
    cos_ref[...] = jnp.cos(ang)
    sin_ref[...] = jnp.sin(ang)


def _rope_tables(positions, tn=4096):
    t = positions.size
    half = QK_ROPE_DIM // 2
    inv = 1.0 / (ROPE_THETA ** (jnp.arange(0, QK_ROPE_DIM, 2, dtype=F32) / QK_ROPE_DIM))
    pos = positions.astype(F32).reshape(1, t)
    cos, sin = pl.pallas_call(
        _rope_kernel,
        grid=(t // tn,),
        in_specs=[pl.BlockSpec((1, tn), lambda i: (0, i)),
                  pl.BlockSpec((half, 1), lambda i: (0, 0))],
        out_specs=[pl.BlockSpec((half, tn), lambda i: (0, i))] * 2,
        out_shape=[jax.ShapeDtypeStruct((half, t), F32)] * 2,
        compiler_params=_params("parallel"),
        name="rope_tables",
    )(pos, inv.reshape(half, 1))
    reps = LANES // half
    return jnp.tile(cos.T, (1, reps)), jnp.tile(sin.T, (1, reps))


def _rope_sign(shape):
    lane = _lane_iota(shape)
    return jnp.where((lane % QK_ROPE_DIM) < (QK_ROPE_DIM // 2), -1.0, 1.0).astype(F32)


def _s5_disc_kernel(lr_ref, li_ref, ldt_ref, br_ref, bi_ref, abr_ref, abi_ref, bbr_ref, bbi_ref):
    lr = lr_ref[0]
    li = li_ref[0]
    dt = jnp.exp(ldt_ref[0])
    mag = jnp.exp(lr * dt)
    ab_re = mag * jnp.cos(li * dt)
    ab_im = mag * jnp.sin(li * dt)
    den = lr * lr + li * li
    nr = ab_re - 1.0
    ni = ab_im
    f_re = (nr * lr + ni * li) / den
    f_im = (ni * lr - nr * li) / den
    br = br_ref[0]
    bi = bi_ref[0]
    abr_ref[0] = ab_re
    abi_ref[0] = ab_im
    bbr_ref[0] = f_re[None] * br - f_im[None] * bi
    bbi_ref[0] = f_re[None] * bi + f_im[None] * br


def _s5_discretise(lam_re, lam_im, log_dt, b_re, b_im):
    na, g, n = lam_re.shape
    p = b_re.shape[-1]
    bt_re = b_re.transpose(0, 3, 1, 2)
    bt_im = b_im.transpose(0, 3, 1, 2)
    m2 = pl.BlockSpec((1, g, n), lambda l: (l, 0, 0))
    m3 = pl.BlockSpec((1, p, g, n), lambda l: (l, 0, 0, 0))
    return pl.pallas_call(
        _s5_disc_kernel,
        grid=(na,),
        in_specs=[m2, m2, pl.BlockSpec((1, g, 1), lambda l: (l, 0, 0)), m3, m3],
        out_specs=[m2, m2, m3, m3],
        out_shape=[jax.ShapeDtypeStruct((na, g, n), F32)] * 2
        + [jax.ShapeDtypeStruct((na, p, g, n), F32)] * 2,
        compiler_params=_params("parallel"),
        name="s5_discretise",
    )(lam_re, lam_im, log_dt.reshape(na, g, 1), bt_re, bt_im)


def _slab_block_diag(w_gab):
    g, a, b = w_gab.shape
    ns = g // SLAB_GROUPS
    w = w_gab.reshape(ns, SLAB_GROUPS, a, b)
    eye = jnp.eye(SLAB_GROUPS, dtype=w.dtype)
    return jnp.einsum("kgab,gh->kgahb", w, eye).reshape(ns, SLAB_GROUPS * a, SLAB_GROUPS * b)


def _s5_mixer_kernel(x_ref, shift_ref, scale_ref, gate_ref, g_ref, bblk_ref, cre_ref, cim_ref,
                     are_ref, aim_ref, d_ref, wglu_ref, bglu_ref, o_ref,
                     bu_re, bu_im, st_re, st_im, y_scr, *, ts, n_slabs, slab_state):
    @pl.when(pl.program_id(0) == 0)
    def _():
        st_re[...] = jnp.zeros_like(st_re)
        st_im[...] = jnp.zeros_like(st_im)

    x = x_ref[...]
    h = _modulate(_rms(x, g_ref[...]), shift_ref[...], scale_ref[...])
    hb = h.astype(BF16)

    for k in range(n_slabs):
        bu = _dot(hb[:, k * LANES:(k + 1) * LANES], bblk_ref[k])
        bu_re[:, k * slab_state:(k + 1) * slab_state] = bu[:, :slab_state]
        bu_im[:, k * slab_state:(k + 1) * slab_state] = bu[:, slab_state:]

    n_state = n_slabs * slab_state
    for cb in range(n_state // SCAN_LANES):
        sl = slice(cb * SCAN_LANES, (cb + 1) * SCAN_LANES)
        ar = jnp.broadcast_to(are_ref[:, sl], (SUBLANES, SCAN_LANES))
        ai = jnp.broadcast_to(aim_ref[:, sl], (SUBLANES, SCAN_LANES))

        def step(i, carry, sl=sl, ar=ar, ai=ai):
            sr, si = carry
            r0 = pl.multiple_of(i * SUBLANES, SUBLANES)
            nr = ar * sr - ai * si + bu_re[pl.ds(r0, SUBLANES), sl]
            ni = ar * si + ai * sr + bu_im[pl.ds(r0, SUBLANES), sl]
            bu_re[pl.ds(r0, SUBLANES), sl] = nr
            bu_im[pl.ds(r0, SUBLANES), sl] = ni
            return nr, ni

        sr, si = lax.fori_loop(0, ts, step, (st_re[:, sl], st_im[:, sl]), unroll=4)
        st_re[:, sl] = sr
        st_im[:, sl] = si

    for k in range(n_slabs):
        ssl = slice(k * slab_state, (k + 1) * slab_state)
        yk = _dot(bu_re[:, ssl].astype(BF16), cre_ref[k]) - _dot(bu_im[:, ssl].astype(BF16), cim_ref[k])
        y_scr[:, k * LANES:(k + 1) * LANES] = yk

    y = y_scr[...] + d_ref[...] * h
    gl = jax.nn.gelu(y)
    z = _dot(gl.astype(BF16), wglu_ref[...]) + bglu_ref[...]
    mix = gl * jax.nn.sigmoid(z)
    o_ref[...] = _gated_add(x, gate_ref[...], mix)


def _s5_mixer(x_tb, shift, scale, gate, norm_g, bblk, cre, cim, a_re, a_im, d_skip, w_glu, b_glu, bsz):
    t, d = x_tb.shape
    ts = S5_TIME_TILE
    rows = ts * bsz
    n_slabs, _, two_state = bblk.shape
    slab_state = two_state // 2
    n_state = n_slabs * slab_state
    full = _resident
    row = lambda a: a.reshape(1, -1)
    args = (x_tb, shift, scale, gate, row(norm_g), bblk, cre, cim, row(a_re), row(a_im),
            row(d_skip), w_glu, row(b_glu))
    return pl.pallas_call(
        functools.partial(_s5_mixer_kernel, ts=ts, n_slabs=n_slabs, slab_state=slab_state),
        grid=(t // rows,),
        in_specs=[pl.BlockSpec((rows, d), lambda i: (i, 0))] + [full(a) for a in args[1:]],
        out_specs=pl.BlockSpec((rows, d), lambda i: (i, 0)),
        out_shape=jax.ShapeDtypeStruct((t, d), F32),
        scratch_shapes=[pltpu.VMEM((rows, n_state), F32), pltpu.VMEM((rows, n_state), F32),
                        pltpu.VMEM((bsz, n_state), F32), pltpu.VMEM((bsz, n_state), F32),
                        pltpu.VMEM((rows, d), F32)],
        compiler_params=_params("arbitrary"),
        name="s5_mixer",
    )(*args)


def _ffn_kernel(*refs, n_chunks, with_attn):
    if with_attn:
        (x_ref, o_in_ref, wo_ref, gate1_ref, shift_ref, scale_ref, gate2_ref, g_ref,
         wg_ref, wu_ref, wd_ref, out_ref, h_scr, acc_scr) = refs
        x = x_ref[...] + gate1_ref[0] * _dot(o_in_ref[...], wo_ref[...])
        shift, scale, gate2 = shift_ref[0], scale_ref[0], gate2_ref[0]
    else:
        (x_ref, shift_ref, scale_ref, gate2_ref, g_ref,
         wg_ref, wu_ref, wd_ref, out_ref, h_scr, acc_scr) = refs
        x = x_ref[...]
        shift, scale, gate2 = shift_ref[...], scale_ref[...], gate2_ref[...]

    h_scr[...] = _modulate(_rms(x, g_ref[...]), shift, scale).astype(BF16)
    acc_scr[...] = jnp.zeros_like(acc_scr)

    def chunk(f, carry):
        hb = h_scr[...]
        gt = _dot(hb, wg_ref[f])
        up = _dot(hb, wu_ref[f])
        act = (gt * jax.nn.sigmoid(gt) * up).astype(BF16)
        acc_scr[...] += _dot(act, wd_ref[f])
        return carry

    lax.fori_loop(0, n_chunks, chunk, 0)
    out_ref[...] = _gated_add(x, gate2, acc_scr[...])


def _ffn(x, shift, scale, gate2, norm_g, wg, wu, wd, attn=None):
    t, d = x.shape
    n_chunks = wg.shape[0]
    tm = ROW_TILE
    full = _resident
    xspec = pl.BlockSpec((tm, d), lambda i: (i, 0))
    g2 = norm_g.reshape(1, d)
    if attn is None:
        args = (x, shift, scale, gate2, g2, wg, wu, wd)
        in_specs = [xspec] + [full(a) for a in args[1:]]
    else:
        o_in, w_o, gate1 = attn
        tiles_per_batch = (t // gate1.shape[0]) // tm
        per_b = pl.BlockSpec((1, 1, d), lambda i: (i // tiles_per_batch, 0, 0))
        args = (x, o_in, w_o, gate1, shift, scale, gate2, g2, wg, wu, wd)
        in_specs = [xspec, xspec, full(w_o), per_b, per_b, per_b, per_b, full(g2), full(wg), full(wu), full(wd)]
    return pl.pallas_call(
        functools.partial(_ffn_kernel, n_chunks=n_chunks, with_attn=attn is not None),
        grid=(t // tm,),
        in_specs=in_specs,
        out_specs=xspec,
        out_shape=jax.ShapeDtypeStruct((t, d), F32),
        scratch_shapes=[pltpu.VMEM((tm, d), BF16), pltpu.VMEM((tm, d), F32)],
        compiler_params=_params("parallel"),
        name="ffn_attn_out" if attn is not None else "ffn",
    )(*args)


def _head_tile(nope, rope_at_64, h):
    blk = nope[:, (h // 2) * LANES:(h // 2 + 1) * LANES]
    if h % 2:
        blk = pltpu.roll(blk, QK_NOPE_DIM, 1)
    return jnp.where(_lane_iota(blk.shape) < QK_NOPE_DIM, blk, rope_at_64)


def _kv_kernel(x_ref, shift_ref, scale_ref, g_ref, wa_ref, ga_ref, wkn_ref, wv_ref, gkn_ref,
               gkr_ref, gkrs_ref, seg_ref, ct_ref, st_ref, k_ref, v_ref):
    x = x_ref[...]
    hk = _modulate(_rms(x, g_ref[...]), shift_ref[0], scale_ref[0])
    kva = _dot(hk.astype(BF16), wa_ref[...])
    rank = ga_ref.shape[1]
    cb = _rms(kva[:, :rank], ga_ref[...]).astype(BF16)

    kn = _dot(cb, wkn_ref[...])
    ss = _dot((kn * kn).astype(BF16), seg_ref[...])
    kn = kn * lax.rsqrt(ss * (1.0 / QK_NOPE_DIM) + EPS) * gkn_ref[...]

    a = kva[:, rank:rank + LANES]
    asw = kva[:, rank + LANES:rank + 2 * LANES]
    rstd = lax.rsqrt(jnp.sum(a * a, axis=-1, keepdims=True) * (1.0 / QK_ROPE_DIM) + EPS)
    sgn = _rope_sign(a.shape)
    kr = rstd * (a * gkr_ref[...] * ct_ref[...] + asw * gkrs_ref[...] * (st_ref[...] * sgn))
    kr64 = pltpu.roll(kr, QK_NOPE_DIM, 1)

    v = _dot(cb, wv_ref[...])
    ones_col = (_lane_iota((1, LANES)) == V_HEAD_DIM).astype(F32)
    for h in range(N_HEADS):
        k_ref[0, h] = _head_tile(kn, kr64, h).astype(BF16)
        v_ref[0, h] = (v[:, h * LANES:(h + 1) * LANES] + ones_col).astype(BF16)


def _shared_kv(x, bsz, k_shift, k_scale, kv_norm_g, w_kv_a, kv_a_norm_g, w_kv_b, k_nope_g, k_rope_g,
               seg64, ct, st):
    t, d = x.shape
    s_len = t // bsz
    tm = ROW_TILE
    tiles = s_len // tm
    rank = kv_a_norm_g.shape[0]
    half = QK_ROPE_DIM // 2
    pad = jnp.zeros((d, LANES - QK_ROPE_DIM), F32)
    w_rope = w_kv_a[:, rank:]
    w_rope_sw = jnp.concatenate([w_rope[:, half:], w_rope[:, :half]], axis=1)
    wa = jnp.concatenate([w_kv_a[:, :rank], w_rope, pad, w_rope_sw, pad], axis=1).astype(BF16)
    wkv = w_kv_b.reshape(rank, N_HEADS, QK_NOPE_DIM + V_HEAD_DIM)
    wkn = wkv[:, :, :QK_NOPE_DIM].reshape(rank, N_HEADS * QK_NOPE_DIM).astype(BF16)
    wv = jnp.pad(wkv[:, :, QK_NOPE_DIM:], ((0, 0), (0, 0), (0, HEAD_LANES - V_HEAD_DIM)))
    wv = wv.reshape(rank, N_HEADS * HEAD_LANES).astype(BF16)
    gkn = jnp.tile(k_nope_g, N_HEADS).reshape(1, -1)
    lane_pad = lambda g: jnp.pad(g, (0, LANES - g.shape[0])).reshape(1, LANES)
    gkr = lane_pad(k_rope_g)
    gkrs = lane_pad(jnp.concatenate([k_rope_g[half:], k_rope_g[:half]]))
    full = _resident
    xspec = pl.BlockSpec((tm, d), lambda b, i: (b * tiles + i, 0))
    tab = pl.BlockSpec((tm, LANES), lambda b, i: (b * tiles + i, 0))
    per_b = pl.BlockSpec((1, 1, d), lambda b, i: (b, 0, 0))
    hspec = pl.BlockSpec((1, N_HEADS, tm, HEAD_LANES), lambda b, i: (b, 0, i, 0))
    args = (x, k_shift, k_scale, kv_norm_g.reshape(1, d), wa, kv_a_norm_g.reshape(1, rank), wkn, wv,
            gkn, gkr, gkrs, seg64, ct, st)
    in_specs = [xspec, per_b, per_b] + [full(a) for a in args[3:12]] + [tab, tab]
    return pl.pallas_call(
        _kv_kernel,
        grid=(bsz, tiles),
        in_specs=in_specs,
        out_specs=[hspec, hspec],
        out_shape=[jax.ShapeDtypeStruct((bsz, N_HEADS, s_len, HEAD_LANES), BF16)] * 2,
        compiler_params=_params("parallel", "parallel"),
        name="mla_shared_kv",
    )(*args)


def _q_kernel(x_ref, shift_ref, scale_ref, g_ref, wdq_ref, gq_ref, wn_ref, wr_ref, wrs_ref,
              gn_ref, gr_ref, grs_ref, seg64_ref, seg32_ref, ct_ref, st_ref, q_ref):
    x = x_ref[...]
    h = _modulate(_rms(x, g_ref[...]), shift_ref[0], scale_ref[0])
    cq = _rms(_dot(h.astype(BF16), wdq_ref[...]), gq_ref[...]).astype(BF16)

    qn = _dot(cq, wn_ref[...])
    ssn = _dot((qn * qn).astype(BF16), seg64_ref[...])
    qn = qn * lax.rsqrt(ssn * (1.0 / QK_NOPE_DIM) + EPS) * (gn_ref[...] * Q_SCALE)

    a = _dot(cq, wr_ref[...])
    asw = _dot(cq, wrs_ref[...])
    ssr = _dot((a * a).astype(BF16), seg32_ref[...])
    rstd = lax.rsqrt(ssr * (1.0 / QK_ROPE_DIM) + EPS) * Q_SCALE
    ct = ct_ref[...]
    st = st_ref[...] * _rope_sign(ct.shape)
    heads_per_blk = LANES // QK_ROPE_DIM
    for blk in range(N_HEADS // heads_per_blk):
        sl = slice(blk * LANES, (blk + 1) * LANES)
        qr = rstd[:, sl] * (a[:, sl] * gr_ref[:, sl] * ct + asw[:, sl] * grs_ref[:, sl] * st)
        lane = _lane_iota(qr.shape)
        for j in range(heads_per_blk):
            h_idx = blk * heads_per_blk + j
            shift = (QK_NOPE_DIM - j * QK_ROPE_DIM) % LANES
            r = pltpu.roll(qr, shift, 1) if shift else qr
            r = jnp.where(lane < QK_NOPE_DIM + QK_ROPE_DIM, r, 0.0)
            q_ref[0, h_idx] = _head_tile(qn, r, h_idx).astype(BF16)


def _queries(x, bsz, shift, scale, norm_g, w_dq, q_norm_g, w_uq, q_nope_g, q_rope_g, seg64, seg32, ct, st):
    t, d = x.shape
    s_len = t // bsz
    tm = ROW_TILE
    tiles = s_len // tm
    rank = w_dq.shape[1]
    half = QK_ROPE_DIM // 2
    wq = w_uq.reshape(rank, N_HEADS, QK_NOPE_DIM + QK_ROPE_DIM)
    wn = wq[:, :, :QK_NOPE_DIM].reshape(rank, -1).astype(BF16)
    wr = wq[:, :, QK_NOPE_DIM:].reshape(rank, -1).astype(BF16)
    wrs = jnp.concatenate([wq[:, :, QK_NOPE_DIM + half:], wq[:, :, QK_NOPE_DIM:QK_NOPE_DIM + half]], axis=2)
    wrs = wrs.reshape(rank, -1).astype(BF16)
    gn = jnp.tile(q_nope_g, N_HEADS).reshape(1, -1)
    gr = jnp.tile(q_rope_g, N_HEADS).reshape(1, -1)
    grs = jnp.tile(jnp.concatenate([q_rope_g[half:], q_rope_g[:half]]), N_HEADS).reshape(1, -1)
    full = _resident
    xspec = pl.BlockSpec((tm, d), lambda b, i: (b * tiles + i, 0))
    tab = pl.BlockSpec((tm, LANES), lambda b, i: (b * tiles + i, 0))
    per_b = pl.BlockSpec((1, 1, d), lambda b, i: (b, 0, 0))
    hspec = pl.BlockSpec((1, N_HEADS, tm, HEAD_LANES), lambda b, i: (b, 0, i, 0))
    args = (x, shift, scale, norm_g.reshape(1, d), w_dq.astype(BF16), q_norm_g.reshape(1, rank), wn, wr, wrs,
            gn, gr, grs, seg64, seg32, ct, st)
    in_specs = [xspec, per_b, per_b] + [full(a) for a in args[3:14]] + [tab, tab]
    return pl.pallas_call(
        _q_kernel,
        grid=(bsz, tiles),
        in_specs=in_specs,
        out_specs=hspec,
        out_shape=jax.ShapeDtypeStruct((bsz, N_HEADS, s_len, HEAD_LANES), BF16),
        compiler_params=_params("parallel", "parallel"),
        name="mla_queries",
    )(*args)


def _attn_kernel(q_ref, k_ref, v_ref, o_ref, *, tile):
    i = pl.program_id(2)
    nt_dims = (((1,), (1,)), ((), ()))
    row_chunk = lax.broadcasted_iota(jnp.int32, (tile, tile), 0) // CHUNK
    col_chunk = lax.broadcasted_iota(jnp.int32, (tile, tile), 1) // CHUNK
    diag_mask = row_chunk >= col_chunk
    qs = [q_ref[0, hh] for hh in range(HEADS_PER_STEP)]

    def tile_update(j, carry, masked):
        off = pl.multiple_of(j * tile, tile)
        new = []
        for hh in range(HEADS_PER_STEP):
            m, acc = carry[hh]
            s = lax.dot_general(qs[hh], k_ref[0, hh, pl.ds(off, tile), :], nt_dims,
                                preferred_element_type=F32)
            if masked:
                s = jnp.where(diag_mask, s, MASK_VALUE)
            m_new = jnp.maximum(m, jnp.max(s, axis=-1, keepdims=True))
            p = jnp.exp2(s - m_new)
            pv = _dot(p.astype(BF16), v_ref[0, hh, pl.ds(off, tile), :])
            new.append((m_new, acc * jnp.exp2(m - m_new) + pv))
        return tuple(new)

    init = tuple((jnp.full((tile, 1), MASK_VALUE, F32), jnp.zeros((tile, HEAD_LANES), F32))
                 for _ in range(HEADS_PER_STEP))
    carry = tile_update(i, init, True)
    carry = lax.fori_loop(0, i, lambda j, c: tile_update(j, c, False), carry)
    outs = [acc / acc[:, V_HEAD_DIM:V_HEAD_DIM + 1] for _, acc in carry]
    lane = _lane_iota(outs[0].shape)
    o_ref[0] = jnp.where(lane < V_HEAD_DIM, outs[0], pltpu.roll(outs[1], V_HEAD_DIM, 1)).astype(o_ref.dtype)


def _attention(q, k, v):
    bsz, nh, s_len, hl = q.shape
    tile = ATTN_TILE
    hp = HEADS_PER_STEP
    return pl.pallas_call(
        functools.partial(_attn_kernel, tile=tile),
        grid=(bsz, nh // hp, s_len // tile),
        in_specs=[pl.BlockSpec((1, hp, tile, hl), lambda b, h, i: (b, h, i, 0)),
                  pl.BlockSpec((1, hp, s_len, hl), lambda b, h, i: (b, h, 0, 0)),
                  pl.BlockSpec((1, hp, s_len, hl), lambda b, h, i: (b, h, 0, 0))],
        out_specs=pl.BlockSpec((1, tile, hp * V_HEAD_DIM), lambda b, h, i: (b, i, h)),
        out_shape=jax.ShapeDtypeStruct((bsz, s_len, nh * V_HEAD_DIM), BF16),
        compiler_params=_params("parallel", "parallel", "arbitrary"),
        name="mla_attention",
    )(q, k, v)


def _seg_ones(n, seg):
    idx = jnp.arange(n) // seg
    return (idx[:, None] == idx[None, :]).astype(BF16)


def _ffn_weights(w_gate, w_up, w_down):
    d, f = w_gate.shape
    nf = f // FFN_CHUNK
    wg = w_gate.reshape(d, nf, FFN_CHUNK).transpose(1, 0, 2).astype(BF16)
    wu = w_up.reshape(d, nf, FFN_CHUNK).transpose(1, 0, 2).astype(BF16)
    wd = w_down.reshape(nf, FFN_CHUNK, d).astype(BF16)
    return wg, wu, wd


def kernel(x, c, positions, ada_w, ada_b, norm1_g, norm2_g, ffn_w_gate, ffn_w_up, ffn_w_down, s5_lam_re, s5_lam_im, s5_log_dt, s5_b_re, s5_b_im, s5_c_re, s5_c_im, s5_d, s5_w_glu, s5_b_glu, kv_ada_w, kv_ada_b, kv_norm_g, w_kv_a, kv_a_norm_g, w_kv_b, k_nope_norm_g, k_rope_norm_g, mla_w_dq, mla_q_norm_g, mla_w_uq, mla_q_nope_norm_g, mla_q_rope_norm_g, mla_w_o):
    bsz, s_len, d = x.shape
    depth = ada_w.shape[0]
    n_a = s5_lam_re.shape[0]
    t = bsz * s_len

    mods = _mods(c, ada_w, ada_b).reshape(depth, bsz, 6, d)
    kv_mods = _mods(c, kv_ada_w[None], kv_ada_b[None]).reshape(bsz, 2, d)
    cos_t, sin_t = _rope_tables(positions)

    ab_re, ab_im, bb_re, bb_im = _s5_discretise(s5_lam_re, s5_lam_im, s5_log_dt, s5_b_re, s5_b_im)

    xt = x.transpose(1, 0, 2).reshape(t, d)
    for l in range(n_a):
        m = mods[l]
        bre = _slab_block_diag(bb_re[l].transpose(1, 0, 2))
        bim = _slab_block_diag(bb_im[l].transpose(1, 0, 2))
        bblk = jnp.concatenate([bre, bim], axis=2).astype(BF16)
        cre = _slab_block_diag(s5_c_re[l].transpose(0, 2, 1)).astype(BF16)
        cim = _slab_block_diag(s5_c_im[l].transpose(0, 2, 1)).astype(BF16)
        xt = _s5_mixer(xt, m[:, 0], m[:, 1], m[:, 2], norm1_g[l], bblk, cre, cim,
                       ab_re[l].reshape(-1), ab_im[l].reshape(-1), s5_d[l],
                       s5_w_glu[l].astype(BF16), s5_b_glu[l], bsz)
        wg, wu, wd = _ffn_weights(ffn_w_gate[l], ffn_w_up[l], ffn_w_down[l])
        xt = _ffn(xt, m[:, 3], m[:, 4], m[:, 5], norm2_g[l], wg, wu, wd)
    xb = xt.reshape(s_len, bsz, d).transpose(1, 0, 2).reshape(t, d)

    seg64 = _seg_ones(N_HEADS * QK_NOPE_DIM, QK_NOPE_DIM)
    seg32 = _seg_ones(N_HEADS * QK_ROPE_DIM, QK_ROPE_DIM)
    per_b = lambda v: v.reshape(bsz, 1, d)
    k_all, v_all = _shared_kv(xb, bsz, per_b(kv_mods[:, 0]), per_b(kv_mods[:, 1]), kv_norm_g, w_kv_a,
                              kv_a_norm_g, w_kv_b, k_nope_norm_g, k_rope_norm_g, seg64, cos_t, sin_t)
    for l in range(n_a, depth):
        j = l - n_a
        m = mods[l]
        q = _queries(xb, bsz, per_b(m[:, 0]), per_b(m[:, 1]), norm1_g[l], mla_w_dq[j], mla_q_norm_g[j],
                     mla_w_uq[j], mla_q_nope_norm_g[j], mla_q_rope_norm_g[j], seg64, seg32, cos_t, sin_t)
        o = _attention(q, k_all, v_all).reshape(t, N_HEADS * V_HEAD_DIM)
        wg, wu, wd = _ffn_weights(ffn_w_gate[l], ffn_w_up[l], ffn_w_down[l])
        xb = _ffn(xb, per_b(m[:, 3]), per_b(m[:, 4]), per_b(m[:, 5]), norm2_g[l], wg, wu, wd,
                  attn=(o, mla_w_o[j].astype(BF16), per_b(m[:, 2])))
    return xb.reshape(bsz, s_len, d)
```

```python
import functools
import math

import jax
import jax.numpy as jnp
from jax import lax
from jax.experimental import pallas as pl
from jax.experimental.pallas import tpu as pltpu

F32 = jnp.float32
BF16 = jnp.bfloat16

CHUNK = 64
SSM_GROUP = 16
SSM_STATE = 64
N_HEADS = 16
QK_NOPE_DIM = 64
QK_ROPE_DIM = 32
V_HEAD_DIM = 64
ROPE_THETA = 10000.0
ATTN_SCALE = 1.0 / math.sqrt(QK_NOPE_DIM + QK_ROPE_DIM)
Q_SCALE = ATTN_SCALE * math.log2(math.e)
EPS = 1e-6
MASK_VALUE = -1e30
SHIFT_LANE = QK_NOPE_DIM + QK_ROPE_DIM
BOUND_SLACK = 1.01
MAX_FIXED_SHIFT_RANGE = 60.0

LANES = 128
SUBLANES = 8
HEAD_LANES = 128
SLAB_GROUPS = LANES // SSM_GROUP
VMEM_LIMIT = 56 * 1024 * 1024

ROW_TILE = 512
S5_TIME_TILE = 32
SCAN_LANES = 512
FFN_CHUNK = 256
ATTN_TILE = 512
HEADS_PER_STEP = 2


def _params(*sem):
    return pltpu.CompilerParams(dimension_semantics=sem, vmem_limit_bytes=VMEM_LIMIT)


def _rms(x, g):
    return x * lax.rsqrt(jnp.mean(x * x, axis=-1, keepdims=True) + EPS) * g


def _rowwise(fn, a, *mods):
    r = mods[0].shape[0]
    if r == 1:
        return fn(a, *mods)
    rows, d = a.shape
    out = fn(a.reshape(rows // r, r, d), *[m[None] for m in mods])
    return out.reshape(rows, d)


def _modulate(h, shift, scale):
    return _rowwise(lambda a, sh, sc: a * (1.0 + sc) + sh, h, shift, scale)


def _gated_add(x, gate, upd):
    return x + _rowwise(lambda a, g: a * g, upd, gate)


def _dot(a, b):
    return jnp.dot(a, b, preferred_element_type=F32)


def _resident(a):
    zeros = (0,) * a.ndim
    return pl.BlockSpec(a.shape, lambda *_: zeros, pipeline_mode=pl.Buffered(1))


def _lane_iota(shape):
    return lax.broadcasted_iota(jnp.int32, shape, len(shape) - 1)


def _mods_kernel(c_ref, w_ref, b_ref, o_ref):
    c = c_ref[...]
    ca = c * jax.nn.sigmoid(c)
    o_ref[0] = jnp.dot(ca, w_ref[0], preferred_element_type=F32,
                       precision=lax.Precision.HIGHEST) + b_ref[0]


def _mods(c, w, b, tn=2048):
    nl, d, n = w.shape
    bsz = c.shape[0]
    return pl.pallas_call(
        _mods_kernel,
        grid=(nl, n // tn),
        in_specs=[pl.BlockSpec((bsz, d), lambda l, j: (0, 0)),
                  pl.BlockSpec((1, d, tn), lambda l, j: (l, 0, j)),
                  pl.BlockSpec((1, 1, tn), lambda l, j: (l, 0, j))],
        out_specs=pl.BlockSpec((1, bsz, tn), lambda l, j: (l, 0, j)),
        out_shape=jax.ShapeDtypeStruct((nl, bsz, n), F32),
        compiler_params=_params("parallel", "parallel"),
        name="adaln_mods",
    )(c, w, b.reshape(nl, 1, n))


def _rope_kernel(pos_ref, inv_ref, cos_ref, sin_ref):
    ang = inv_ref[...] * pos_ref[...]
    cos_ref[...] = jnp.cos(ang)
    sin_ref[...] = jnp.sin(ang)


def _rope_tables(positions, tn=4096):
    t = positions.size
    half = QK_ROPE_DIM // 2
    inv = 1.0 / (ROPE_THETA ** (jnp.arange(0, QK_ROPE_DIM, 2, dtype=F32) / QK_ROPE_DIM))
    pos = positions.astype(F32).reshape(1, t)
    cos, sin = pl.pallas_call(
        _rope_kernel,
        grid=(t // tn,),
        in_specs=[pl.BlockSpec((1, tn), lambda i: (0, i)),
                  pl.BlockSpec((half, 1), lambda i: (0, 0))],
        out_specs=[pl.BlockSpec((half, tn), lambda i: (0, i))] * 2,
        out_shape=[jax.ShapeDtypeStruct((half, t), F32)] * 2,
        compiler_params=_params("parallel"),
        name="rope_tables",
    )(pos, inv.reshape(half, 1))
    reps = LANES // half
    return jnp.tile(cos.T, (1, reps)), jnp.tile(sin.T, (1, reps))


def _rope_sign(shape):
    lane = _lane_iota(shape)
    return jnp.where((lane % QK_ROPE_DIM) < (QK_ROPE_DIM // 2), -1.0, 1.0).astype(F32)


def _s5_disc_kernel(lr_ref, li_ref, ldt_ref, br_ref, bi_ref, abr_ref, abi_ref, bbr_ref, bbi_ref):
    lr = lr_ref[0]
    li = li_ref[0]
    dt = jnp.exp(ldt_ref[0])
    mag = jnp.exp(lr * dt)
    ab_re = mag * jnp.cos(li * dt)
    ab_im = mag * jnp.sin(li * dt)
    den = lr * lr + li * li
    nr = ab_re - 1.0
    ni = ab_im
    f_re = (nr * lr + ni * li) / den
    f_im = (ni * lr - nr * li) / den
    br = br_ref[0]
    bi = bi_ref[0]
    abr_ref[0] = ab_re
    abi_ref[0] = ab_im
    bbr_ref[0] = f_re[None] * br - f_im[None] * bi
    bbi_ref[0] = f_re[None] * bi + f_im[None] * br


def _s5_discretise(lam_re, lam_im, log_dt, b_re, b_im):
    na, g, n = lam_re.shape
    p = b_re.shape[-1]
    bt_re = b_re.transpose(0, 3, 1, 2)
    bt_im = b_im.transpose(0, 3, 1, 2)
    m2 = pl.BlockSpec((1, g, n), lambda l: (l, 0, 0))
    m3 = pl.BlockSpec((1, p, g, n), lambda l: (l, 0, 0, 0))
    return pl.pallas_call(
        _s5_disc_kernel,
        grid=(na,),
        in_specs=[m2, m2, pl.BlockSpec((1, g, 1), lambda l: (l, 0, 0)), m3, m3],
        out_specs=[m2, m2, m3, m3],
        out_shape=[jax.ShapeDtypeStruct((na, g, n), F32)] * 2
        + [jax.ShapeDtypeStruct((na, p, g, n), F32)] * 2,
        compiler_params=_params("parallel"),
        name="s5_discretise",
    )(lam_re, lam_im, log_dt.reshape(na, g, 1), bt_re, bt_im)


def _slab_block_diag(w_gab):
    g, a, b = w_gab.shape
    ns = g // SLAB_GROUPS
    w = w_gab.reshape(ns, SLAB_GROUPS, a, b)
    eye = jnp.eye(SLAB_GROUPS, dtype=w.dtype)
    return jnp.einsum("kgab,gh->kgahb", w, eye).reshape(ns, SLAB_GROUPS * a, SLAB_GROUPS * b)


def _s5_mixer_kernel(x_ref, shift_ref, scale_ref, gate_ref, g_ref, bblk_ref, cre_ref, cim_ref,
                     are_ref, aim_ref, d_ref, wglu_ref, bglu_ref, o_ref,
                     bu_re, bu_im, st_re, st_im, y_scr, *, ts, n_slabs, slab_state):
    @pl.when(pl.program_id(0) == 0)
    def _():
        st_re[...] = jnp.zeros_like(st_re)
        st_im[...] = jnp.zeros_like(st_im)

    x = x_ref[...]
    h = _modulate(_rms(x, g_ref[...]), shift_ref[...], scale_ref[...])
    hb = h.astype(BF16)

    for k in range(n_slabs):
        bu = _dot(hb[:, k * LANES:(k + 1) * LANES], bblk_ref[k])
        bu_re[:, k * slab_state:(k + 1) * slab_state] = bu[:, :slab_state]
        bu_im[:, k * slab_state:(k + 1) * slab_state] = bu[:, slab_state:]

    n_state = n_slabs * slab_state
    for cb in range(n_state // SCAN_LANES):
        sl = slice(cb * SCAN_LANES, (cb + 1) * SCAN_LANES)
        ar = jnp.broadcast_to(are_ref[:, sl], (SUBLANES, SCAN_LANES))
        ai = jnp.broadcast_to(aim_ref[:, sl], (SUBLANES, SCAN_LANES))

        def step(i, carry, sl=sl, ar=ar, ai=ai):
            sr, si = carry
            r0 = pl.multiple_of(i * SUBLANES, SUBLANES)
            nr = ar * sr - ai * si + bu_re[pl.ds(r0, SUBLANES), sl]
            ni = ar * si + ai * sr + bu_im[pl.ds(r0, SUBLANES), sl]
            bu_re[pl.ds(r0, SUBLANES), sl] = nr
            bu_im[pl.ds(r0, SUBLANES), sl] = ni
            return nr, ni

        sr, si = lax.fori_loop(0, ts, step, (st_re[:, sl], st_im[:, sl]), unroll=True)
        st_re[:, sl] = sr
        st_im[:, sl] = si

    for k in range(n_slabs):
        ssl = slice(k * slab_state, (k + 1) * slab_state)
        yk = _dot(bu_re[:, ssl].astype(BF16), cre_ref[k]) - _dot(bu_im[:, ssl].astype(BF16), cim_ref[k])
        y_scr[:, k * LANES:(k + 1) * LANES] = yk

    y = y_scr[...] + d_ref[...] * h
    gl = jax.nn.gelu(y)
    z = _dot(gl.astype(BF16), wglu_ref[...]) + bglu_ref[...]
    mix = gl * jax.nn.sigmoid(z)
    o_ref[...] = _gated_add(x, gate_ref[...], mix)


def _s5_mixer(x_tb, shift, scale, gate, norm_g, bblk, cre, cim, a_re, a_im, d_skip, w_glu, b_glu, bsz):
    t, d = x_tb.shape
    ts = S5_TIME_TILE
    rows = ts * bsz
    n_slabs, _, two_state = bblk.shape
    slab_state = two_state // 2
    n_state = n_slabs * slab_state
    full = _resident
    row = lambda a: a.reshape(1, -1)
    args = (x_tb, shift, scale, gate, row(norm_g), bblk, cre, cim, row(a_re), row(a_im),
            row(d_skip), w_glu, row(b_glu))
    return pl.pallas_call(
        functools.partial(_s5_mixer_kernel, ts=ts, n_slabs=n_slabs, slab_state=slab_state),
        grid=(t // rows,),
        in_specs=[pl.BlockSpec((rows, d), lambda i: (i, 0))] + [full(a) for a in args[1:]],
        out_specs=pl.BlockSpec((rows, d), lambda i: (i, 0)),
        out_shape=jax.ShapeDtypeStruct((t, d), F32),
        scratch_shapes=[pltpu.VMEM((rows, n_state), F32), pltpu.VMEM((rows, n_state), F32),
                        pltpu.VMEM((bsz, n_state), F32), pltpu.VMEM((bsz, n_state), F32),
                        pltpu.VMEM((rows, d), F32)],
        compiler_params=_params("arbitrary"),
        name="s5_mixer",
    )(*args)


def _ffn_kernel(*refs, n_chunks, with_attn):
    if with_attn:
        (x_ref, o_in_ref, wo_ref, gate1_ref, shift_ref, scale_ref, gate2_ref, g_ref,
         wg_ref, wu_ref, wd_ref, out_ref, h_scr, acc_scr) = refs
        x = x_ref[...] + gate1_ref[0] * _dot(o_in_ref[...], wo_ref[...])
        shift, scale, gate2 = shift_ref[0], scale_ref[0], gate2_ref[0]
    else:
        (x_ref, shift_ref, scale_ref, gate2_ref, g_ref,
         wg_ref, wu_ref, wd_ref, out_ref, h_scr, acc_scr) = refs
        x = x_ref[...]
        shift, scale, gate2 = shift_ref[...], scale_ref[...], gate2_ref[...]

    h_scr[...] = _modulate(_rms(x, g_ref[...]), shift, scale).astype(BF16)
    acc_scr[...] = jnp.zeros_like(acc_scr)

    def chunk(f, carry):
        hb = h_scr[...]
        gt = _dot(hb, wg_ref[f])
        up = _dot(hb, wu_ref[f])
        act = (gt * jax.nn.sigmoid(gt) * up).astype(BF16)
        acc_scr[...] += _dot(act, wd_ref[f])
        return carry

    lax.fori_loop(0, n_chunks, chunk, 0)
    out_ref[...] = _gated_add(x, gate2, acc_scr[...])


def _ffn(x, shift, scale, gate2, norm_g, wg, wu, wd, attn=None):
    t, d = x.shape
    n_chunks = wg.shape[0]
    tm = ROW_TILE
    full = _resident
    xspec = pl.BlockSpec((tm, d), lambda i: (i, 0))
    g2 = norm_g.reshape(1, d)
    if attn is None:
        args = (x, shift, scale, gate2, g2, wg, wu, wd)
        in_specs = [xspec] + [full(a) for a in args[1:]]
    else:
        o_in, w_o, gate1 = attn
        tiles_per_batch = (t // gate1.shape[0]) // tm
        per_b = pl.BlockSpec((1, 1, d), lambda i: (i // tiles_per_batch, 0, 0))
        args = (x, o_in, w_o, gate1, shift, scale, gate2, g2, wg, wu, wd)
        in_specs = [xspec, xspec, full(w_o), per_b, per_b, per_b, per_b, full(g2), full(wg), full(wu), full(wd)]
    return pl.pallas_call(
        functools.partial(_ffn_kernel, n_chunks=n_chunks, with_attn=attn is not None),
        grid=(t // tm,),
        in_specs=in_specs,
        out_specs=xspec,
        out_shape=jax.ShapeDtypeStruct((t, d), F32),
        scratch_shapes=[pltpu.VMEM((tm, d), BF16), pltpu.VMEM((tm, d), F32)],
        compiler_params=_params("parallel"),
        name="ffn_attn_out" if attn is not None else "ffn",
    )(*args)


def _head_tile(nope, rope_at_64, h):
    blk = nope[:, (h // 2) * LANES:(h // 2 + 1) * LANES]
    if h % 2:
        blk = pltpu.roll(blk, QK_NOPE_DIM, 1)
    return jnp.where(_lane_iota(blk.shape) < QK_NOPE_DIM, blk, rope_at_64)


def _kv_kernel(x_ref, shift_ref, scale_ref, g_ref, wa_ref, ga_ref, wkn_ref, wv_ref, gkn_ref,
               gkr_ref, gkrs_ref, seg_ref, ct_ref, st_ref, k_ref, v_ref):
    x = x_ref[...]
    hk = _modulate(_rms(x, g_ref[...]), shift_ref[0], scale_ref[0])
    kva = _dot(hk.astype(BF16), wa_ref[...])
    rank = ga_ref.shape[1]
    cb = _rms(kva[:, :rank], ga_ref[...]).astype(BF16)

    kn = _dot(cb, wkn_ref[...])
    ss = _dot((kn * kn).astype(BF16), seg_ref[...])
    kn = kn * lax.rsqrt(ss * (1.0 / QK_NOPE_DIM) + EPS) * gkn_ref[...]

    a = kva[:, rank:rank + LANES]
    asw = kva[:, rank + LANES:rank + 2 * LANES]
    rstd = lax.rsqrt(jnp.sum(a * a, axis=-1, keepdims=True) * (1.0 / QK_ROPE_DIM) + EPS)
    sgn = _rope_sign(a.shape)
    kr = rstd * (a * gkr_ref[...] * ct_ref[...] + asw * gkrs_ref[...] * (st_ref[...] * sgn))
    kr64 = pltpu.roll(kr, QK_NOPE_DIM, 1)
    kr64 = kr64 + (_lane_iota((1, LANES)) == SHIFT_LANE).astype(F32)

    v = _dot(cb, wv_ref[...])
    ones_col = (_lane_iota((1, LANES)) == V_HEAD_DIM).astype(F32)
    for h in range(N_HEADS):
        k_ref[0, h] = _head_tile(kn, kr64, h).astype(BF16)
        v_ref[0, h] = (v[:, h * LANES:(h + 1) * LANES] + ones_col).astype(BF16)


def _shared_kv(x, bsz, k_shift, k_scale, kv_norm_g, w_kv_a, kv_a_norm_g, w_kv_b, k_nope_g, k_rope_g,
               seg64, ct, st):
    t, d = x.shape
    s_len = t // bsz
    tm = ROW_TILE
    tiles = s_len // tm
    rank = kv_a_norm_g.shape[0]
    half = QK_ROPE_DIM // 2
    pad = jnp.zeros((d, LANES - QK_ROPE_DIM), F32)
    w_rope = w_kv_a[:, rank:]
    w_rope_sw = jnp.concatenate([w_rope[:, half:], w_rope[:, :half]], axis=1)
    wa = jnp.concatenate([w_kv_a[:, :rank], w_rope, pad, w_rope_sw, pad], axis=1).astype(BF16)
    wkv = w_kv_b.reshape(rank, N_HEADS, QK_NOPE_DIM + V_HEAD_DIM)
    wkn = wkv[:, :, :QK_NOPE_DIM].reshape(rank, N_HEADS * QK_NOPE_DIM).astype(BF16)
    wv = jnp.pad(wkv[:, :, QK_NOPE_DIM:], ((0, 0), (0, 0), (0, HEAD_LANES - V_HEAD_DIM)))
    wv = wv.reshape(rank, N_HEADS * HEAD_LANES).astype(BF16)
    gkn = jnp.tile(k_nope_g, N_HEADS).reshape(1, -1)
    lane_pad = lambda g: jnp.pad(g, (0, LANES - g.shape[0])).reshape(1, LANES)
    gkr = lane_pad(k_rope_g)
    gkrs = lane_pad(jnp.concatenate([k_rope_g[half:], k_rope_g[:half]]))
    full = _resident
    xspec = pl.BlockSpec((tm, d), lambda b, i: (b * tiles + i, 0))
    tab = pl.BlockSpec((tm, LANES), lambda b, i: (b * tiles + i, 0))
    per_b = pl.BlockSpec((1, 1, d), lambda b, i: (b, 0, 0))
    hspec = pl.BlockSpec((1, N_HEADS, tm, HEAD_LANES), lambda b, i: (b, 0, i, 0))
    args = (x, k_shift, k_scale, kv_norm_g.reshape(1, d), wa, kv_a_norm_g.reshape(1, rank), wkn, wv,
            gkn, gkr, gkrs, seg64, ct, st)
    in_specs = [xspec, per_b, per_b] + [full(a) for a in args[3:12]] + [tab, tab]
    return pl.pallas_call(
        _kv_kernel,
        grid=(bsz, tiles),
        in_specs=in_specs,
        out_specs=[hspec, hspec],
        out_shape=[jax.ShapeDtypeStruct((bsz, N_HEADS, s_len, HEAD_LANES), BF16)] * 2,
        compiler_params=_params("parallel", "parallel"),
        name="mla_shared_kv",
    )(*args)


def _q_kernel(x_ref, shift_ref, scale_ref, g_ref, wdq_ref, gq_ref, wn_ref, wr_ref, wrs_ref,
              gn_ref, gr_ref, grs_ref, seg64_ref, seg32_ref, qshift_ref, ct_ref, st_ref, q_ref):
    x = x_ref[...]
    h = _modulate(_rms(x, g_ref[...]), shift_ref[0], scale_ref[0])
    cq = _rms(_dot(h.astype(BF16), wdq_ref[...]), gq_ref[...]).astype(BF16)

    qn = _dot(cq, wn_ref[...])
    ssn = _dot((qn * qn).astype(BF16), seg64_ref[...])
    qn = qn * lax.rsqrt(ssn * (1.0 / QK_NOPE_DIM) + EPS) * (gn_ref[...] * Q_SCALE)

    a = _dot(cq, wr_ref[...])
    asw = _dot(cq, wrs_ref[...])
    ssr = _dot((a * a).astype(BF16), seg32_ref[...])
    rstd = lax.rsqrt(ssr * (1.0 / QK_ROPE_DIM) + EPS) * Q_SCALE
    ct = ct_ref[...]
    st = st_ref[...] * _rope_sign(ct.shape)
    heads_per_blk = LANES // QK_ROPE_DIM
    for blk in range(N_HEADS // heads_per_blk):
        sl = slice(blk * LANES, (blk + 1) * LANES)
        qr = rstd[:, sl] * (a[:, sl] * gr_ref[:, sl] * ct + asw[:, sl] * grs_ref[:, sl] * st)
        lane = _lane_iota(qr.shape)
        for j in range(heads_per_blk):
            h_idx = blk * heads_per_blk + j
            shift = (QK_NOPE_DIM - j * QK_ROPE_DIM) % LANES
            r = pltpu.roll(qr, shift, 1) if shift else qr
            r = jnp.where(lane < QK_NOPE_DIM + QK_ROPE_DIM, r, qshift_ref[...])
            q_ref[0, h_idx] = _head_tile(qn, r, h_idx).astype(BF16)


def _queries(x, bsz, shift, scale, norm_g, w_dq, q_norm_g, w_uq, q_nope_g, q_rope_g, seg64, seg32, bound,
             ct, st):
    t, d = x.shape
    qshift = jnp.where(jnp.arange(LANES) == SHIFT_LANE, -bound, 0.0).astype(F32).reshape(1, LANES)
    s_len = t // bsz
    tm = ROW_TILE
    tiles = s_len // tm
    rank = w_dq.shape[1]
    half = QK_ROPE_DIM // 2
    wq = w_uq.reshape(rank, N_HEADS, QK_NOPE_DIM + QK_ROPE_DIM)
    wn = wq[:, :, :QK_NOPE_DIM].reshape(rank, -1).astype(BF16)
    wr = wq[:, :, QK_NOPE_DIM:].reshape(rank, -1).astype(BF16)
    wrs = jnp.concatenate([wq[:, :, QK_NOPE_DIM + half:], wq[:, :, QK_NOPE_DIM:QK_NOPE_DIM + half]], axis=2)
    wrs = wrs.reshape(rank, -1).astype(BF16)
    gn = jnp.tile(q_nope_g, N_HEADS).reshape(1, -1)
    gr = jnp.tile(q_rope_g, N_HEADS).reshape(1, -1)
    grs = jnp.tile(jnp.concatenate([q_rope_g[half:], q_rope_g[:half]]), N_HEADS).reshape(1, -1)
    full = _resident
    xspec = pl.BlockSpec((tm, d), lambda b, i: (b * tiles + i, 0))
    tab = pl.BlockSpec((tm, LANES), lambda b, i: (b * tiles + i, 0))
    per_b = pl.BlockSpec((1, 1, d), lambda b, i: (b, 0, 0))
    hspec = pl.BlockSpec((1, N_HEADS, tm, HEAD_LANES), lambda b, i: (b, 0, i, 0))
    args = (x, shift, scale, norm_g.reshape(1, d), w_dq.astype(BF16), q_norm_g.reshape(1, rank), wn, wr, wrs,
            gn, gr, grs, seg64, seg32, qshift, ct, st)
    in_specs = [xspec, per_b, per_b] + [full(a) for a in args[3:15]] + [tab, tab]
    return pl.pallas_call(
        _q_kernel,
        grid=(bsz, tiles),
        in_specs=in_specs,
        out_specs=hspec,
        out_shape=jax.ShapeDtypeStruct((bsz, N_HEADS, s_len, HEAD_LANES), BF16),
        compiler_params=_params("parallel", "parallel"),
        name="mla_queries",
    )(*args)


def _attn_kernel(q_ref, k_ref, v_ref, o_ref, m_scr, acc_scr, *, tile, fixed_shift):
    i = pl.program_id(2)
    nt_dims = (((1,), (1,)), ((), ()))
    row_chunk = lax.broadcasted_iota(jnp.int32, (tile, tile), 0) // CHUNK
    col_chunk = lax.broadcasted_iota(jnp.int32, (tile, tile), 1) // CHUNK
    diag_mask = row_chunk >= col_chunk
    qs = [q_ref[0, hh] for hh in range(HEADS_PER_STEP)]

    def span_update(off, width, diagonal):
        for hh in range(HEADS_PER_STEP):
            s = lax.dot_general(qs[hh], k_ref[0, hh, pl.ds(off, width), :], nt_dims,
                                preferred_element_type=F32)
            if diagonal:
                s = jnp.where(diag_mask, s, MASK_VALUE)
            v = v_ref[0, hh, pl.ds(off, width), :]
            if fixed_shift:
                pv = _dot(jnp.exp2(s).astype(BF16), v)
                acc_scr[hh] = pv if diagonal else acc_scr[hh] + pv
            else:
                row_max = jnp.max(s, axis=-1, keepdims=True)
                m = row_max if diagonal else m_scr[hh]
                m_new = row_max if diagonal else jnp.maximum(m, row_max)
                pv = _dot(jnp.exp2(s - m_new).astype(BF16), v)
                acc_scr[hh] = pv if diagonal else acc_scr[hh] * jnp.exp2(m - m_new) + pv
                m_scr[hh] = m_new

    span_update(pl.multiple_of(i * tile, tile), tile, True)

    per_trip = 2 if fixed_shift else 1

    def below_diagonal(j, carry):
        for u in range(per_trip):
            span_update(pl.multiple_of((j * per_trip + u) * tile, tile), tile, False)
        return carry

    lax.fori_loop(0, i // per_trip, below_diagonal, 0)
    if per_trip == 2:
        @pl.when(i % 2 == 1)
        def _():
            span_update(pl.multiple_of((i - 1) * tile, tile), tile, False)

    outs = []
    for hh in range(HEADS_PER_STEP):
        acc = acc_scr[hh]
        outs.append(acc / acc[:, V_HEAD_DIM:V_HEAD_DIM + 1])
    lane = _lane_iota(outs[0].shape)
    o_ref[0] = jnp.where(lane < V_HEAD_DIM, outs[0], pltpu.roll(outs[1], V_HEAD_DIM, 1)).astype(o_ref.dtype)


def _score_bound(q_nope_g, q_rope_g, k_nope_g, k_rope_g):
    def sq_len(g_nope, g_rope):
        return QK_NOPE_DIM * jnp.max(g_nope * g_nope) + QK_ROPE_DIM * jnp.max(g_rope * g_rope)
    return Q_SCALE * jnp.sqrt(sq_len(q_nope_g, q_rope_g) * sq_len(k_nope_g, k_rope_g)) * BOUND_SLACK


def _attention(q, k, v, bound):
    run = lambda fixed: functools.partial(_attention_call, fixed_shift=fixed)
    return lax.cond(2.0 * bound <= MAX_FIXED_SHIFT_RANGE, run(True), run(False), q, k, v)


def _attention_call(q, k, v, *, fixed_shift):
    bsz, nh, s_len, hl = q.shape
    tile = ATTN_TILE
    hp = HEADS_PER_STEP
    return pl.pallas_call(
        functools.partial(_attn_kernel, tile=tile, fixed_shift=fixed_shift),
        grid=(bsz, nh // hp, s_len // tile),
        in_specs=[pl.BlockSpec((1, hp, tile, hl), lambda b, h, i: (b, h, i, 0)),
                  pl.BlockSpec((1, hp, s_len, hl), lambda b, h, i: (b, h, 0, 0)),
                  pl.BlockSpec((1, hp, s_len, hl), lambda b, h, i: (b, h, 0, 0))],
        out_specs=pl.BlockSpec((1, tile, hp * V_HEAD_DIM), lambda b, h, i: (b, i, h)),
        out_shape=jax.ShapeDtypeStruct((bsz, s_len, nh * V_HEAD_DIM), BF16),
        scratch_shapes=[pltpu.VMEM((hp, tile, 1), F32), pltpu.VMEM((hp, tile, hl), F32)],
        compiler_params=_params("parallel", "parallel", "arbitrary"),
        name="mla_attention_fixed_shift" if fixed_shift else "mla_attention_online",
    )(q, k, v)


def _seg_ones(n, seg):
    idx = jnp.arange(n) // seg
    return (idx[:, None] == idx[None, :]).astype(BF16)


def _ffn_weights(w_gate, w_up, w_down):
    d, f = w_gate.shape
    nf = f // FFN_CHUNK
    wg = w_gate.reshape(d, nf, FFN_CHUNK).transpose(1, 0, 2).astype(BF16)
    wu = w_up.reshape(d, nf, FFN_CHUNK).transpose(1, 0, 2).astype(BF16)
    wd = w_down.reshape(nf, FFN_CHUNK, d).astype(BF16)
    return wg, wu, wd


def kernel(x, c, positions, ada_w, ada_b, norm1_g, norm2_g, ffn_w_gate, ffn_w_up, ffn_w_down, s5_lam_re, s5_lam_im, s5_log_dt, s5_b_re, s5_b_im, s5_c_re, s5_c_im, s5_d, s5_w_glu, s5_b_glu, kv_ada_w, kv_ada_b, kv_norm_g, w_kv_a, kv_a_norm_g, w_kv_b, k_nope_norm_g, k_rope_norm_g, mla_w_dq, mla_q_norm_g, mla_w_uq, mla_q_nope_norm_g, mla_q_rope_norm_g, mla_w_o):
    bsz, s_len, d = x.shape
    depth = ada_w.shape[0]
    n_a = s5_lam_re.shape[0]
    t = bsz * s_len

    mods = _mods(c, ada_w, ada_b).reshape(depth, bsz, 6, d)
    kv_mods = _mods(c, kv_ada_w[None], kv_ada_b[None]).reshape(bsz, 2, d)
    cos_t, sin_t = _rope_tables(positions)

    ab_re, ab_im, bb_re, bb_im = _s5_discretise(s5_lam_re, s5_lam_im, s5_log_dt, s5_b_re, s5_b_im)

    xt = x.transpose(1, 0, 2).reshape(t, d)
    for l in range(n_a):
        m = mods[l]
        bre = _slab_block_diag(bb_re[l].transpose(1, 0, 2))
        bim = _slab_block_diag(bb_im[l].transpose(1, 0, 2))
        bblk = jnp.concatenate([bre, bim], axis=2).astype(BF16)
        cre = _slab_block_diag(s5_c_re[l].transpose(0, 2, 1)).astype(BF16)
        cim = _slab_block_diag(s5_c_im[l].transpose(0, 2, 1)).astype(BF16)
        xt = _s5_mixer(xt, m[:, 0], m[:, 1], m[:, 2], norm1_g[l], bblk, cre, cim,
                       ab_re[l].reshape(-1), ab_im[l].reshape(-1), s5_d[l],
                       s5_w_glu[l].astype(BF16), s5_b_glu[l], bsz)
        wg, wu, wd = _ffn_weights(ffn_w_gate[l], ffn_w_up[l], ffn_w_down[l])
        xt = _ffn(xt, m[:, 3], m[:, 4], m[:, 5], norm2_g[l], wg, wu, wd)
    xb = xt.reshape(s_len, bsz, d).transpose(1, 0, 2).reshape(t, d)

    seg64 = _seg_ones(N_HEADS * QK_NOPE_DIM, QK_NOPE_DIM)
    seg32 = _seg_ones(N_HEADS * QK_ROPE_DIM, QK_ROPE_DIM)
    per_b = lambda v: v.reshape(bsz, 1, d)
    k_all, v_all = _shared_kv(xb, bsz, per_b(kv_mods[:, 0]), per_b(kv_mods[:, 1]), kv_norm_g, w_kv_a,
                              kv_a_norm_g, w_kv_b, k_nope_norm_g, k_rope_norm_g, seg64, cos_t, sin_t)
    for l in range(n_a, depth):
        j = l - n_a
        m = mods[l]
        bound = _score_bound(mla_q_nope_norm_g[j], mla_q_rope_norm_g[j], k_nope_norm_g, k_rope_norm_g)
        q = _queries(xb, bsz, per_b(m[:, 0]), per_b(m[:, 1]), norm1_g[l], mla_w_dq[j], mla_q_norm_g[j],
                     mla_w_uq[j], mla_q_nope_norm_g[j], mla_q_rope_norm_g[j], seg64, seg32, bound,
                     cos_t, sin_t)
        o = _attention(q, k_all, v_all, bound).reshape(t, N_HEADS * V_HEAD_DIM)
        wg, wu, wd = _ffn_weights(ffn_w_gate[l], ffn_w_up[l], ffn_w_down[l])
        xb = _ffn(xb, per_b(m[:, 3]), per_b(m[:, 4]), per_b(m[:, 5]), norm2_g[l], wg, wu, wd,
                  attn=(o, mla_w_o[j].astype(BF16), per_b(m[:, 2])))
    return xb.reshape(bsz, s_len, d)
```

```python
import functools
import math

import jax
import jax.numpy as jnp
from jax import lax
from jax.experimental import pallas as pl
from jax.experimental.pallas import tpu as pltpu

F32 = jnp.float32
BF16 = jnp.bfloat16

CHUNK = 64
SSM_GROUP = 16
SSM_STATE = 64
N_HEADS = 16
QK_NOPE_DIM = 64
QK_ROPE_DIM = 32
V_HEAD_DIM = 64
ROPE_THETA = 10000.0
ATTN_SCALE = 1.0 / math.sqrt(QK_NOPE_DIM + QK_ROPE_DIM)
Q_SCALE = ATTN_SCALE * math.log2(math.e)
EPS = 1e-6
MASK_VALUE = -1e30
SHIFT_LANE = QK_NOPE_DIM + QK_ROPE_DIM
BOUND_SLACK = 1.01
MAX_FIXED_SHIFT_RANGE = 60.0

LANES = 128
SUBLANES = 8
HEAD_LANES = 128
SLAB_GROUPS = LANES // SSM_GROUP
VMEM_LIMIT = 56 * 1024 * 1024

ROW_TILE = 512
S5_TIME_TILE = 32
SCAN_LANES = 512
FFN_CHUNK = 256
ATTN_TILE = 512
HEADS_PER_STEP = 2


def _params(*sem):
    return pltpu.CompilerParams(dimension_semantics=sem, vmem_limit_bytes=VMEM_LIMIT)


def _rms(x, g):
    return x * lax.rsqrt(jnp.mean(x * x, axis=-1, keepdims=True) + EPS) * g


def _rowwise(fn, a, *mods):
    r = mods[0].shape[0]
    if r == 1:
        return fn(a, *mods)
    rows, d = a.shape
    out = fn(a.reshape(rows // r, r, d), *[m[None] for m in mods])
    return out.reshape(rows, d)


def _modulate(h, shift, scale):
    return _rowwise(lambda a, sh, sc: a * (1.0 + sc) + sh, h, shift, scale)


def _gated_add(x, gate, upd):
    return x + _rowwise(lambda a, g: a * g, upd, gate)


def _dot(a, b):
    return jnp.dot(a, b, preferred_element_type=F32)


def _resident(a):
    zeros = (0,) * a.ndim
    return pl.BlockSpec(a.shape, lambda *_: zeros, pipeline_mode=pl.Buffered(1))


def _lane_iota(shape):
    return lax.broadcasted_iota(jnp.int32, shape, len(shape) - 1)


def _mods_kernel(c_ref, w_ref, b_ref, o_ref):
    c = c_ref[...]
    ca = c * jax.nn.sigmoid(c)
    o_ref[0] = jnp.dot(ca, w_ref[0], preferred_element_type=F32,
                       precision=lax.Precision.HIGHEST) + b_ref[0]


def _mods(c, w, b, tn=2048):
    nl, d, n = w.shape
    bsz = c.shape[0]
    return pl.pallas_call(
        _mods_kernel,
        grid=(nl, n // tn),
        in_specs=[pl.BlockSpec((bsz, d), lambda l, j: (0, 0)),
                  pl.BlockSpec((1, d, tn), lambda l, j: (l, 0, j)),
                  pl.BlockSpec((1, 1, tn), lambda l, j: (l, 0, j))],
        out_specs=pl.BlockSpec((1, bsz, tn), lambda l, j: (l, 0, j)),
        out_shape=jax.ShapeDtypeStruct((nl, bsz, n), F32),
        compiler_params=_params("parallel", "parallel"),
        name="adaln_mods",
    )(c, w, b.reshape(nl, 1, n))


def _rope_kernel(pos_ref, inv_ref, cos_ref, sin_ref):
    ang = inv_ref[...] * pos_ref[...]
    cos_ref[...] = jnp.cos(ang)
    sin_ref[...] = jnp.sin(ang)


def _rope_tables(positions, tn=4096):
    t = positions.size
    half = QK_ROPE_DIM // 2
    inv = 1.0 / (ROPE_THETA ** (jnp.arange(0, QK_ROPE_DIM, 2, dtype=F32) / QK_ROPE_DIM))
    pos = positions.astype(F32).reshape(1, t)
    cos, sin = pl.pallas_call(
        _rope_kernel,
        grid=(t // tn,),
        in_specs=[pl.BlockSpec((1, tn), lambda i: (0, i)),
                  pl.BlockSpec((half, 1), lambda i: (0, 0))],
        out_specs=[pl.BlockSpec((half, tn), lambda i: (0, i))] * 2,
        out_shape=[jax.ShapeDtypeStruct((half, t), F32)] * 2,
        compiler_params=_params("parallel"),
        name="rope_tables",
    )(pos, inv.reshape(half, 1))
    reps = LANES // half
    return jnp.tile(cos.T, (1, reps)), jnp.tile(sin.T, (1, reps))


def _rope_sign(shape):
    lane = _lane_iota(shape)
    return jnp.where((lane % QK_ROPE_DIM) < (QK_ROPE_DIM // 2), -1.0, 1.0).astype(F32)


def _s5_disc_kernel(lr_ref, li_ref, ldt_ref, br_ref, bi_ref, abr_ref, abi_ref, bbr_ref, bbi_ref):
    lr = lr_ref[0]
    li = li_ref[0]
    dt = jnp.exp(ldt_ref[0])
    mag = jnp.exp(lr * dt)
    ab_re = mag * jnp.cos(li * dt)
    ab_im = mag * jnp.sin(li * dt)
    den = lr * lr + li * li
    nr = ab_re - 1.0
    ni = ab_im
    f_re = (nr * lr + ni * li) / den
    f_im = (ni * lr - nr * li) / den
    br = br_ref[0]
    bi = bi_ref[0]
    abr_ref[0] = ab_re
    abi_ref[0] = ab_im
    bbr_ref[0] = f_re[None] * br - f_im[None] * bi
    bbi_ref[0] = f_re[None] * bi + f_im[None] * br


def _s5_discretise(lam_re, lam_im, log_dt, b_re, b_im):
    na, g, n = lam_re.shape
    p = b_re.shape[-1]
    bt_re = b_re.transpose(0, 3, 1, 2)
    bt_im = b_im.transpose(0, 3, 1, 2)
    m2 = pl.BlockSpec((1, g, n), lambda l: (l, 0, 0))
    m3 = pl.BlockSpec((1, p, g, n), lambda l: (l, 0, 0, 0))
    return pl.pallas_call(
        _s5_disc_kernel,
        grid=(na,),
        in_specs=[m2, m2, pl.BlockSpec((1, g, 1), lambda l: (l, 0, 0)), m3, m3],
        out_specs=[m2, m2, m3, m3],
        out_shape=[jax.ShapeDtypeStruct((na, g, n), F32)] * 2
        + [jax.ShapeDtypeStruct((na, p, g, n), F32)] * 2,
        compiler_params=_params("parallel"),
        name="s5_discretise",
    )(lam_re, lam_im, log_dt.reshape(na, g, 1), bt_re, bt_im)


def _slab_block_diag(w_gab):
    g, a, b = w_gab.shape
    ns = g // SLAB_GROUPS
    w = w_gab.reshape(ns, SLAB_GROUPS, a, b)
    eye = jnp.eye(SLAB_GROUPS, dtype=w.dtype)
    return jnp.einsum("kgab,gh->kgahb", w, eye).reshape(ns, SLAB_GROUPS * a, SLAB_GROUPS * b)


def _s5_mixer_kernel(x_ref, shift_ref, scale_ref, gate_ref, g_ref, bblk_ref, cre_ref, cim_ref,
                     are_ref, aim_ref, d_ref, wglu_ref, bglu_ref, o_ref,
                     bu_re, bu_im, st_re, st_im, y_scr, *, ts, n_slabs, slab_state):
    @pl.when(pl.program_id(0) == 0)
    def _():
        st_re[...] = jnp.zeros_like(st_re)
        st_im[...] = jnp.zeros_like(st_im)

    x = x_ref[...]
    h = _modulate(_rms(x, g_ref[...]), shift_ref[...], scale_ref[...])
    hb = h.astype(BF16)

    for k in range(n_slabs):
        bu = _dot(hb[:, k * LANES:(k + 1) * LANES], bblk_ref[k])
        bu_re[:, k * slab_state:(k + 1) * slab_state] = bu[:, :slab_state]
        bu_im[:, k * slab_state:(k + 1) * slab_state] = bu[:, slab_state:]

    n_state = n_slabs * slab_state
    for cb in range(n_state // SCAN_LANES):
        sl = slice(cb * SCAN_LANES, (cb + 1) * SCAN_LANES)
        ar = jnp.broadcast_to(are_ref[:, sl], (SUBLANES, SCAN_LANES))
        ai = jnp.broadcast_to(aim_ref[:, sl], (SUBLANES, SCAN_LANES))

        def step(i, carry, sl=sl, ar=ar, ai=ai):
            sr, si = carry
            r0 = pl.multiple_of(i * SUBLANES, SUBLANES)
            nr = ar * sr - ai * si + bu_re[pl.ds(r0, SUBLANES), sl]
            ni = ar * si + ai * sr + bu_im[pl.ds(r0, SUBLANES), sl]
            bu_re[pl.ds(r0, SUBLANES), sl] = nr
            bu_im[pl.ds(r0, SUBLANES), sl] = ni
            return nr, ni

        sr, si = lax.fori_loop(0, ts, step, (st_re[:, sl], st_im[:, sl]), unroll=True)
        st_re[:, sl] = sr
        st_im[:, sl] = si

    for k in range(n_slabs):
        ssl = slice(k * slab_state, (k + 1) * slab_state)
        yk = _dot(bu_re[:, ssl].astype(BF16), cre_ref[k]) - _dot(bu_im[:, ssl].astype(BF16), cim_ref[k])
        y_scr[:, k * LANES:(k + 1) * LANES] = yk

    y = y_scr[...] + d_ref[...] * h
    gl = jax.nn.gelu(y)
    z = _dot(gl.astype(BF16), wglu_ref[...]) + bglu_ref[...]
    mix = gl * jax.nn.sigmoid(z)
    o_ref[...] = _gated_add(x, gate_ref[...], mix)


def _s5_mixer(x_tb, shift, scale, gate, norm_g, bblk, cre, cim, a_re, a_im, d_skip, w_glu, b_glu, bsz):
    t, d = x_tb.shape
    ts = S5_TIME_TILE
    rows = ts * bsz
    n_slabs, _, two_state = bblk.shape
    slab_state = two_state // 2
    n_state = n_slabs * slab_state
    full = _resident
    row = lambda a: a.reshape(1, -1)
    args = (x_tb, shift, scale, gate, row(norm_g), bblk, cre, cim, row(a_re), row(a_im),
            row(d_skip), w_glu, row(b_glu))
    return pl.pallas_call(
        functools.partial(_s5_mixer_kernel, ts=ts, n_slabs=n_slabs, slab_state=slab_state),
        grid=(t // rows,),
        in_specs=[pl.BlockSpec((rows, d), lambda i: (i, 0))] + [full(a) for a in args[1:]],
        out_specs=pl.BlockSpec((rows, d), lambda i: (i, 0)),
        out_shape=jax.ShapeDtypeStruct((t, d), F32),
        scratch_shapes=[pltpu.VMEM((rows, n_state), F32), pltpu.VMEM((rows, n_state), F32),
                        pltpu.VMEM((bsz, n_state), F32), pltpu.VMEM((bsz, n_state), F32),
                        pltpu.VMEM((rows, d), F32)],
        compiler_params=_params("arbitrary"),
        name="s5_mixer",
    )(*args)


def _ffn_kernel(*refs, n_chunks, with_attn):
    if with_attn:
        (x_ref, o_in_ref, wo_ref, gate1_ref, shift_ref, scale_ref, gate2_ref, g_ref,
         wg_ref, wu_ref, wd_ref, out_ref, h_scr, acc_scr) = refs
        x = x_ref[...] + gate1_ref[0] * _dot(o_in_ref[...], wo_ref[...])
        shift, scale, gate2 = shift_ref[0], scale_ref[0], gate2_ref[0]
    else:
        (x_ref, shift_ref, scale_ref, gate2_ref, g_ref,
         wg_ref, wu_ref, wd_ref, out_ref, h_scr, acc_scr) = refs
        x = x_ref[...]
        shift, scale, gate2 = shift_ref[...], scale_ref[...], gate2_ref[...]

    h_scr[...] = _modulate(_rms(x, g_ref[...]), shift, scale).astype(BF16)
    acc_scr[...] = jnp.zeros_like(acc_scr)

    hb = h_scr[...]
    for f in range(n_chunks):
        cols = slice(f * FFN_CHUNK, (f + 1) * FFN_CHUNK)
        gt = _dot(hb, wg_ref[:, cols])
        up = _dot(hb, wu_ref[:, cols])
        act = (gt * jax.nn.sigmoid(gt) * up).astype(BF16)
        acc_scr[...] += _dot(act, wd_ref[cols, :])
    out_ref[...] = _gated_add(x, gate2, acc_scr[...])


def _ffn(x, shift, scale, gate2, norm_g, wg, wu, wd, attn=None):
    t, d = x.shape
    n_chunks = wg.shape[1] // FFN_CHUNK
    tm = ROW_TILE
    full = _resident
    xspec = pl.BlockSpec((tm, d), lambda i: (i, 0))
    g2 = norm_g.reshape(1, d)
    if attn is None:
        args = (x, shift, scale, gate2, g2, wg, wu, wd)
        in_specs = [xspec] + [full(a) for a in args[1:]]
    else:
        o_in, w_o, gate1 = attn
        tiles_per_batch = (t // gate1.shape[0]) // tm
        per_b = pl.BlockSpec((1, 1, d), lambda i: (i // tiles_per_batch, 0, 0))
        args = (x, o_in, w_o, gate1, shift, scale, gate2, g2, wg, wu, wd)
        in_specs = [xspec, xspec, full(w_o), per_b, per_b, per_b, per_b, full(g2), full(wg), full(wu), full(wd)]
    return pl.pallas_call(
        functools.partial(_ffn_kernel, n_chunks=n_chunks, with_attn=attn is not None),
        grid=(t // tm,),
        in_specs=in_specs,
        out_specs=xspec,
        out_shape=jax.ShapeDtypeStruct((t, d), F32),
        scratch_shapes=[pltpu.VMEM((tm, d), BF16), pltpu.VMEM((tm, d), F32)],
        compiler_params=_params("parallel"),
        name="ffn_attn_out" if attn is not None else "ffn",
    )(*args)


def _head_tile(nope, rope_at_64, h):
    blk = nope[:, (h // 2) * LANES:(h // 2 + 1) * LANES]
    if h % 2:
        blk = pltpu.roll(blk, QK_NOPE_DIM, 1)
    return jnp.where(_lane_iota(blk.shape) < QK_NOPE_DIM, blk, rope_at_64)


def _kv_kernel(x_ref, shift_ref, scale_ref, g_ref, wa_ref, ga_ref, wkn_ref, wv_ref, gkn_ref,
               gkr_ref, gkrs_ref, seg_ref, ct_ref, st_ref, k_ref, v_ref):
    x = x_ref[...]
    hk = _modulate(_rms(x, g_ref[...]), shift_ref[0], scale_ref[0])
    kva = _dot(hk.astype(BF16), wa_ref[...])
    rank = ga_ref.shape[1]
    cb = _rms(kva[:, :rank], ga_ref[...]).astype(BF16)

    kn = _dot(cb, wkn_ref[...])
    ss = _dot((kn * kn).astype(BF16), seg_ref[...])
    kn = kn * lax.rsqrt(ss * (1.0 / QK_NOPE_DIM) + EPS) * gkn_ref[...]

    a = kva[:, rank:rank + LANES]
    asw = kva[:, rank + LANES:rank + 2 * LANES]
    rstd = lax.rsqrt(jnp.sum(a * a, axis=-1, keepdims=True) * (1.0 / QK_ROPE_DIM) + EPS)
    sgn = _rope_sign(a.shape)
    kr = rstd * (a * gkr_ref[...] * ct_ref[...] + asw * gkrs_ref[...] * (st_ref[...] * sgn))
    kr64 = pltpu.roll(kr, QK_NOPE_DIM, 1)
    kr64 = kr64 + (_lane_iota((1, LANES)) == SHIFT_LANE).astype(F32)

    v = _dot(cb, wv_ref[...])
    ones_col = (_lane_iota((1, LANES)) == V_HEAD_DIM).astype(F32)
    for h in range(N_HEADS):
        k_ref[0, h] = _head_tile(kn, kr64, h).astype(BF16)
        v_ref[0, h] = (v[:, h * LANES:(h + 1) * LANES] + ones_col).astype(BF16)


def _shared_kv(x, bsz, k_shift, k_scale, kv_norm_g, w_kv_a, kv_a_norm_g, w_kv_b, k_nope_g, k_rope_g,
               seg64, ct, st):
    t, d = x.shape
    s_len = t // bsz
    tm = ROW_TILE
    tiles = s_len // tm
    rank = kv_a_norm_g.shape[0]
    half = QK_ROPE_DIM // 2
    pad = jnp.zeros((d, LANES - QK_ROPE_DIM), F32)
    w_rope = w_kv_a[:, rank:]
    w_rope_sw = jnp.concatenate([w_rope[:, half:], w_rope[:, :half]], axis=1)
    wa = jnp.concatenate([w_kv_a[:, :rank], w_rope, pad, w_rope_sw, pad], axis=1).astype(BF16)
    wkv = w_kv_b.reshape(rank, N_HEADS, QK_NOPE_DIM + V_HEAD_DIM)
    wkn = wkv[:, :, :QK_NOPE_DIM].reshape(rank, N_HEADS * QK_NOPE_DIM).astype(BF16)
    wv = jnp.pad(wkv[:, :, QK_NOPE_DIM:], ((0, 0), (0, 0), (0, HEAD_LANES - V_HEAD_DIM)))
    wv = wv.reshape(rank, N_HEADS * HEAD_LANES).astype(BF16)
    gkn = jnp.tile(k_nope_g, N_HEADS).reshape(1, -1)
    lane_pad = lambda g: jnp.pad(g, (0, LANES - g.shape[0])).reshape(1, LANES)
    gkr = lane_pad(k_rope_g)
    gkrs = lane_pad(jnp.concatenate([k_rope_g[half:], k_rope_g[:half]]))
    full = _resident
    xspec = pl.BlockSpec((tm, d), lambda b, i: (b * tiles + i, 0))
    tab = pl.BlockSpec((tm, LANES), lambda b, i: (b * tiles + i, 0))
    per_b = pl.BlockSpec((1, 1, d), lambda b, i: (b, 0, 0))
    hspec = pl.BlockSpec((1, N_HEADS, tm, HEAD_LANES), lambda b, i: (b, 0, i, 0))
    args = (x, k_shift, k_scale, kv_norm_g.reshape(1, d), wa, kv_a_norm_g.reshape(1, rank), wkn, wv,
            gkn, gkr, gkrs, seg64, ct, st)
    in_specs = [xspec, per_b, per_b] + [full(a) for a in args[3:12]] + [tab, tab]
    return pl.pallas_call(
        _kv_kernel,
        grid=(bsz, tiles),
        in_specs=in_specs,
        out_specs=[hspec, hspec],
        out_shape=[jax.ShapeDtypeStruct((bsz, N_HEADS, s_len, HEAD_LANES), BF16)] * 2,
        compiler_params=_params("parallel", "parallel"),
        name="mla_shared_kv",
    )(*args)


def _q_kernel(x_ref, shift_ref, scale_ref, g_ref, wdq_ref, gq_ref, wn_ref, wr_ref, wrs_ref,
              gn_ref, gr_ref, grs_ref, seg64_ref, seg32_ref, qshift_ref, ct_ref, st_ref, q_ref):
    x = x_ref[...]
    h = _modulate(_rms(x, g_ref[...]), shift_ref[0], scale_ref[0])
    cq = _rms(_dot(h.astype(BF16), wdq_ref[...]), gq_ref[...]).astype(BF16)

    qn = _dot(cq, wn_ref[...])
    ssn = _dot((qn * qn).astype(BF16), seg64_ref[...])
    qn = qn * lax.rsqrt(ssn * (1.0 / QK_NOPE_DIM) + EPS) * (gn_ref[...] * Q_SCALE)

    a = _dot(cq, wr_ref[...])
    asw = _dot(cq, wrs_ref[...])
    ssr = _dot((a * a).astype(BF16), seg32_ref[...])
    rstd = lax.rsqrt(ssr * (1.0 / QK_ROPE_DIM) + EPS) * Q_SCALE
    ct = ct_ref[...]
    st = st_ref[...] * _rope_sign(ct.shape)
    heads_per_blk = LANES // QK_ROPE_DIM
    for blk in range(N_HEADS // heads_per_blk):
        sl = slice(blk * LANES, (blk + 1) * LANES)
        qr = rstd[:, sl] * (a[:, sl] * gr_ref[:, sl] * ct + asw[:, sl] * grs_ref[:, sl] * st)
        lane = _lane_iota(qr.shape)
        for j in range(heads_per_blk):
            h_idx = blk * heads_per_blk + j
            shift = (QK_NOPE_DIM - j * QK_ROPE_DIM) % LANES
            r = pltpu.roll(qr, shift, 1) if shift else qr
            r = jnp.where(lane < QK_NOPE_DIM + QK_ROPE_DIM, r, qshift_ref[...])
            q_ref[0, h_idx] = _head_tile(qn, r, h_idx).astype(BF16)


def _queries(x, bsz, shift, scale, norm_g, w_dq, q_norm_g, w_uq, q_nope_g, q_rope_g, seg64, seg32, bound,
             ct, st):
    t, d = x.shape
    qshift = jnp.where(jnp.arange(LANES) == SHIFT_LANE, -bound, 0.0).astype(F32).reshape(1, LANES)
    s_len = t // bsz
    tm = ROW_TILE
    tiles = s_len // tm
    rank = w_dq.shape[1]
    half = QK_ROPE_DIM // 2
    wq = w_uq.reshape(rank, N_HEADS, QK_NOPE_DIM + QK_ROPE_DIM)
    wn = wq[:, :, :QK_NOPE_DIM].reshape(rank, -1).astype(BF16)
    wr = wq[:, :, QK_NOPE_DIM:].reshape(rank, -1).astype(BF16)
    wrs = jnp.concatenate([wq[:, :, QK_NOPE_DIM + half:], wq[:, :, QK_NOPE_DIM:QK_NOPE_DIM + half]], axis=2)
    wrs = wrs.reshape(rank, -1).astype(BF16)
    gn = jnp.tile(q_nope_g, N_HEADS).reshape(1, -1)
    gr = jnp.tile(q_rope_g, N_HEADS).reshape(1, -1)
    grs = jnp.tile(jnp.concatenate([q_rope_g[half:], q_rope_g[:half]]), N_HEADS).reshape(1, -1)
    full = _resident
    xspec = pl.BlockSpec((tm, d), lambda b, i: (b * tiles + i, 0))
    tab = pl.BlockSpec((tm, LANES), lambda b, i: (b * tiles + i, 0))
    per_b = pl.BlockSpec((1, 1, d), lambda b, i: (b, 0, 0))
    hspec = pl.BlockSpec((1, N_HEADS, tm, HEAD_LANES), lambda b, i: (b, 0, i, 0))
    args = (x, shift, scale, norm_g.reshape(1, d), w_dq.astype(BF16), q_norm_g.reshape(1, rank), wn, wr, wrs,
            gn, gr, grs, seg64, seg32, qshift, ct, st)
    in_specs = [xspec, per_b, per_b] + [full(a) for a in args[3:15]] + [tab, tab]
    return pl.pallas_call(
        _q_kernel,
        grid=(bsz, tiles),
        in_specs=in_specs,
        out_specs=hspec,
        out_shape=jax.ShapeDtypeStruct((bsz, N_HEADS, s_len, HEAD_LANES), BF16),
        compiler_params=_params("parallel", "parallel"),
        name="mla_queries",
    )(*args)


def _chunk_offsets(tile):
    row_chunk = lax.broadcasted_iota(jnp.int32, (tile, tile), 0) // CHUNK
    col_chunk = lax.broadcasted_iota(jnp.int32, (tile, tile), 1) // CHUNK
    return row_chunk - col_chunk


def _merge_head_pair(accs):
    outs = [acc / acc[:, V_HEAD_DIM:V_HEAD_DIM + 1] for acc in accs]
    lane = _lane_iota(outs[0].shape)
    return jnp.where(lane < V_HEAD_DIM, outs[0], pltpu.roll(outs[1], V_HEAD_DIM, 1))


_NT_DIMS = (((1,), (1,)), ((), ()))


def _attn_online_kernel(q_ref, k_ref, v_ref, o_ref, m_scr, acc_scr, *, tile):
    i = pl.program_id(2)
    diag_mask = _chunk_offsets(tile) >= 0
    qs = [q_ref[0, hh] for hh in range(HEADS_PER_STEP)]

    def tile_update(off, diagonal):
        for hh in range(HEADS_PER_STEP):
            s = lax.dot_general(qs[hh], k_ref[0, hh, pl.ds(off, tile), :], _NT_DIMS,
                                preferred_element_type=F32)
            if diagonal:
                s = jnp.where(diag_mask, s, MASK_VALUE)
            row_max = jnp.max(s, axis=-1, keepdims=True)
            m = row_max if diagonal else m_scr[hh]
            m_new = row_max if diagonal else jnp.maximum(m, row_max)
            pv = _dot(jnp.exp2(s - m_new).astype(BF16), v_ref[0, hh, pl.ds(off, tile), :])
            acc_scr[hh] = pv if diagonal else acc_scr[hh] * jnp.exp2(m - m_new) + pv
            m_scr[hh] = m_new

    tile_update(pl.multiple_of(i * tile, tile), True)

    def below_diagonal(j, carry):
        tile_update(pl.multiple_of(j * tile, tile), False)
        return carry

    lax.fori_loop(0, i, below_diagonal, 0)
    o_ref[0] = _merge_head_pair([acc_scr[hh] for hh in range(HEADS_PER_STEP)]).astype(o_ref.dtype)


def _attn_fixed_shift_kernel(q_ref, k_ref, v_ref, o_ref, *, tile, n_tiles):
    g = pl.program_id(2)
    lo, hi = g, n_tiles - 1 - g
    rel = _chunk_offsets(tile)
    chunks_per_tile = tile // CHUNK
    zero = jnp.zeros((tile, HEAD_LANES), F32)
    acc_lo = [zero] * HEADS_PER_STEP
    acc_hi = [zero] * HEADS_PER_STEP
    for slot in range(n_tiles + 1):
        always_lo = slot == 0
        always_hi = slot > n_tiles // 2 - 1
        is_lo = slot <= lo
        if always_lo:
            q_idx, k_idx = lo, slot
        elif always_hi:
            q_idx, k_idx = hi, slot - lo - 1
        else:
            q_idx = jnp.where(is_lo, lo, hi)
            k_idx = jnp.where(is_lo, slot, slot - lo - 1)
        q_off = pl.multiple_of(q_idx * tile, tile)
        k_off = pl.multiple_of(k_idx * tile, tile)
        if slot == n_tiles:
            mask = rel >= 0
        elif always_hi:
            mask = None
        else:
            mask = rel >= (k_idx - q_idx) * chunks_per_tile
        for hh in range(HEADS_PER_STEP):
            s = lax.dot_general(q_ref[0, hh, pl.ds(q_off, tile), :], k_ref[0, hh, pl.ds(k_off, tile), :],
                                _NT_DIMS, preferred_element_type=F32)
            if mask is not None:
                s = jnp.where(mask, s, MASK_VALUE)
            pv = _dot(jnp.exp2(s).astype(BF16), v_ref[0, hh, pl.ds(k_off, tile), :])
            if always_lo:
                acc_lo[hh] = acc_lo[hh] + pv
            elif always_hi:
                acc_hi[hh] = acc_hi[hh] + pv
            else:
                acc_lo[hh] = acc_lo[hh] + jnp.where(is_lo, pv, 0.0)
                acc_hi[hh] = acc_hi[hh] + jnp.where(is_lo, 0.0, pv)
    o_ref[0, pl.ds(pl.multiple_of(lo * tile, tile), tile), :] = _merge_head_pair(acc_lo).astype(o_ref.dtype)
    o_ref[0, pl.ds(pl.multiple_of(hi * tile, tile), tile), :] = _merge_head_pair(acc_hi).astype(o_ref.dtype)


def _score_bound(q_nope_g, q_rope_g, k_nope_g, k_rope_g):
    def sq_len(g_nope, g_rope):
        return QK_NOPE_DIM * jnp.max(g_nope * g_nope) + QK_ROPE_DIM * jnp.max(g_rope * g_rope)
    return Q_SCALE * jnp.sqrt(sq_len(q_nope_g, q_rope_g) * sq_len(k_nope_g, k_rope_g)) * BOUND_SLACK


def _attention(q, k, v, bound):
    return lax.cond(2.0 * bound <= MAX_FIXED_SHIFT_RANGE, _attention_fixed_shift, _attention_online, q, k, v)


def _attention_online(q, k, v):
    bsz, nh, s_len, hl = q.shape
    tile = ATTN_TILE
    hp = HEADS_PER_STEP
    whole = pl.BlockSpec((1, hp, s_len, hl), lambda b, h, i: (b, h, 0, 0))
    return pl.pallas_call(
        functools.partial(_attn_online_kernel, tile=tile),
        grid=(bsz, nh // hp, s_len // tile),
        in_specs=[pl.BlockSpec((1, hp, tile, hl), lambda b, h, i: (b, h, i, 0)), whole, whole],
        out_specs=pl.BlockSpec((1, tile, hp * V_HEAD_DIM), lambda b, h, i: (b, i, h)),
        out_shape=jax.ShapeDtypeStruct((bsz, s_len, nh * V_HEAD_DIM), BF16),
        scratch_shapes=[pltpu.VMEM((hp, tile, 1), F32), pltpu.VMEM((hp, tile, hl), F32)],
        compiler_params=_params("parallel", "parallel", "arbitrary"),
        name="mla_attention_online",
    )(q, k, v)


def _attention_fixed_shift(q, k, v):
    bsz, nh, s_len, hl = q.shape
    tile = ATTN_TILE
    hp = HEADS_PER_STEP
    n_tiles = s_len // tile
    whole = pl.BlockSpec((1, hp, s_len, hl), lambda b, h, g: (b, h, 0, 0))
    return pl.pallas_call(
        functools.partial(_attn_fixed_shift_kernel, tile=tile, n_tiles=n_tiles),
        grid=(bsz, nh // hp, n_tiles // 2),
        in_specs=[whole, whole, whole],
        out_specs=pl.BlockSpec((1, s_len, hp * V_HEAD_DIM), lambda b, h, g: (b, 0, h)),
        out_shape=jax.ShapeDtypeStruct((bsz, s_len, nh * V_HEAD_DIM), BF16),
        compiler_params=_params("parallel", "parallel", "arbitrary"),
        name="mla_attention_fixed_shift",
    )(q, k, v)


def _seg_ones(n, seg):
    idx = jnp.arange(n) // seg
    return (idx[:, None] == idx[None, :]).astype(BF16)


def _ffn_weights(w_gate, w_up, w_down):
    return w_gate.astype(BF16), w_up.astype(BF16), w_down.astype(BF16)


def kernel(x, c, positions, ada_w, ada_b, norm1_g, norm2_g, ffn_w_gate, ffn_w_up, ffn_w_down, s5_lam_re, s5_lam_im, s5_log_dt, s5_b_re, s5_b_im, s5_c_re, s5_c_im, s5_d, s5_w_glu, s5_b_glu, kv_ada_w, kv_ada_b, kv_norm_g, w_kv_a, kv_a_norm_g, w_kv_b, k_nope_norm_g, k_rope_norm_g, mla_w_dq, mla_q_norm_g, mla_w_uq, mla_q_nope_norm_g, mla_q_rope_norm_g, mla_w_o):
    bsz, s_len, d = x.shape
    depth = ada_w.shape[0]
    n_a = s5_lam_re.shape[0]
    t = bsz * s_len

    mods = _mods(c, ada_w, ada_b).reshape(depth, bsz, 6, d)
    kv_mods = _mods(c, kv_ada_w[None], kv_ada_b[None]).reshape(bsz, 2, d)
    cos_t, sin_t = _rope_tables(positions)

    ab_re, ab_im, bb_re, bb_im = _s5_discretise(s5_lam_re, s5_lam_im, s5_log_dt, s5_b_re, s5_b_im)

    xt = x.transpose(1, 0, 2).reshape(t, d)
    for l in range(n_a):
        m = mods[l]
        bre = _slab_block_diag(bb_re[l].transpose(1, 0, 2))
        bim = _slab_block_diag(bb_im[l].transpose(1, 0, 2))
        bblk = jnp.concatenate([bre, bim], axis=2).astype(BF16)
        cre = _slab_block_diag(s5_c_re[l].transpose(0, 2, 1)).astype(BF16)
        cim = _slab_block_diag(s5_c_im[l].transpose(0, 2, 1)).astype(BF16)
        xt = _s5_mixer(xt, m[:, 0], m[:, 1], m[:, 2], norm1_g[l], bblk, cre, cim,
                       ab_re[l].reshape(-1), ab_im[l].reshape(-1), s5_d[l],
                       s5_w_glu[l].astype(BF16), s5_b_glu[l], bsz)
        wg, wu, wd = _ffn_weights(ffn_w_gate[l], ffn_w_up[l], ffn_w_down[l])
        xt = _ffn(xt, m[:, 3], m[:, 4], m[:, 5], norm2_g[l], wg, wu, wd)
    xb = xt.reshape(s_len, bsz, d).transpose(1, 0, 2).reshape(t, d)

    seg64 = _seg_ones(N_HEADS * QK_NOPE_DIM, QK_NOPE_DIM)
    seg32 = _seg_ones(N_HEADS * QK_ROPE_DIM, QK_ROPE_DIM)
    per_b = lambda v: v.reshape(bsz, 1, d)
    k_all, v_all = _shared_kv(xb, bsz, per_b(kv_mods[:, 0]), per_b(kv_mods[:, 1]), kv_norm_g, w_kv_a,
                              kv_a_norm_g, w_kv_b, k_nope_norm_g, k_rope_norm_g, seg64, cos_t, sin_t)
    for l in range(n_a, depth):
        j = l - n_a
        m = mods[l]
        bound = _score_bound(mla_q_nope_norm_g[j], mla_q_rope_norm_g[j], k_nope_norm_g, k_rope_norm_g)
        q = _queries(xb, bsz, per_b(m[:, 0]), per_b(m[:, 1]), norm1_g[l], mla_w_dq[j], mla_q_norm_g[j],
                     mla_w_uq[j], mla_q_nope_norm_g[j], mla_q_rope_norm_g[j], seg64, seg32, bound,
                     cos_t, sin_t)
        o = _attention(q, k_all, v_all, bound).reshape(t, N_HEADS * V_HEAD_DIM)
        wg, wu, wd = _ffn_weights(ffn_w_gate[l], ffn_w_up[l], ffn_w_down[l])
        xb = _ffn(xb, per_b(m[:, 3]), per_b(m[:, 4]), per_b(m[:, 5]), norm2_g[l], wg, wu, wd,
                  attn=(o, mla_w_o[j].astype(BF16), per_b(m[:, 2])))
    return xb.reshape(bsz, s_len, d)
```

```python
import functools
import math

import jax
import jax.numpy as jnp
from jax import lax
from jax.experimental import pallas as pl
from jax.experimental.pallas import tpu as pltpu

F32 = jnp.float32
BF16 = jnp.bfloat16

CHUNK = 64
SSM_GROUP = 16
SSM_STATE = 64
N_HEADS = 16
QK_NOPE_DIM = 64
QK_ROPE_DIM = 32
V_HEAD_DIM = 64
ROPE_THETA = 10000.0
ATTN_SCALE = 1.0 / math.sqrt(QK_NOPE_DIM + QK_ROPE_DIM)
Q_SCALE = ATTN_SCALE * math.log2(math.e)
EPS = 1e-6
MASK_VALUE = -1e30
BOUND_SLACK = 1.01
MAX_UNSHIFTED_BOUND = 40.0

LANES = 128
SUBLANES = 8
HEAD_LANES = 128
SLAB_GROUPS = LANES // SSM_GROUP
VMEM_LIMIT = 56 * 1024 * 1024

ROW_TILE = 512
S5_TIME_TILE = 32
SCAN_LANES = 512
FFN_CHUNK = 256
ATTN_TILE = 512
HEADS_PER_STEP = 2


def _params(*sem):
    return pltpu.CompilerParams(dimension_semantics=sem, vmem_limit_bytes=VMEM_LIMIT)


def _rms(x, g):
    return x * lax.rsqrt(jnp.mean(x * x, axis=-1, keepdims=True) + EPS) * g


def _rowwise(fn, a, *mods):
    r = mods[0].shape[0]
    if r == 1:
        return fn(a, *mods)
    rows, d = a.shape
    out = fn(a.reshape(rows // r, r, d), *[m[None] for m in mods])
    return out.reshape(rows, d)


def _modulate(h, shift, scale):
    return _rowwise(lambda a, sh, sc: a * (1.0 + sc) + sh, h, shift, scale)


def _gated_add(x, gate, upd):
    return x + _rowwise(lambda a, g: a * g, upd, gate)


def _dot(a, b):
    return jnp.dot(a, b, preferred_element_type=F32)


_NT_DIMS = (((1,), (1,)), ((), ()))


def _resident(a):
    zeros = (0,) * a.ndim
    return pl.BlockSpec(a.shape, lambda *_: zeros, pipeline_mode=pl.Buffered(1))


def _lane_iota(shape):
    return lax.broadcasted_iota(jnp.int32, shape, len(shape) - 1)


def _head_rstd(x, sum_ref, spread_ref, dim):
    ss = _dot((x * x).astype(BF16), sum_ref[...])
    rstd = lax.rsqrt(ss * (1.0 / dim) + EPS)
    hi = rstd.astype(BF16)
    lo = (rstd - hi.astype(F32)).astype(BF16)
    return _dot(jnp.concatenate([hi, lo], axis=1), spread_ref[...])


def _mods_kernel(c_ref, w_ref, b_ref, o_ref):
    c = c_ref[...]
    ca = c * jax.nn.sigmoid(c)
    o_ref[0] = jnp.dot(ca, w_ref[0], preferred_element_type=F32,
                       precision=lax.Precision.HIGHEST) + b_ref[0]


def _mods(c, w, b, tn=2048):
    nl, d, n = w.shape
    bsz = c.shape[0]
    return pl.pallas_call(
        _mods_kernel,
        grid=(nl, n // tn),
        in_specs=[pl.BlockSpec((bsz, d), lambda l, j: (0, 0)),
                  pl.BlockSpec((1, d, tn), lambda l, j: (l, 0, j)),
                  pl.BlockSpec((1, 1, tn), lambda l, j: (l, 0, j))],
        out_specs=pl.BlockSpec((1, bsz, tn), lambda l, j: (l, 0, j)),
        out_shape=jax.ShapeDtypeStruct((nl, bsz, n), F32),
        compiler_params=_params("parallel", "parallel"),
        name="adaln_mods",
    )(c, w, b.reshape(nl, 1, n))


def _rope_kernel(pos_ref, inv_ref, cos_ref, sin_ref):
    ang = inv_ref[...] * pos_ref[...]
    cos_ref[...] = jnp.cos(ang)
    sin_ref[...] = jnp.sin(ang)


def _rope_tables(positions, tn=4096):
    t = positions.size
    half = QK_ROPE_DIM // 2
    inv = 1.0 / (ROPE_THETA ** (jnp.arange(0, QK_ROPE_DIM, 2, dtype=F32) / QK_ROPE_DIM))
    pos = positions.astype(F32).reshape(1, t)
    cos, sin = pl.pallas_call(
        _rope_kernel,
        grid=(t // tn,),
        in_specs=[pl.BlockSpec((1, tn), lambda i: (0, i)),
                  pl.BlockSpec((half, 1), lambda i: (0, 0))],
        out_specs=[pl.BlockSpec((half, tn), lambda i: (0, i))] * 2,
        out_shape=[jax.ShapeDtypeStruct((half, t), F32)] * 2,
        compiler_params=_params("parallel"),
        name="rope_tables",
    )(pos, inv.reshape(half, 1))
    reps = LANES // half
    return jnp.tile(cos.T, (1, reps)), jnp.tile(sin.T, (1, reps))


def _rope_sign(shape):
    lane = _lane_iota(shape)
    return jnp.where((lane % QK_ROPE_DIM) < (QK_ROPE_DIM // 2), -1.0, 1.0).astype(F32)


def _cmul(ar, ai, br, bi):
    return ar * br - ai * bi, ar * bi + ai * br


def _s5_disc_kernel(lr_ref, li_ref, ldt_ref, br_ref, bi_ref, cr_ref, ci_ref,
                    a2r_ref, a2i_ref, bbr_ref, bbi_ref, abbr_ref, abbi_ref, car_ref, cai_ref, cb_ref):
    lr = lr_ref[0]
    li = li_ref[0]
    dt = jnp.exp(ldt_ref[0])
    mag = jnp.exp(lr * dt)
    a_re = mag * jnp.cos(li * dt)
    a_im = mag * jnp.sin(li * dt)
    den = lr * lr + li * li
    nr = a_re - 1.0
    ni = a_im
    f_re = (nr * lr + ni * li) / den
    f_im = (ni * lr - nr * li) / den
    bb_re, bb_im = _cmul(f_re, f_im, br_ref[0], bi_ref[0])
    cr = cr_ref[0]
    ci = ci_ref[0]
    a2r_ref[0], a2i_ref[0] = _cmul(a_re, a_im, a_re, a_im)
    bbr_ref[0] = bb_re
    bbi_ref[0] = bb_im
    abbr_ref[0], abbi_ref[0] = _cmul(a_re, a_im, bb_re, bb_im)
    car_ref[0], cai_ref[0] = _cmul(cr, ci, a_re, a_im)
    group_dot = functools.partial(jnp.einsum, "gpn,gqn->gpq", preferred_element_type=F32,
                                  precision=lax.Precision.HIGHEST)
    cb_ref[0] = group_dot(cr, bb_re) - group_dot(ci, bb_im)


def _s5_discretise(lam_re, lam_im, log_dt, b_re, b_im, c_re, c_im):
    na, g, n = lam_re.shape
    p = b_re.shape[-1]
    vec = pl.BlockSpec((1, g, 1, n), lambda l: (l, 0, 0, 0))
    mat = pl.BlockSpec((1, g, p, n), lambda l: (l, 0, 0, 0))
    vec_shape = jax.ShapeDtypeStruct((na, g, 1, n), F32)
    mat_shape = jax.ShapeDtypeStruct((na, g, p, n), F32)
    return pl.pallas_call(
        _s5_disc_kernel,
        grid=(na,),
        in_specs=[vec, vec, pl.BlockSpec((1, g, 1, 1), lambda l: (l, 0, 0, 0)), mat, mat, mat, mat],
        out_specs=[vec, vec] + [mat] * 6 + [pl.BlockSpec((1, g, p, p), lambda l: (l, 0, 0, 0))],
        out_shape=[vec_shape] * 2 + [mat_shape] * 6 + [jax.ShapeDtypeStruct((na, g, p, p), F32)],
        compiler_params=_params("parallel"),
        name="s5_discretise",
    )(lam_re.reshape(na, g, 1, n), lam_im.reshape(na, g, 1, n), log_dt.reshape(na, g, 1, 1),
      b_re.transpose(0, 1, 3, 2), b_im.transpose(0, 1, 3, 2), c_re, c_im)


def _slab_block_diag(w_gab):
    g, a, b = w_gab.shape
    ns = g // SLAB_GROUPS
    w = w_gab.reshape(ns, SLAB_GROUPS, a, b)
    eye = jnp.eye(SLAB_GROUPS, dtype=w.dtype)
    return jnp.einsum("kgab,gh->kgahb", w, eye).reshape(ns, SLAB_GROUPS * a, SLAB_GROUPS * b)


def _s5_slab_weights(bb_re, bb_im, abb_re, abb_im, c_re, c_im, ca_re, ca_im, cb):
    col = lambda re, im: jnp.concatenate([_slab_block_diag(re), _slab_block_diag(im)], axis=2)
    b_pair = jnp.concatenate([col(bb_re, bb_im), col(abb_re, abb_im)], axis=1)
    out = lambda w: _slab_block_diag(w.transpose(0, 2, 1))
    c_pair_re = jnp.concatenate([out(c_re), out(ca_re)], axis=2)
    c_pair_im = jnp.concatenate([out(c_im), out(ca_im)], axis=2)
    cb_blk = _slab_block_diag(cb.transpose(0, 2, 1))
    return tuple(w.astype(BF16) for w in (b_pair, c_pair_re, c_pair_im, cb_blk))


def _s5_mixer_kernel(x_ref, shift_ref, scale_ref, gate_ref, g_ref, bpair_ref, cre_ref, cim_ref, cb_ref,
                     a2r_ref, a2i_ref, d_ref, wglu_ref, bglu_ref, o_ref,
                     bu_re, bu_im, st_re, st_im, tail_scr, yf_scr, ys_scr, *, ts, n_slabs, slab_state):
    @pl.when(pl.program_id(0) == 0)
    def _():
        st_re[...] = jnp.zeros_like(st_re)
        st_im[...] = jnp.zeros_like(st_im)
        tail_scr[...] = jnp.zeros_like(tail_scr)

    pairs = ts // 2
    pair_rows = pairs * SUBLANES
    x = x_ref[...]
    d = x.shape[1]
    h = _modulate(_rms(x, g_ref[...]), shift_ref[...], scale_ref[...])
    h3 = h.reshape(pairs, 2 * SUBLANES, d)
    h_first = h3[:, :SUBLANES].reshape(pair_rows, d).astype(BF16)
    h_second = h3[:, SUBLANES:].reshape(pair_rows, d).astype(BF16)

    for k in range(n_slabs):
        lanes = slice(k * LANES, (k + 1) * LANES)
        bu = _dot(jnp.concatenate([h_second[:, lanes], h_first[:, lanes]], axis=1), bpair_ref[k])
        bu_re[:, k * slab_state:(k + 1) * slab_state] = bu[:, :slab_state]
        bu_im[:, k * slab_state:(k + 1) * slab_state] = bu[:, slab_state:]

    n_state = n_slabs * slab_state
    for cb in range(n_state // SCAN_LANES):
        sl = slice(cb * SCAN_LANES, (cb + 1) * SCAN_LANES)
        ar = jnp.broadcast_to(a2r_ref[:, sl], (SUBLANES, SCAN_LANES))
        ai = jnp.broadcast_to(a2i_ref[:, sl], (SUBLANES, SCAN_LANES))

        def step(i, carry, sl=sl, ar=ar, ai=ai):
            sr, si = carry
            r0 = pl.multiple_of(i * SUBLANES, SUBLANES)
            nr = ar * sr - ai * si + bu_re[pl.ds(r0, SUBLANES), sl]
            ni = ar * si + ai * sr + bu_im[pl.ds(r0, SUBLANES), sl]
            bu_re[pl.ds(r0, SUBLANES), sl] = nr
            bu_im[pl.ds(r0, SUBLANES), sl] = ni
            return nr, ni

        sr, si = lax.fori_loop(0, pairs, step, (st_re[:, sl], st_im[:, sl]), unroll=True)
        st_re[:, sl] = sr
        st_im[:, sl] = si

    for k in range(n_slabs):
        ssl = slice(k * slab_state, (k + 1) * slab_state)
        lanes = slice(k * LANES, (k + 1) * LANES)
        z = _dot(bu_re[:, ssl].astype(BF16), cre_ref[k]) - _dot(bu_im[:, ssl].astype(BF16), cim_ref[k])
        ys_scr[:, lanes] = z[:, :LANES]
        carried = jnp.concatenate([tail_scr[:, lanes], z[:pair_rows - SUBLANES, LANES:]], axis=0)
        tail_scr[:, lanes] = z[pair_rows - SUBLANES:, LANES:]
        yf_scr[:, lanes] = carried + _dot(h_first[:, lanes], cb_ref[k])

    y = jnp.concatenate([yf_scr[...].reshape(pairs, SUBLANES, d), ys_scr[...].reshape(pairs, SUBLANES, d)],
                        axis=1).reshape(2 * pair_rows, d)
    y = y + d_ref[...] * h
    gl = jax.nn.gelu(y)
    z = _dot(gl.astype(BF16), wglu_ref[...]) + bglu_ref[...]
    mix = gl * jax.nn.sigmoid(z)
    o_ref[...] = _gated_add(x, gate_ref[...], mix)


def _s5_mixer(x_tb, shift, scale, gate, norm_g, slab_weights, a2_re, a2_im, d_skip, w_glu, b_glu, bsz):
    t, d = x_tb.shape
    ts = S5_TIME_TILE
    rows = ts * bsz
    pair_rows = rows // 2
    b_pair, c_pair_re, c_pair_im, cb_blk = slab_weights
    n_slabs, _, two_state = b_pair.shape
    slab_state = two_state // 2
    n_state = n_slabs * slab_state
    full = _resident
    row = lambda a: a.reshape(1, -1)
    args = (x_tb, shift, scale, gate, row(norm_g), b_pair, c_pair_re, c_pair_im, cb_blk, row(a2_re), row(a2_im),
            row(d_skip), w_glu, row(b_glu))
    return pl.pallas_call(
        functools.partial(_s5_mixer_kernel, ts=ts, n_slabs=n_slabs, slab_state=slab_state),
        grid=(t // rows,),
        in_specs=[pl.BlockSpec((rows, d), lambda i: (i, 0))] + [full(a) for a in args[1:]],
        out_specs=pl.BlockSpec((rows, d), lambda i: (i, 0)),
        out_shape=jax.ShapeDtypeStruct((t, d), F32),
        scratch_shapes=[pltpu.VMEM((pair_rows, n_state), F32), pltpu.VMEM((pair_rows, n_state), F32),
                        pltpu.VMEM((bsz, n_state), F32), pltpu.VMEM((bsz, n_state), F32),
                        pltpu.VMEM((bsz, d), F32), pltpu.VMEM((pair_rows, d), F32),
                        pltpu.VMEM((pair_rows, d), F32)],
        compiler_params=_params("arbitrary"),
        name="s5_mixer",
    )(*args)


def _ffn_kernel(*refs, n_chunks, with_attn):
    if with_attn:
        (x_ref, o_in_ref, wo_ref, gate1_ref, shift_ref, scale_ref, gate2_ref, g_ref,
         wg_ref, wu_ref, wd_ref, out_ref, h_scr, acc_scr) = refs
        x = x_ref[...] + gate1_ref[0] * _dot(o_in_ref[...], wo_ref[...])
        shift, scale, gate2 = shift_ref[0], scale_ref[0], gate2_ref[0]
    else:
        (x_ref, shift_ref, scale_ref, gate2_ref, g_ref,
         wg_ref, wu_ref, wd_ref, out_ref, h_scr, acc_scr) = refs
        x = x_ref[...]
        shift, scale, gate2 = shift_ref[...], scale_ref[...], gate2_ref[...]

    h_scr[...] = _modulate(_rms(x, g_ref[...]), shift, scale).astype(BF16)
    acc_scr[...] = jnp.zeros_like(acc_scr)

    hb = h_scr[...]
    for f in range(n_chunks):
        cols = slice(f * FFN_CHUNK, (f + 1) * FFN_CHUNK)
        gt = _dot(hb, wg_ref[:, cols])
        up = _dot(hb, wu_ref[:, cols])
        act = (gt * jax.nn.sigmoid(gt) * up).astype(BF16)
        acc_scr[...] += _dot(act, wd_ref[cols, :])
    out_ref[...] = _gated_add(x, gate2, acc_scr[...])


def _ffn(x, shift, scale, gate2, norm_g, wg, wu, wd, attn=None):
    t, d = x.shape
    n_chunks = wg.shape[1] // FFN_CHUNK
    tm = ROW_TILE
    full = _resident
    xspec = pl.BlockSpec((tm, d), lambda i: (i, 0))
    g2 = norm_g.reshape(1, d)
    if attn is None:
        args = (x, shift, scale, gate2, g2, wg, wu, wd)
        in_specs = [xspec] + [full(a) for a in args[1:]]
    else:
        o_in, w_o, gate1 = attn
        tiles_per_batch = (t // gate1.shape[0]) // tm
        per_b = pl.BlockSpec((1, 1, d), lambda i: (i // tiles_per_batch, 0, 0))
        args = (x, o_in, w_o, gate1, shift, scale, gate2, g2, wg, wu, wd)
        in_specs = [xspec, xspec, full(w_o), per_b, per_b, per_b, per_b, full(g2), full(wg), full(wu), full(wd)]
    return pl.pallas_call(
        functools.partial(_ffn_kernel, n_chunks=n_chunks, with_attn=attn is not None),
        grid=(t // tm,),
        in_specs=in_specs,
        out_specs=xspec,
        out_shape=jax.ShapeDtypeStruct((t, d), F32),
        scratch_shapes=[pltpu.VMEM((tm, d), BF16), pltpu.VMEM((tm, d), F32)],
        compiler_params=_params("parallel"),
        name="ffn_attn_out" if attn is not None else "ffn",
    )(*args)


def _head_tile(nope, rope_at_64, h):
    blk = nope[:, (h // 2) * LANES:(h // 2 + 1) * LANES]
    if h % 2:
        blk = pltpu.roll(blk, QK_NOPE_DIM, 1)
    return jnp.where(_lane_iota(blk.shape) < QK_NOPE_DIM, blk, rope_at_64)


def _kv_kernel(x_ref, shift_ref, scale_ref, g_ref, wa_ref, ga_ref, wkn_ref, wv_ref, gkn_ref,
               gkr_ref, gkrs_ref, sum_ref, spread_ref, ct_ref, st_ref, k_ref, v_ref):
    x = x_ref[...]
    hk = _modulate(_rms(x, g_ref[...]), shift_ref[0], scale_ref[0])
    kva = _dot(hk.astype(BF16), wa_ref[...])
    rank = ga_ref.shape[1]
    cb = _rms(kva[:, :rank], ga_ref[...]).astype(BF16)

    kn = _dot(cb, wkn_ref[...])
    kn = kn * _head_rstd(kn, sum_ref, spread_ref, QK_NOPE_DIM) * gkn_ref[...]

    a = kva[:, rank:rank + LANES]
    asw = kva[:, rank + LANES:rank + 2 * LANES]
    rstd = lax.rsqrt(jnp.sum(a * a, axis=-1, keepdims=True) * (1.0 / QK_ROPE_DIM) + EPS)
    sgn = _rope_sign(a.shape)
    kr = rstd * (a * gkr_ref[...] * ct_ref[...] + asw * gkrs_ref[...] * (st_ref[...] * sgn))
    kr64 = pltpu.roll(kr, QK_NOPE_DIM, 1)

    v = _dot(cb, wv_ref[...])
    ones_col = (_lane_iota((1, LANES)) == V_HEAD_DIM).astype(F32)
    for h in range(N_HEADS):
        k_ref[0, h] = _head_tile(kn, kr64, h).astype(BF16)
        v_ref[0, h] = (v[:, h * LANES:(h + 1) * LANES] + ones_col).astype(BF16)


def _shared_kv(x, bsz, k_shift, k_scale, kv_norm_g, w_kv_a, kv_a_norm_g, w_kv_b, k_nope_g, k_rope_g,
               head_sum, head_spread, ct, st):
    t, d = x.shape
    s_len = t // bsz
    tm = ROW_TILE
    tiles = s_len // tm
    rank = kv_a_norm_g.shape[0]
    half = QK_ROPE_DIM // 2
    pad = jnp.zeros((d, LANES - QK_ROPE_DIM), F32)
    w_rope = w_kv_a[:, rank:]
    w_rope_sw = jnp.concatenate([w_rope[:, half:], w_rope[:, :half]], axis=1)
    wa = jnp.concatenate([w_kv_a[:, :rank], w_rope, pad, w_rope_sw, pad], axis=1).astype(BF16)
    wkv = w_kv_b.reshape(rank, N_HEADS, QK_NOPE_DIM + V_HEAD_DIM)
    wkn = wkv[:, :, :QK_NOPE_DIM].reshape(rank, N_HEADS * QK_NOPE_DIM).astype(BF16)
    wv = jnp.pad(wkv[:, :, QK_NOPE_DIM:], ((0, 0), (0, 0), (0, HEAD_LANES - V_HEAD_DIM)))
    wv = wv.reshape(rank, N_HEADS * HEAD_LANES).astype(BF16)
    gkn = jnp.tile(k_nope_g, N_HEADS).reshape(1, -1)
    lane_pad = lambda g: jnp.pad(g, (0, LANES - g.shape[0])).reshape(1, LANES)
    gkr = lane_pad(k_rope_g)
    gkrs = lane_pad(jnp.concatenate([k_rope_g[half:], k_rope_g[:half]]))
    full = _resident
    xspec = pl.BlockSpec((tm, d), lambda b, i: (b * tiles + i, 0))
    tab = pl.BlockSpec((tm, LANES), lambda b, i: (b * tiles + i, 0))
    per_b = pl.BlockSpec((1, 1, d), lambda b, i: (b, 0, 0))
    hspec = pl.BlockSpec((1, N_HEADS, tm, HEAD_LANES), lambda b, i: (b, 0, i, 0))
    args = (x, k_shift, k_scale, kv_norm_g.reshape(1, d), wa, kv_a_norm_g.reshape(1, rank), wkn, wv,
            gkn, gkr, gkrs, head_sum, head_spread, ct, st)
    in_specs = [xspec, per_b, per_b] + [full(a) for a in args[3:13]] + [tab, tab]
    return pl.pallas_call(
        _kv_kernel,
        grid=(bsz, tiles),
        in_specs=in_specs,
        out_specs=[hspec, hspec],
        out_shape=[jax.ShapeDtypeStruct((bsz, N_HEADS, s_len, HEAD_LANES), BF16)] * 2,
        compiler_params=_params("parallel", "parallel"),
        name="mla_shared_kv",
    )(*args)


def _q_kernel(x_ref, shift_ref, scale_ref, g_ref, wdq_ref, gq_ref, wn_ref, wr_ref, wrs_ref,
              gn_ref, gr_ref, grs_ref, sum_ref, spread_ref, seg32_ref, ct_ref, st_ref, q_ref):
    x = x_ref[...]
    h = _modulate(_rms(x, g_ref[...]), shift_ref[0], scale_ref[0])
    cq = _rms(_dot(h.astype(BF16), wdq_ref[...]), gq_ref[...]).astype(BF16)

    qn = _dot(cq, wn_ref[...])
    qn = qn * _head_rstd(qn, sum_ref, spread_ref, QK_NOPE_DIM) * (gn_ref[...] * Q_SCALE)

    a = _dot(cq, wr_ref[...])
    asw = _dot(cq, wrs_ref[...])
    ssr = _dot((a * a).astype(BF16), seg32_ref[...])
    rstd = lax.rsqrt(ssr * (1.0 / QK_ROPE_DIM) + EPS) * Q_SCALE
    ct = ct_ref[...]
    st = st_ref[...] * _rope_sign(ct.shape)
    heads_per_blk = LANES // QK_ROPE_DIM
    for blk in range(N_HEADS // heads_per_blk):
        sl = slice(blk * LANES, (blk + 1) * LANES)
        qr = rstd[:, sl] * (a[:, sl] * gr_ref[:, sl] * ct + asw[:, sl] * grs_ref[:, sl] * st)
        lane = _lane_iota(qr.shape)
        for j in range(heads_per_blk):
            h_idx = blk * heads_per_blk + j
            shift = (QK_NOPE_DIM - j * QK_ROPE_DIM) % LANES
            r = pltpu.roll(qr, shift, 1) if shift else qr
            r = jnp.where(lane < QK_NOPE_DIM + QK_ROPE_DIM, r, 0.0)
            q_ref[0, h_idx] = _head_tile(qn, r, h_idx).astype(BF16)


def _queries(x, bsz, shift, scale, norm_g, w_dq, q_norm_g, w_uq, q_nope_g, q_rope_g, head_sum, head_spread,
             seg32, ct, st):
    t, d = x.shape
    s_len = t // bsz
    tm = ROW_TILE
    tiles = s_len // tm
    rank = w_dq.shape[1]
    half = QK_ROPE_DIM // 2
    wq = w_uq.reshape(rank, N_HEADS, QK_NOPE_DIM + QK_ROPE_DIM)
    wn = wq[:, :, :QK_NOPE_DIM].reshape(rank, -1).astype(BF16)
    wr = wq[:, :, QK_NOPE_DIM:].reshape(rank, -1).astype(BF16)
    wrs = jnp.concatenate([wq[:, :, QK_NOPE_DIM + half:], wq[:, :, QK_NOPE_DIM:QK_NOPE_DIM + half]], axis=2)
    wrs = wrs.reshape(rank, -1).astype(BF16)
    gn = jnp.tile(q_nope_g, N_HEADS).reshape(1, -1)
    gr = jnp.tile(q_rope_g, N_HEADS).reshape(1, -1)
    grs = jnp.tile(jnp.concatenate([q_rope_g[half:], q_rope_g[:half]]), N_HEADS).reshape(1, -1)
    full = _resident
    xspec = pl.BlockSpec((tm, d), lambda b, i: (b * tiles + i, 0))
    tab = pl.BlockSpec((tm, LANES), lambda b, i: (b * tiles + i, 0))
    per_b = pl.BlockSpec((1, 1, d), lambda b, i: (b, 0, 0))
    hspec = pl.BlockSpec((1, N_HEADS, tm, HEAD_LANES), lambda b, i: (b, 0, i, 0))
    args = (x, shift, scale, norm_g.reshape(1, d), w_dq.astype(BF16), q_norm_g.reshape(1, rank), wn, wr, wrs,
            gn, gr, grs, head_sum, head_spread, seg32, ct, st)
    in_specs = [xspec, per_b, per_b] + [full(a) for a in args[3:15]] + [tab, tab]
    return pl.pallas_call(
        _q_kernel,
        grid=(bsz, tiles),
        in_specs=in_specs,
        out_specs=hspec,
        out_shape=jax.ShapeDtypeStruct((bsz, N_HEADS, s_len, HEAD_LANES), BF16),
        compiler_params=_params("parallel", "parallel"),
        name="mla_queries",
    )(*args)


def _chunk_offsets(tile):
    row_chunk = lax.broadcasted_iota(jnp.int32, (tile, tile), 0) // CHUNK
    col_chunk = lax.broadcasted_iota(jnp.int32, (tile, tile), 1) // CHUNK
    return row_chunk - col_chunk


def _merge_head_pair(accs):
    outs = [acc / acc[:, V_HEAD_DIM:V_HEAD_DIM + 1] for acc in accs]
    lane = _lane_iota(outs[0].shape)
    return jnp.where(lane < V_HEAD_DIM, outs[0], pltpu.roll(outs[1], V_HEAD_DIM, 1))


def _attn_online_kernel(q_ref, k_ref, v_ref, o_ref, m_scr, acc_scr, *, tile):
    i = pl.program_id(2)
    diag_mask = _chunk_offsets(tile) >= 0
    qs = [q_ref[0, hh] for hh in range(HEADS_PER_STEP)]

    def tile_update(off, diagonal):
        for hh in range(HEADS_PER_STEP):
            s = lax.dot_general(qs[hh], k_ref[0, hh, pl.ds(off, tile), :], _NT_DIMS,
                                preferred_element_type=F32)
            if diagonal:
                s = jnp.where(diag_mask, s, MASK_VALUE)
            row_max = jnp.max(s, axis=-1, keepdims=True)
            m = row_max if diagonal else m_scr[hh]
            m_new = row_max if diagonal else jnp.maximum(m, row_max)
            pv = _dot(jnp.exp2(s - m_new).astype(BF16), v_ref[0, hh, pl.ds(off, tile), :])
            acc_scr[hh] = pv if diagonal else acc_scr[hh] * jnp.exp2(m - m_new) + pv
            m_scr[hh] = m_new

    tile_update(pl.multiple_of(i * tile, tile), True)

    def below_diagonal(j, carry):
        tile_update(pl.multiple_of(j * tile, tile), False)
        return carry

    lax.fori_loop(0, i, below_diagonal, 0)
    o_ref[0] = _merge_head_pair([acc_scr[hh] for hh in range(HEADS_PER_STEP)]).astype(o_ref.dtype)


def _attn_bounded_kernel(q_ref, k_ref, v_ref, o_ref, *, tile, n_tiles):
    g = pl.program_id(2)
    lo, hi = g, n_tiles - 1 - g
    diag_mask = _chunk_offsets(tile) >= 0
    zero = jnp.zeros((tile, HEAD_LANES), F32)
    acc_lo = [zero] * HEADS_PER_STEP
    acc_hi = [zero] * HEADS_PER_STEP

    def rows(ref, hh, idx):
        return ref[0, hh, pl.ds(pl.multiple_of(idx * tile, tile), tile), :]

    def weighted_values(q, k, v, mask):
        s = lax.dot_general(q, k, _NT_DIMS, preferred_element_type=F32)
        if mask is not None:
            s = jnp.where(mask, s, MASK_VALUE)
        return _dot(jnp.exp2(s).astype(BF16), v)

    def diagonal(idx, accs):
        for hh in range(HEADS_PER_STEP):
            accs[hh] = accs[hh] + weighted_values(rows(q_ref, hh, idx), rows(k_ref, hh, idx),
                                                  rows(v_ref, hh, idx), diag_mask)

    diagonal(lo, acc_lo)
    diagonal(hi, acc_hi)
    max_lo = n_tiles // 2 - 1
    for slot in range(1, n_tiles):
        owner_known = slot > max_lo
        if owner_known:
            q_idx, k_idx = hi, slot - 1 - lo
        else:
            is_lo = slot <= lo
            q_idx = jnp.where(is_lo, lo, hi)
            k_idx = jnp.where(is_lo, slot - 1, slot - 1 - lo)
        for hh in range(HEADS_PER_STEP):
            pv = weighted_values(rows(q_ref, hh, q_idx), rows(k_ref, hh, k_idx), rows(v_ref, hh, k_idx), None)
            if owner_known:
                acc_hi[hh] = acc_hi[hh] + pv
            else:
                acc_lo[hh] = acc_lo[hh] + jnp.where(is_lo, pv, 0.0)
                acc_hi[hh] = acc_hi[hh] + jnp.where(is_lo, 0.0, pv)
    o_ref[0, pl.ds(pl.multiple_of(lo * tile, tile), tile), :] = _merge_head_pair(acc_lo).astype(o_ref.dtype)
    o_ref[0, pl.ds(pl.multiple_of(hi * tile, tile), tile), :] = _merge_head_pair(acc_hi).astype(o_ref.dtype)


def _score_bound(q_nope_g, q_rope_g, k_nope_g, k_rope_g):
    def sq_len(g_nope, g_rope):
        return QK_NOPE_DIM * jnp.max(g_nope * g_nope) + QK_ROPE_DIM * jnp.max(g_rope * g_rope)
    return Q_SCALE * jnp.sqrt(sq_len(q_nope_g, q_rope_g) * sq_len(k_nope_g, k_rope_g)) * BOUND_SLACK


def _attention(q, k, v, bound):
    return lax.cond(bound <= MAX_UNSHIFTED_BOUND, _attention_bounded, _attention_online, q, k, v)


def _attention_online(q, k, v):
    bsz, nh, s_len, hl = q.shape
    tile = ATTN_TILE
    hp = HEADS_PER_STEP
    whole = pl.BlockSpec((1, hp, s_len, hl), lambda b, h, i: (b, h, 0, 0))
    return pl.pallas_call(
        functools.partial(_attn_online_kernel, tile=tile),
        grid=(bsz, nh // hp, s_len // tile),
        in_specs=[pl.BlockSpec((1, hp, tile, hl), lambda b, h, i: (b, h, i, 0)), whole, whole],
        out_specs=pl.BlockSpec((1, tile, hp * V_HEAD_DIM), lambda b, h, i: (b, i, h)),
        out_shape=jax.ShapeDtypeStruct((bsz, s_len, nh * V_HEAD_DIM), BF16),
        scratch_shapes=[pltpu.VMEM((hp, tile, 1), F32), pltpu.VMEM((hp, tile, hl), F32)],
        compiler_params=_params("parallel", "parallel", "arbitrary"),
        name="mla_attention_online",
    )(q, k, v)


def _attention_bounded(q, k, v):
    bsz, nh, s_len, hl = q.shape
    tile = ATTN_TILE
    hp = HEADS_PER_STEP
    n_tiles = s_len // tile
    whole = pl.BlockSpec((1, hp, s_len, hl), lambda b, h, g: (b, h, 0, 0))
    return pl.pallas_call(
        functools.partial(_attn_bounded_kernel, tile=tile, n_tiles=n_tiles),
        grid=(bsz, nh // hp, n_tiles // 2),
        in_specs=[whole, whole, whole],
        out_specs=pl.BlockSpec((1, s_len, hp * V_HEAD_DIM), lambda b, h, g: (b, 0, h)),
        out_shape=jax.ShapeDtypeStruct((bsz, s_len, nh * V_HEAD_DIM), BF16),
        compiler_params=_params("parallel", "parallel", "arbitrary"),
        name="mla_attention_bounded",
    )(q, k, v)


def _seg_ones(n, seg):
    idx = jnp.arange(n) // seg
    return (idx[:, None] == idx[None, :]).astype(BF16)


def _ffn_weights(w_gate, w_up, w_down):
    return w_gate.astype(BF16), w_up.astype(BF16), w_down.astype(BF16)


def kernel(x, c, positions, ada_w, ada_b, norm1_g, norm2_g, ffn_w_gate, ffn_w_up, ffn_w_down, s5_lam_re, s5_lam_im, s5_log_dt, s5_b_re, s5_b_im, s5_c_re, s5_c_im, s5_d, s5_w_glu, s5_b_glu, kv_ada_w, kv_ada_b, kv_norm_g, w_kv_a, kv_a_norm_g, w_kv_b, k_nope_norm_g, k_rope_norm_g, mla_w_dq, mla_q_norm_g, mla_w_uq, mla_q_nope_norm_g, mla_q_rope_norm_g, mla_w_o):
    bsz, s_len, d = x.shape
    depth = ada_w.shape[0]
    n_a = s5_lam_re.shape[0]
    t = bsz * s_len

    mods = _mods(c, ada_w, ada_b).reshape(depth, bsz, 6, d)
    kv_mods = _mods(c, kv_ada_w[None], kv_ada_b[None]).reshape(bsz, 2, d)
    cos_t, sin_t = _rope_tables(positions)

    a2_re, a2_im, *s5_mats = _s5_discretise(s5_lam_re, s5_lam_im, s5_log_dt, s5_b_re, s5_b_im, s5_c_re, s5_c_im)

    xt = x.transpose(1, 0, 2).reshape(t, d)
    for l in range(n_a):
        m = mods[l]
        bb_re, bb_im, abb_re, abb_im, ca_re, ca_im, cb = (w[l] for w in s5_mats)
        slab_weights = _s5_slab_weights(bb_re, bb_im, abb_re, abb_im, s5_c_re[l], s5_c_im[l], ca_re, ca_im, cb)
        xt = _s5_mixer(xt, m[:, 0], m[:, 1], m[:, 2], norm1_g[l], slab_weights, a2_re[l], a2_im[l], s5_d[l],
                       s5_w_glu[l].astype(BF16), s5_b_glu[l], bsz)
        wg, wu, wd = _ffn_weights(ffn_w_gate[l], ffn_w_up[l], ffn_w_down[l])
        xt = _ffn(xt, m[:, 3], m[:, 4], m[:, 5], norm2_g[l], wg, wu, wd)
    xb = xt.reshape(s_len, bsz, d).transpose(1, 0, 2).reshape(t, d)

    head_of_lane = jnp.arange(N_HEADS * QK_NOPE_DIM) // QK_NOPE_DIM
    head_sum = (head_of_lane[:, None] == jnp.arange(LANES)[None, :]).astype(BF16)
    head_spread = jnp.concatenate([head_sum.T, head_sum.T], axis=0)
    seg32 = _seg_ones(N_HEADS * QK_ROPE_DIM, QK_ROPE_DIM)
    per_b = lambda v: v.reshape(bsz, 1, d)
    k_all, v_all = _shared_kv(xb, bsz, per_b(kv_mods[:, 0]), per_b(kv_mods[:, 1]), kv_norm_g, w_kv_a,
                              kv_a_norm_g, w_kv_b, k_nope_norm_g, k_rope_norm_g, head_sum, head_spread,
                              cos_t, sin_t)
    for l in range(n_a, depth):
        j = l - n_a
        m = mods[l]
        bound = _score_bound(mla_q_nope_norm_g[j], mla_q_rope_norm_g[j], k_nope_norm_g, k_rope_norm_g)
        q = _queries(xb, bsz, per_b(m[:, 0]), per_b(m[:, 1]), norm1_g[l], mla_w_dq[j], mla_q_norm_g[j],
                     mla_w_uq[j], mla_q_nope_norm_g[j], mla_q_rope_norm_g[j], head_sum, head_spread, seg32,
                     cos_t, sin_t)
        o = _attention(q, k_all, v_all, bound).reshape(t, N_HEADS * V_HEAD_DIM)
        wg, wu, wd = _ffn_weights(ffn_w_gate[l], ffn_w_up[l], ffn_w_down[l])
        xb = _ffn(xb, per_b(m[:, 3]), per_b(m[:, 4]), per_b(m[:, 5]), norm2_g[l], wg, wu, wd,
                  attn=(o, mla_w_o[j].astype(BF16), per_b(m[:, 2])))
    return xb.reshape(bsz, s_len, d)
```

```python
import functools
import math

import jax
import jax.numpy as jnp
from jax import lax
from jax.experimental import pallas as pl
from jax.experimental.pallas import tpu as pltpu

F32 = jnp.float32
BF16 = jnp.bfloat16

CHUNK = 64
SSM_GROUP = 16
SSM_STATE = 64
N_HEADS = 16
QK_NOPE_DIM = 64
QK_ROPE_DIM = 32
V_HEAD_DIM = 64
ROPE_THETA = 10000.0
ATTN_SCALE = 1.0 / math.sqrt(QK_NOPE_DIM + QK_ROPE_DIM)
Q_SCALE = ATTN_SCALE * math.log2(math.e)
EPS = 1e-6
MASK_VALUE = -1e30
BOUND_SLACK = 1.01
MAX_UNSHIFTED_BOUND = 40.0

LANES = 128
SUBLANES = 8
HEAD_LANES = 128
SLAB_GROUPS = LANES // SSM_GROUP
VMEM_LIMIT = 56 * 1024 * 1024

ROW_TILE = 512
S5_TIME_TILE = 64
SCAN_LANES = 512
FFN_CHUNK = 256
ATTN_TILE = 512
HEADS_PER_STEP = 4


def _params(*sem):
    return pltpu.CompilerParams(dimension_semantics=sem, vmem_limit_bytes=VMEM_LIMIT)


def _rms(x, g):
    return x * lax.rsqrt(jnp.mean(x * x, axis=-1, keepdims=True) + EPS) * g


def _rowwise(fn, a, *mods):
    r = mods[0].shape[0]
    if r == 1:
        return fn(a, *mods)
    rows, d = a.shape
    out = fn(a.reshape(rows // r, r, d), *[m[None] for m in mods])
    return out.reshape(rows, d)


def _modulate(h, shift, scale):
    return _rowwise(lambda a, sh, sc: a * (1.0 + sc) + sh, h, shift, scale)


def _gated_add(x, gate, upd):
    return x + _rowwise(lambda a, g: a * g, upd, gate)


def _dot(a, b):
    return jnp.dot(a, b, preferred_element_type=F32)


_NT_DIMS = (((1,), (1,)), ((), ()))


def _resident(a):
    zeros = (0,) * a.ndim
    return pl.BlockSpec(a.shape, lambda *_: zeros, pipeline_mode=pl.Buffered(1))


def _lane_iota(shape):
    return lax.broadcasted_iota(jnp.int32, shape, len(shape) - 1)


def _head_rstd(x, sum_ref, spread_ref, dim):
    ss = _dot((x * x).astype(BF16), sum_ref[...])
    rstd = lax.rsqrt(ss * (1.0 / dim) + EPS)
    hi = rstd.astype(BF16)
    lo = (rstd - hi.astype(F32)).astype(BF16)
    return _dot(jnp.concatenate([hi, lo], axis=1), spread_ref[...])


def _mods_kernel(c_ref, w_ref, b_ref, o_ref):
    c = c_ref[...]
    ca = c * jax.nn.sigmoid(c)
    o_ref[0] = jnp.dot(ca, w_ref[0], preferred_element_type=F32,
                       precision=lax.Precision.HIGHEST) + b_ref[0]


def _mods(c, w, b, tn=2048):
    nl, d, n = w.shape
    bsz = c.shape[0]
    return pl.pallas_call(
        _mods_kernel,
        grid=(nl, n // tn),
        in_specs=[pl.BlockSpec((bsz, d), lambda l, j: (0, 0)),
                  pl.BlockSpec((1, d, tn), lambda l, j: (l, 0, j)),
                  pl.BlockSpec((1, 1, tn), lambda l, j: (l, 0, j))],
        out_specs=pl.BlockSpec((1, bsz, tn), lambda l, j: (l, 0, j)),
        out_shape=jax.ShapeDtypeStruct((nl, bsz, n), F32),
        compiler_params=_params("parallel", "parallel"),
        name="adaln_mods",
    )(c, w, b.reshape(nl, 1, n))


def _rope_kernel(pos_ref, inv_ref, cos_ref, sin_ref):
    ang = inv_ref[...] * pos_ref[...]
    cos_ref[...] = jnp.cos(ang)
    sin_ref[...] = jnp.sin(ang)


def _rope_tables(positions, tn=4096):
    t = positions.size
    half = QK_ROPE_DIM // 2
    inv = 1.0 / (ROPE_THETA ** (jnp.arange(0, QK_ROPE_DIM, 2, dtype=F32) / QK_ROPE_DIM))
    pos = positions.astype(F32).reshape(1, t)
    cos, sin = pl.pallas_call(
        _rope_kernel,
        grid=(t // tn,),
        in_specs=[pl.BlockSpec((1, tn), lambda i: (0, i)),
                  pl.BlockSpec((half, 1), lambda i: (0, 0))],
        out_specs=[pl.BlockSpec((half, tn), lambda i: (0, i))] * 2,
        out_shape=[jax.ShapeDtypeStruct((half, t), F32)] * 2,
        compiler_params=_params("parallel"),
        name="rope_tables",
    )(pos, inv.reshape(half, 1))
    reps = LANES // half
    return jnp.tile(cos.T, (1, reps)), jnp.tile(sin.T, (1, reps))


def _rope_sign(shape):
    lane = _lane_iota(shape)
    return jnp.where((lane % QK_ROPE_DIM) < (QK_ROPE_DIM // 2), -1.0, 1.0).astype(F32)


def _cmul(ar, ai, br, bi):
    return ar * br - ai * bi, ar * bi + ai * br


def _s5_disc_kernel(lr_ref, li_ref, ldt_ref, br_ref, bi_ref, cr_ref, ci_ref,
                    a2r_ref, a2i_ref, bbr_ref, bbi_ref, abbr_ref, abbi_ref, car_ref, cai_ref, cb_ref):
    lr = lr_ref[0]
    li = li_ref[0]
    dt = jnp.exp(ldt_ref[0])
    mag = jnp.exp(lr * dt)
    a_re = mag * jnp.cos(li * dt)
    a_im = mag * jnp.sin(li * dt)
    den = lr * lr + li * li
    nr = a_re - 1.0
    ni = a_im
    f_re = (nr * lr + ni * li) / den
    f_im = (ni * lr - nr * li) / den
    bb_re, bb_im = _cmul(f_re, f_im, br_ref[0], bi_ref[0])
    cr = cr_ref[0]
    ci = ci_ref[0]
    a2r_ref[0], a2i_ref[0] = _cmul(a_re, a_im, a_re, a_im)
    bbr_ref[0] = bb_re
    bbi_ref[0] = bb_im
    abbr_ref[0], abbi_ref[0] = _cmul(a_re, a_im, bb_re, bb_im)
    car_ref[0], cai_ref[0] = _cmul(cr, ci, a_re, a_im)
    group_dot = functools.partial(jnp.einsum, "gpn,gqn->gpq", preferred_element_type=F32,
                                  precision=lax.Precision.HIGHEST)
    cb_ref[0] = group_dot(cr, bb_re) - group_dot(ci, bb_im)


def _s5_discretise(lam_re, lam_im, log_dt, b_re, b_im, c_re, c_im):
    na, g, n = lam_re.shape
    p = b_re.shape[-1]
    vec = pl.BlockSpec((1, g, 1, n), lambda l: (l, 0, 0, 0))
    mat = pl.BlockSpec((1, g, p, n), lambda l: (l, 0, 0, 0))
    vec_shape = jax.ShapeDtypeStruct((na, g, 1, n), F32)
    mat_shape = jax.ShapeDtypeStruct((na, g, p, n), F32)
    return pl.pallas_call(
        _s5_disc_kernel,
        grid=(na,),
        in_specs=[vec, vec, pl.BlockSpec((1, g, 1, 1), lambda l: (l, 0, 0, 0)), mat, mat, mat, mat],
        out_specs=[vec, vec] + [mat] * 6 + [pl.BlockSpec((1, g, p, p), lambda l: (l, 0, 0, 0))],
        out_shape=[vec_shape] * 2 + [mat_shape] * 6 + [jax.ShapeDtypeStruct((na, g, p, p), F32)],
        compiler_params=_params("parallel"),
        name="s5_discretise",
    )(lam_re.reshape(na, g, 1, n), lam_im.reshape(na, g, 1, n), log_dt.reshape(na, g, 1, 1),
      b_re.transpose(0, 1, 3, 2), b_im.transpose(0, 1, 3, 2), c_re, c_im)


def _slab_block_diag(w_gab):
    g, a, b = w_gab.shape
    ns = g // SLAB_GROUPS
    w = w_gab.reshape(ns, SLAB_GROUPS, a, b)
    eye = jnp.eye(SLAB_GROUPS, dtype=w.dtype)
    return jnp.einsum("kgab,gh->kgahb", w, eye).reshape(ns, SLAB_GROUPS * a, SLAB_GROUPS * b)


def _s5_slab_weights(bb_re, bb_im, abb_re, abb_im, c_re, c_im, ca_re, ca_im, cb):
    col = lambda re, im: jnp.concatenate([_slab_block_diag(re), _slab_block_diag(im)], axis=2)
    b_pair = jnp.concatenate([col(bb_re, bb_im), col(abb_re, abb_im)], axis=1)
    out = lambda w: _slab_block_diag(w.transpose(0, 2, 1))
    c_pair_re = jnp.concatenate([out(c_re), out(ca_re)], axis=2)
    c_pair_im = jnp.concatenate([out(c_im), out(ca_im)], axis=2)
    cb_blk = _slab_block_diag(cb.transpose(0, 2, 1))
    return tuple(w.astype(BF16) for w in (b_pair, c_pair_re, c_pair_im, cb_blk))


def _s5_mixer_kernel(x_ref, shift_ref, scale_ref, gate_ref, g_ref, bpair_ref, cre_ref, cim_ref, cb_ref,
                     a2r_ref, a2i_ref, d_ref, wglu_ref, bglu_ref, o_ref,
                     bu_re, bu_im, st_re, st_im, tail_scr, yf_scr, ys_scr, *, ts, n_slabs, slab_state):
    @pl.when(pl.program_id(0) == 0)
    def _():
        st_re[...] = jnp.zeros_like(st_re)
        st_im[...] = jnp.zeros_like(st_im)
        tail_scr[...] = jnp.zeros_like(tail_scr)

    pairs = ts // 2
    pair_rows = pairs * SUBLANES
    x = x_ref[...]
    d = x.shape[1]
    h = _modulate(_rms(x, g_ref[...]), shift_ref[...], scale_ref[...])
    h3 = h.reshape(pairs, 2 * SUBLANES, d)
    h_first = h3[:, :SUBLANES].reshape(pair_rows, d).astype(BF16)
    h_second = h3[:, SUBLANES:].reshape(pair_rows, d).astype(BF16)

    for k in range(n_slabs):
        lanes = slice(k * LANES, (k + 1) * LANES)
        bu = _dot(jnp.concatenate([h_second[:, lanes], h_first[:, lanes]], axis=1), bpair_ref[k])
        bu_re[:, k * slab_state:(k + 1) * slab_state] = bu[:, :slab_state]
        bu_im[:, k * slab_state:(k + 1) * slab_state] = bu[:, slab_state:]

    n_state = n_slabs * slab_state
    for cb in range(n_state // SCAN_LANES):
        sl = slice(cb * SCAN_LANES, (cb + 1) * SCAN_LANES)
        ar = jnp.broadcast_to(a2r_ref[:, sl], (SUBLANES, SCAN_LANES))
        ai = jnp.broadcast_to(a2i_ref[:, sl], (SUBLANES, SCAN_LANES))

        def step(i, carry, sl=sl, ar=ar, ai=ai):
            sr, si = carry
            r0 = pl.multiple_of(i * SUBLANES, SUBLANES)
            nr = ar * sr - ai * si + bu_re[pl.ds(r0, SUBLANES), sl]
            ni = ar * si + ai * sr + bu_im[pl.ds(r0, SUBLANES), sl]
            bu_re[pl.ds(r0, SUBLANES), sl] = nr
            bu_im[pl.ds(r0, SUBLANES), sl] = ni
            return nr, ni

        sr, si = lax.fori_loop(0, pairs, step, (st_re[:, sl], st_im[:, sl]), unroll=True)
        st_re[:, sl] = sr
        st_im[:, sl] = si

    for k in range(n_slabs):
        ssl = slice(k * slab_state, (k + 1) * slab_state)
        lanes = slice(k * LANES, (k + 1) * LANES)
        z = _dot(bu_re[:, ssl].astype(BF16), cre_ref[k]) - _dot(bu_im[:, ssl].astype(BF16), cim_ref[k])
        ys_scr[:, lanes] = z[:, :LANES]
        carried = jnp.concatenate([tail_scr[:, lanes], z[:pair_rows - SUBLANES, LANES:]], axis=0)
        tail_scr[:, lanes] = z[pair_rows - SUBLANES:, LANES:]
        yf_scr[:, lanes] = carried + _dot(h_first[:, lanes], cb_ref[k])

    y = jnp.concatenate([yf_scr[...].reshape(pairs, SUBLANES, d), ys_scr[...].reshape(pairs, SUBLANES, d)],
                        axis=1).reshape(2 * pair_rows, d)
    y = y + d_ref[...] * h
    gl = jax.nn.gelu(y)
    z = _dot(gl.astype(BF16), wglu_ref[...]) + bglu_ref[...]
    mix = gl * jax.nn.sigmoid(z)
    o_ref[...] = _gated_add(x, gate_ref[...], mix)


def _s5_mixer(x_tb, shift, scale, gate, norm_g, slab_weights, a2_re, a2_im, d_skip, w_glu, b_glu, bsz):
    t, d = x_tb.shape
    ts = S5_TIME_TILE
    rows = ts * bsz
    pair_rows = rows // 2
    b_pair, c_pair_re, c_pair_im, cb_blk = slab_weights
    n_slabs, _, two_state = b_pair.shape
    slab_state = two_state // 2
    n_state = n_slabs * slab_state
    full = _resident
    row = lambda a: a.reshape(1, -1)
    args = (x_tb, shift, scale, gate, row(norm_g), b_pair, c_pair_re, c_pair_im, cb_blk, row(a2_re), row(a2_im),
            row(d_skip), w_glu, row(b_glu))
    return pl.pallas_call(
        functools.partial(_s5_mixer_kernel, ts=ts, n_slabs=n_slabs, slab_state=slab_state),
        grid=(t // rows,),
        in_specs=[pl.BlockSpec((rows, d), lambda i: (i, 0))] + [full(a) for a in args[1:]],
        out_specs=pl.BlockSpec((rows, d), lambda i: (i, 0)),
        out_shape=jax.ShapeDtypeStruct((t, d), F32),
        scratch_shapes=[pltpu.VMEM((pair_rows, n_state), F32), pltpu.VMEM((pair_rows, n_state), F32),
                        pltpu.VMEM((bsz, n_state), F32), pltpu.VMEM((bsz, n_state), F32),
                        pltpu.VMEM((bsz, d), F32), pltpu.VMEM((pair_rows, d), F32),
                        pltpu.VMEM((pair_rows, d), F32)],
        compiler_params=_params("arbitrary"),
        name="s5_mixer",
    )(*args)


def _ffn_kernel(*refs, n_chunks, with_attn):
    if with_attn:
        (x_ref, o_in_ref, wo_ref, gate1_ref, shift_ref, scale_ref, gate2_ref, g_ref,
         wg_ref, wu_ref, wd_ref, out_ref, h_scr, acc_scr) = refs
        x = x_ref[...] + gate1_ref[0] * _dot(o_in_ref[...], wo_ref[...])
        shift, scale, gate2 = shift_ref[0], scale_ref[0], gate2_ref[0]
    else:
        (x_ref, shift_ref, scale_ref, gate2_ref, g_ref,
         wg_ref, wu_ref, wd_ref, out_ref, h_scr, acc_scr) = refs
        x = x_ref[...]
        shift, scale, gate2 = shift_ref[...], scale_ref[...], gate2_ref[...]

    h_scr[...] = _modulate(_rms(x, g_ref[...]), shift, scale).astype(BF16)
    acc_scr[...] = jnp.zeros_like(acc_scr)

    hb = h_scr[...]
    for f in range(n_chunks):
        cols = slice(f * FFN_CHUNK, (f + 1) * FFN_CHUNK)
        gt = _dot(hb, wg_ref[:, cols])
        up = _dot(hb, wu_ref[:, cols])
        act = (gt * jax.nn.sigmoid(gt) * up).astype(BF16)
        acc_scr[...] += _dot(act, wd_ref[cols, :])
    out_ref[...] = _gated_add(x, gate2, acc_scr[...])


def _ffn(x, shift, scale, gate2, norm_g, wg, wu, wd, attn=None):
    t, d = x.shape
    n_chunks = wg.shape[1] // FFN_CHUNK
    tm = ROW_TILE
    full = _resident
    xspec = pl.BlockSpec((tm, d), lambda i: (i, 0))
    g2 = norm_g.reshape(1, d)
    if attn is None:
        args = (x, shift, scale, gate2, g2, wg, wu, wd)
        in_specs = [xspec] + [full(a) for a in args[1:]]
    else:
        o_in, w_o, gate1 = attn
        tiles_per_batch = (t // gate1.shape[0]) // tm
        per_b = pl.BlockSpec((1, 1, d), lambda i: (i // tiles_per_batch, 0, 0))
        args = (x, o_in, w_o, gate1, shift, scale, gate2, g2, wg, wu, wd)
        in_specs = [xspec, xspec, full(w_o), per_b, per_b, per_b, per_b, full(g2), full(wg), full(wu), full(wd)]
    return pl.pallas_call(
        functools.partial(_ffn_kernel, n_chunks=n_chunks, with_attn=attn is not None),
        grid=(t // tm,),
        in_specs=in_specs,
        out_specs=xspec,
        out_shape=jax.ShapeDtypeStruct((t, d), F32),
        scratch_shapes=[pltpu.VMEM((tm, d), BF16), pltpu.VMEM((tm, d), F32)],
        compiler_params=_params("parallel"),
        name="ffn_attn_out" if attn is not None else "ffn",
    )(*args)


def _head_tile(nope, rope_at_64, h):
    blk = nope[:, (h // 2) * LANES:(h // 2 + 1) * LANES]
    if h % 2:
        blk = pltpu.roll(blk, QK_NOPE_DIM, 1)
    return jnp.where(_lane_iota(blk.shape) < QK_NOPE_DIM, blk, rope_at_64)


def _kv_kernel(x_ref, shift_ref, scale_ref, g_ref, wa_ref, ga_ref, wkn_ref, wv_ref, gkn_ref,
               gkr_ref, gkrs_ref, sum_ref, spread_ref, ct_ref, st_ref, k_ref, v_ref):
    x = x_ref[...]
    hk = _modulate(_rms(x, g_ref[...]), shift_ref[0], scale_ref[0])
    kva = _dot(hk.astype(BF16), wa_ref[...])
    rank = ga_ref.shape[1]
    cb = _rms(kva[:, :rank], ga_ref[...]).astype(BF16)

    kn = _dot(cb, wkn_ref[...])
    kn = kn * _head_rstd(kn, sum_ref, spread_ref, QK_NOPE_DIM) * gkn_ref[...]

    a = kva[:, rank:rank + LANES]
    asw = kva[:, rank + LANES:rank + 2 * LANES]
    rstd = lax.rsqrt(jnp.sum(a * a, axis=-1, keepdims=True) * (1.0 / QK_ROPE_DIM) + EPS)
    sgn = _rope_sign(a.shape)
    kr = rstd * (a * gkr_ref[...] * ct_ref[...] + asw * gkrs_ref[...] * (st_ref[...] * sgn))
    kr64 = pltpu.roll(kr, QK_NOPE_DIM, 1)

    v = _dot(cb, wv_ref[...])
    ones_col = (_lane_iota((1, LANES)) == V_HEAD_DIM).astype(F32)
    for h in range(N_HEADS):
        k_ref[0, h] = _head_tile(kn, kr64, h).astype(BF16)
        v_ref[0, h] = (v[:, h * LANES:(h + 1) * LANES] + ones_col).astype(BF16)


def _shared_kv(x, bsz, k_shift, k_scale, kv_norm_g, w_kv_a, kv_a_norm_g, w_kv_b, k_nope_g, k_rope_g,
               head_sum, head_spread, ct, st):
    t, d = x.shape
    s_len = t // bsz
    tm = ROW_TILE
    tiles = s_len // tm
    rank = kv_a_norm_g.shape[0]
    half = QK_ROPE_DIM // 2
    pad = jnp.zeros((d, LANES - QK_ROPE_DIM), F32)
    w_rope = w_kv_a[:, rank:]
    w_rope_sw = jnp.concatenate([w_rope[:, half:], w_rope[:, :half]], axis=1)
    wa = jnp.concatenate([w_kv_a[:, :rank], w_rope, pad, w_rope_sw, pad], axis=1).astype(BF16)
    wkv = w_kv_b.reshape(rank, N_HEADS, QK_NOPE_DIM + V_HEAD_DIM)
    wkn = wkv[:, :, :QK_NOPE_DIM].reshape(rank, N_HEADS * QK_NOPE_DIM).astype(BF16)
    wv = jnp.pad(wkv[:, :, QK_NOPE_DIM:], ((0, 0), (0, 0), (0, HEAD_LANES - V_HEAD_DIM)))
    wv = wv.reshape(rank, N_HEADS * HEAD_LANES).astype(BF16)
    gkn = jnp.tile(k_nope_g, N_HEADS).reshape(1, -1)
    lane_pad = lambda g: jnp.pad(g, (0, LANES - g.shape[0])).reshape(1, LANES)
    gkr = lane_pad(k_rope_g)
    gkrs = lane_pad(jnp.concatenate([k_rope_g[half:], k_rope_g[:half]]))
    full = _resident
    xspec = pl.BlockSpec((tm, d), lambda b, i: (b * tiles + i, 0))
    tab = pl.BlockSpec((tm, LANES), lambda b, i: (b * tiles + i, 0))
    per_b = pl.BlockSpec((1, 1, d), lambda b, i: (b, 0, 0))
    hspec = pl.BlockSpec((1, N_HEADS, tm, HEAD_LANES), lambda b, i: (b, 0, i, 0))
    args = (x, k_shift, k_scale, kv_norm_g.reshape(1, d), wa, kv_a_norm_g.reshape(1, rank), wkn, wv,
            gkn, gkr, gkrs, head_sum, head_spread, ct, st)
    in_specs = [xspec, per_b, per_b] + [full(a) for a in args[3:13]] + [tab, tab]
    return pl.pallas_call(
        _kv_kernel,
        grid=(bsz, tiles),
        in_specs=in_specs,
        out_specs=[hspec, hspec],
        out_shape=[jax.ShapeDtypeStruct((bsz, N_HEADS, s_len, HEAD_LANES), BF16)] * 2,
        compiler_params=_params("parallel", "parallel"),
        name="mla_shared_kv",
    )(*args)


def _q_kernel(x_ref, shift_ref, scale_ref, g_ref, wdq_ref, gq_ref, wn_ref, wr_ref, wrs_ref,
              gn_ref, gr_ref, grs_ref, sum_ref, spread_ref, seg32_ref, ct_ref, st_ref, q_ref):
    x = x_ref[...]
    h = _modulate(_rms(x, g_ref[...]), shift_ref[0], scale_ref[0])
    cq = _rms(_dot(h.astype(BF16), wdq_ref[...]), gq_ref[...]).astype(BF16)

    qn = _dot(cq, wn_ref[...])
    qn = qn * _head_rstd(qn, sum_ref, spread_ref, QK_NOPE_DIM) * (gn_ref[...] * Q_SCALE)

    a = _dot(cq, wr_ref[...])
    asw = _dot(cq, wrs_ref[...])
    ssr = _dot((a * a).astype(BF16), seg32_ref[...])
    rstd = lax.rsqrt(ssr * (1.0 / QK_ROPE_DIM) + EPS) * Q_SCALE
    ct = ct_ref[...]
    st = st_ref[...] * _rope_sign(ct.shape)
    heads_per_blk = LANES // QK_ROPE_DIM
    for blk in range(N_HEADS // heads_per_blk):
        sl = slice(blk * LANES, (blk + 1) * LANES)
        qr = rstd[:, sl] * (a[:, sl] * gr_ref[:, sl] * ct + asw[:, sl] * grs_ref[:, sl] * st)
        lane = _lane_iota(qr.shape)
        for j in range(heads_per_blk):
            h_idx = blk * heads_per_blk + j
            shift = (QK_NOPE_DIM - j * QK_ROPE_DIM) % LANES
            r = pltpu.roll(qr, shift, 1) if shift else qr
            r = jnp.where(lane < QK_NOPE_DIM + QK_ROPE_DIM, r, 0.0)
            q_ref[0, h_idx] = _head_tile(qn, r, h_idx).astype(BF16)


def _queries(x, bsz, shift, scale, norm_g, w_dq, q_norm_g, w_uq, q_nope_g, q_rope_g, head_sum, head_spread,
             seg32, ct, st):
    t, d = x.shape
    s_len = t // bsz
    tm = ROW_TILE
    tiles = s_len // tm
    rank = w_dq.shape[1]
    half = QK_ROPE_DIM // 2
    wq = w_uq.reshape(rank, N_HEADS, QK_NOPE_DIM + QK_ROPE_DIM)
    wn = wq[:, :, :QK_NOPE_DIM].reshape(rank, -1).astype(BF16)
    wr = wq[:, :, QK_NOPE_DIM:].reshape(rank, -1).astype(BF16)
    wrs = jnp.concatenate([wq[:, :, QK_NOPE_DIM + half:], wq[:, :, QK_NOPE_DIM:QK_NOPE_DIM + half]], axis=2)
    wrs = wrs.reshape(rank, -1).astype(BF16)
    gn = jnp.tile(q_nope_g, N_HEADS).reshape(1, -1)
    gr = jnp.tile(q_rope_g, N_HEADS).reshape(1, -1)
    grs = jnp.tile(jnp.concatenate([q_rope_g[half:], q_rope_g[:half]]), N_HEADS).reshape(1, -1)
    full = _resident
    xspec = pl.BlockSpec((tm, d), lambda b, i: (b * tiles + i, 0))
    tab = pl.BlockSpec((tm, LANES), lambda b, i: (b * tiles + i, 0))
    per_b = pl.BlockSpec((1, 1, d), lambda b, i: (b, 0, 0))
    hspec = pl.BlockSpec((1, N_HEADS, tm, HEAD_LANES), lambda b, i: (b, 0, i, 0))
    args = (x, shift, scale, norm_g.reshape(1, d), w_dq.astype(BF16), q_norm_g.reshape(1, rank), wn, wr, wrs,
            gn, gr, grs, head_sum, head_spread, seg32, ct, st)
    in_specs = [xspec, per_b, per_b] + [full(a) for a in args[3:15]] + [tab, tab]
    return pl.pallas_call(
        _q_kernel,
        grid=(bsz, tiles),
        in_specs=in_specs,
        out_specs=hspec,
        out_shape=jax.ShapeDtypeStruct((bsz, N_HEADS, s_len, HEAD_LANES), BF16),
        compiler_params=_params("parallel", "parallel"),
        name="mla_queries",
    )(*args)


def _chunk_offsets(tile):
    row_chunk = lax.broadcasted_iota(jnp.int32, (tile, tile), 0) // CHUNK
    col_chunk = lax.broadcasted_iota(jnp.int32, (tile, tile), 1) // CHUNK
    return row_chunk - col_chunk


def _merge_head_pair(accs):
    outs = [acc / acc[:, V_HEAD_DIM:V_HEAD_DIM + 1] for acc in accs]
    lane = _lane_iota(outs[0].shape)
    tiles = [jnp.where(lane < V_HEAD_DIM, outs[i], pltpu.roll(outs[i + 1], V_HEAD_DIM, 1))
             for i in range(0, len(outs), 2)]
    return tiles[0] if len(tiles) == 1 else jnp.concatenate(tiles, axis=1)


def _attn_online_kernel(q_ref, k_ref, v_ref, o_ref, m_scr, acc_scr, *, tile):
    i = pl.program_id(2)
    diag_mask = _chunk_offsets(tile) >= 0
    qs = [q_ref[0, hh] for hh in range(HEADS_PER_STEP)]

    def tile_update(off, diagonal):
        for hh in range(HEADS_PER_STEP):
            s = lax.dot_general(qs[hh], k_ref[0, hh, pl.ds(off, tile), :], _NT_DIMS,
                                preferred_element_type=F32)
            if diagonal:
                s = jnp.where(diag_mask, s, MASK_VALUE)
            row_max = jnp.max(s, axis=-1, keepdims=True)
            m = row_max if diagonal else m_scr[hh]
            m_new = row_max if diagonal else jnp.maximum(m, row_max)
            pv = _dot(jnp.exp2(s - m_new).astype(BF16), v_ref[0, hh, pl.ds(off, tile), :])
            acc_scr[hh] = pv if diagonal else acc_scr[hh] * jnp.exp2(m - m_new) + pv
            m_scr[hh] = m_new

    tile_update(pl.multiple_of(i * tile, tile), True)

    def below_diagonal(j, carry):
        tile_update(pl.multiple_of(j * tile, tile), False)
        return carry

    lax.fori_loop(0, i, below_diagonal, 0)
    o_ref[0] = _merge_head_pair([acc_scr[hh] for hh in range(HEADS_PER_STEP)]).astype(o_ref.dtype)


def _attn_bounded_kernel(q_ref, k_ref, v_ref, o_ref, *, tile, n_tiles):
    g = pl.program_id(2)
    lo, hi = g, n_tiles - 1 - g
    diag_mask = _chunk_offsets(tile) >= 0
    zero = jnp.zeros((tile, HEAD_LANES), F32)
    acc_lo = [zero] * HEADS_PER_STEP
    acc_hi = [zero] * HEADS_PER_STEP

    def rows(ref, hh, idx):
        return ref[0, hh, pl.ds(pl.multiple_of(idx * tile, tile), tile), :]

    def weighted_values(q, k, v, mask):
        s = lax.dot_general(q, k, _NT_DIMS, preferred_element_type=F32)
        if mask is not None:
            s = jnp.where(mask, s, MASK_VALUE)
        return _dot(jnp.exp2(s).astype(BF16), v)

    def diagonal(idx, accs):
        for hh in range(HEADS_PER_STEP):
            accs[hh] = accs[hh] + weighted_values(rows(q_ref, hh, idx), rows(k_ref, hh, idx),
                                                  rows(v_ref, hh, idx), diag_mask)

    diagonal(lo, acc_lo)
    diagonal(hi, acc_hi)
    max_lo = n_tiles // 2 - 1
    for slot in range(1, n_tiles):
        owner_known = slot > max_lo
        if owner_known:
            q_idx, k_idx = hi, slot - 1 - lo
        else:
            is_lo = slot <= lo
            q_idx = jnp.where(is_lo, lo, hi)
            k_idx = jnp.where(is_lo, slot - 1, slot - 1 - lo)
        for hh in range(HEADS_PER_STEP):
            pv = weighted_values(rows(q_ref, hh, q_idx), rows(k_ref, hh, k_idx), rows(v_ref, hh, k_idx), None)
            if owner_known:
                acc_hi[hh] = acc_hi[hh] + pv
            else:
                acc_lo[hh] = acc_lo[hh] + jnp.where(is_lo, pv, 0.0)
                acc_hi[hh] = acc_hi[hh] + jnp.where(is_lo, 0.0, pv)
    o_ref[0, pl.ds(pl.multiple_of(lo * tile, tile), tile), :] = _merge_head_pair(acc_lo).astype(o_ref.dtype)
    o_ref[0, pl.ds(pl.multiple_of(hi * tile, tile), tile), :] = _merge_head_pair(acc_hi).astype(o_ref.dtype)


def _score_bound(q_nope_g, q_rope_g, k_nope_g, k_rope_g):
    def sq_len(g_nope, g_rope):
        return QK_NOPE_DIM * jnp.max(g_nope * g_nope) + QK_ROPE_DIM * jnp.max(g_rope * g_rope)
    return Q_SCALE * jnp.sqrt(sq_len(q_nope_g, q_rope_g) * sq_len(k_nope_g, k_rope_g)) * BOUND_SLACK


def _attention(q, k, v, bound):
    return lax.cond(bound <= MAX_UNSHIFTED_BOUND, _attention_bounded, _attention_online, q, k, v)


def _attention_online(q, k, v):
    bsz, nh, s_len, hl = q.shape
    tile = ATTN_TILE
    hp = HEADS_PER_STEP
    whole = pl.BlockSpec((1, hp, s_len, hl), lambda b, h, i: (b, h, 0, 0))
    return pl.pallas_call(
        functools.partial(_attn_online_kernel, tile=tile),
        grid=(bsz, nh // hp, s_len // tile),
        in_specs=[pl.BlockSpec((1, hp, tile, hl), lambda b, h, i: (b, h, i, 0)), whole, whole],
        out_specs=pl.BlockSpec((1, tile, hp * V_HEAD_DIM), lambda b, h, i: (b, i, h)),
        out_shape=jax.ShapeDtypeStruct((bsz, s_len, nh * V_HEAD_DIM), BF16),
        scratch_shapes=[pltpu.VMEM((hp, tile, 1), F32), pltpu.VMEM((hp, tile, hl), F32)],
        compiler_params=_params("parallel", "parallel", "arbitrary"),
        name="mla_attention_online",
    )(q, k, v)


def _attention_bounded(q, k, v):
    bsz, nh, s_len, hl = q.shape
    tile = ATTN_TILE
    hp = HEADS_PER_STEP
    n_tiles = s_len // tile
    whole = pl.BlockSpec((1, hp, s_len, hl), lambda b, h, g: (b, h, 0, 0))
    return pl.pallas_call(
        functools.partial(_attn_bounded_kernel, tile=tile, n_tiles=n_tiles),
        grid=(bsz, nh // hp, n_tiles // 2),
        in_specs=[whole, whole, whole],
        out_specs=pl.BlockSpec((1, s_len, hp * V_HEAD_DIM), lambda b, h, g: (b, 0, h)),
        out_shape=jax.ShapeDtypeStruct((bsz, s_len, nh * V_HEAD_DIM), BF16),
        compiler_params=_params("parallel", "parallel", "arbitrary"),
        name="mla_attention_bounded",
    )(q, k, v)


def _seg_ones(n, seg):
    idx = jnp.arange(n) // seg
    return (idx[:, None] == idx[None, :]).astype(BF16)


def _ffn_weights(w_gate, w_up, w_down):
    return w_gate.astype(BF16), w_up.astype(BF16), w_down.astype(BF16)


def kernel(x, c, positions, ada_w, ada_b, norm1_g, norm2_g, ffn_w_gate, ffn_w_up, ffn_w_down, s5_lam_re, s5_lam_im, s5_log_dt, s5_b_re, s5_b_im, s5_c_re, s5_c_im, s5_d, s5_w_glu, s5_b_glu, kv_ada_w, kv_ada_b, kv_norm_g, w_kv_a, kv_a_norm_g, w_kv_b, k_nope_norm_g, k_rope_norm_g, mla_w_dq, mla_q_norm_g, mla_w_uq, mla_q_nope_norm_g, mla_q_rope_norm_g, mla_w_o):
    bsz, s_len, d = x.shape
    depth = ada_w.shape[0]
    n_a = s5_lam_re.shape[0]
    t = bsz * s_len

    mods = _mods(c, ada_w, ada_b).reshape(depth, bsz, 6, d)
    kv_mods = _mods(c, kv_ada_w[None], kv_ada_b[None]).reshape(bsz, 2, d)
    cos_t, sin_t = _rope_tables(positions)

    a2_re, a2_im, *s5_mats = _s5_discretise(s5_lam_re, s5_lam_im, s5_log_dt, s5_b_re, s5_b_im, s5_c_re, s5_c_im)

    xt = x.transpose(1, 0, 2).reshape(t, d)
    for l in range(n_a):
        m = mods[l]
        bb_re, bb_im, abb_re, abb_im, ca_re, ca_im, cb = (w[l] for w in s5_mats)
        slab_weights = _s5_slab_weights(bb_re, bb_im, abb_re, abb_im, s5_c_re[l], s5_c_im[l], ca_re, ca_im, cb)
        xt = _s5_mixer(xt, m[:, 0], m[:, 1], m[:, 2], norm1_g[l], slab_weights, a2_re[l], a2_im[l], s5_d[l],
                       s5_w_glu[l].astype(BF16), s5_b_glu[l], bsz)
        wg, wu, wd = _ffn_weights(ffn_w_gate[l], ffn_w_up[l], ffn_w_down[l])
        xt = _ffn(xt, m[:, 3], m[:, 4], m[:, 5], norm2_g[l], wg, wu, wd)
    xb = xt.reshape(s_len, bsz, d).transpose(1, 0, 2).reshape(t, d)

    head_of_lane = jnp.arange(N_HEADS * QK_NOPE_DIM) // QK_NOPE_DIM
    head_sum = (head_of_lane[:, None] == jnp.arange(LANES)[None, :]).astype(BF16)
    head_spread = jnp.concatenate([head_sum.T, head_sum.T], axis=0)
    seg32 = _seg_ones(N_HEADS * QK_ROPE_DIM, QK_ROPE_DIM)
    per_b = lambda v: v.reshape(bsz, 1, d)
    k_all, v_all = _shared_kv(xb, bsz, per_b(kv_mods[:, 0]), per_b(kv_mods[:, 1]), kv_norm_g, w_kv_a,
                              kv_a_norm_g, w_kv_b, k_nope_norm_g, k_rope_norm_g, head_sum, head_spread,
                              cos_t, sin_t)
    for l in range(n_a, depth):
        j = l - n_a
        m = mods[l]
        bound = _score_bound(mla_q_nope_norm_g[j], mla_q_rope_norm_g[j], k_nope_norm_g, k_rope_norm_g)
        q = _queries(xb, bsz, per_b(m[:, 0]), per_b(m[:, 1]), norm1_g[l], mla_w_dq[j], mla_q_norm_g[j],
                     mla_w_uq[j], mla_q_nope_norm_g[j], mla_q_rope_norm_g[j], head_sum, head_spread, seg32,
                     cos_t, sin_t)
        o = _attention(q, k_all, v_all, bound).reshape(t, N_HEADS * V_HEAD_DIM)
        wg, wu, wd = _ffn_weights(ffn_w_gate[l], ffn_w_up[l], ffn_w_down[l])
        xb = _ffn(xb, per_b(m[:, 3]), per_b(m[:, 4]), per_b(m[:, 5]), norm2_g[l], wg, wu, wd,
                  attn=(o, mla_w_o[j].astype(BF16), per_b(m[:, 2])))
    return xb.reshape(bsz, s_len, d)
```

```python
import functools
import math

import jax
import jax.numpy as jnp
from jax import lax
from jax.experimental import pallas as pl
from jax.experimental.pallas import tpu as pltpu

F32 = jnp.float32
BF16 = jnp.bfloat16

CHUNK = 64
SSM_GROUP = 16
SSM_STATE = 64
N_HEADS = 16
QK_NOPE_DIM = 64
QK_ROPE_DIM = 32
V_HEAD_DIM = 64
ROPE_THETA = 10000.0
ATTN_SCALE = 1.0 / math.sqrt(QK_NOPE_DIM + QK_ROPE_DIM)
Q_SCALE = ATTN_SCALE * math.log2(math.e)
EPS = 1e-6
MASK_VALUE = -1e30
BOUND_SLACK = 1.01
MAX_UNSHIFTED_BOUND = 40.0

LANES = 128
SUBLANES = 8
HEAD_LANES = 128
SLAB_GROUPS = LANES // SSM_GROUP
VMEM_LIMIT = 56 * 1024 * 1024

ROW_TILE = 512
ROW_BLOCKS = 2
S5_TIME_TILE = 64
FFN_CHUNK = 256
ATTN_TILE = 512
HEADS_PER_STEP = 4


def _params(*sem):
    return pltpu.CompilerParams(dimension_semantics=sem, vmem_limit_bytes=VMEM_LIMIT)


def _rms(x, g):
    return x * lax.rsqrt(jnp.mean(x * x, axis=-1, keepdims=True) + EPS) * g


def _rowwise(fn, a, *mods):
    r = mods[0].shape[0]
    if r == 1:
        return fn(a, *mods)
    rows, d = a.shape
    out = fn(a.reshape(rows // r, r, d), *[m[None] for m in mods])
    return out.reshape(rows, d)


def _modulate(h, shift, scale):
    return _rowwise(lambda a, sh, sc: a * (1.0 + sc) + sh, h, shift, scale)


def _gated_add(x, gate, upd):
    return x + _rowwise(lambda a, g: a * g, upd, gate)


def _dot(a, b):
    return jnp.dot(a, b, preferred_element_type=F32)


_NT_DIMS = (((1,), (1,)), ((), ()))


def _resident(a):
    zeros = (0,) * a.ndim
    return pl.BlockSpec(a.shape, lambda *_: zeros, pipeline_mode=pl.Buffered(1))


def _lane_iota(shape):
    return lax.broadcasted_iota(jnp.int32, shape, len(shape) - 1)


def _row_blocks(n_rows):
    step = n_rows // ROW_BLOCKS
    return [slice(i * step, (i + 1) * step) for i in range(ROW_BLOCKS)]


def _run_staggered(blocks):
    waiting, live = list(blocks), []
    while waiting or live:
        if waiting:
            live.append(waiting.pop(0))
        for gen in list(live):
            if next(gen, StopIteration) is StopIteration:
                live.remove(gen)


def _head_rstd(x, sum_ref, spread_ref, dim):
    ss = _dot((x * x).astype(BF16), sum_ref[...])
    rstd = lax.rsqrt(ss * (1.0 / dim) + EPS)
    hi = rstd.astype(BF16)
    lo = (rstd - hi.astype(F32)).astype(BF16)
    return _dot(jnp.concatenate([hi, lo], axis=1), spread_ref[...])


def _mods_kernel(c_ref, w_ref, b_ref, o_ref):
    c = c_ref[...]
    ca = c * jax.nn.sigmoid(c)
    o_ref[0] = _dot(ca.astype(BF16), w_ref[0].astype(BF16)) + b_ref[0]


def _mods(c, w, b, tn=2048):
    nl, d, n = w.shape
    bsz = c.shape[0]
    return pl.pallas_call(
        _mods_kernel,
        grid=(nl, n // tn),
        in_specs=[pl.BlockSpec((bsz, d), lambda l, j: (0, 0)),
                  pl.BlockSpec((1, d, tn), lambda l, j: (l, 0, j)),
                  pl.BlockSpec((1, 1, tn), lambda l, j: (l, 0, j))],
        out_specs=pl.BlockSpec((1, bsz, tn), lambda l, j: (l, 0, j)),
        out_shape=jax.ShapeDtypeStruct((nl, bsz, n), F32),
        compiler_params=_params("parallel", "parallel"),
        name="adaln_mods",
    )(c, w, b.reshape(nl, 1, n))


def _rope_kernel(pos_ref, inv_ref, cos_ref, sin_ref):
    ang = inv_ref[...] * pos_ref[...]
    cos_ref[...] = jnp.cos(ang)
    sin_ref[...] = jnp.sin(ang)


def _rope_tables(positions, tn=4096):
    t = positions.size
    half = QK_ROPE_DIM // 2
    inv = 1.0 / (ROPE_THETA ** (jnp.arange(0, QK_ROPE_DIM, 2, dtype=F32) / QK_ROPE_DIM))
    pos = positions.astype(F32).reshape(1, t)
    cos, sin = pl.pallas_call(
        _rope_kernel,
        grid=(t // tn,),
        in_specs=[pl.BlockSpec((1, tn), lambda i: (0, i)),
                  pl.BlockSpec((half, 1), lambda i: (0, 0))],
        out_specs=[pl.BlockSpec((half, tn), lambda i: (0, i))] * 2,
        out_shape=[jax.ShapeDtypeStruct((half, t), F32)] * 2,
        compiler_params=_params("parallel"),
        name="rope_tables",
    )(pos, inv.reshape(half, 1))
    reps = LANES // half
    return jnp.tile(cos.T, (1, reps)), jnp.tile(sin.T, (1, reps))


def _rope_sign(shape):
    lane = _lane_iota(shape)
    return jnp.where((lane % QK_ROPE_DIM) < (QK_ROPE_DIM // 2), -1.0, 1.0).astype(F32)


def _cmul(ar, ai, br, bi):
    return ar * br - ai * bi, ar * bi + ai * br


def _s5_disc_kernel(lr_ref, li_ref, ldt_ref, br_ref, bi_ref, cr_ref, ci_ref,
                    a2r_ref, a2i_ref, bbr_ref, bbi_ref, abbr_ref, abbi_ref, car_ref, cai_ref, cb_ref):
    lr = lr_ref[0]
    li = li_ref[0]
    dt = jnp.exp(ldt_ref[0])
    mag = jnp.exp(lr * dt)
    a_re = mag * jnp.cos(li * dt)
    a_im = mag * jnp.sin(li * dt)
    den = lr * lr + li * li
    nr = a_re - 1.0
    ni = a_im
    f_re = (nr * lr + ni * li) / den
    f_im = (ni * lr - nr * li) / den
    bb_re, bb_im = _cmul(f_re, f_im, br_ref[0], bi_ref[0])
    cr = cr_ref[0]
    ci = ci_ref[0]
    a2r_ref[0], a2i_ref[0] = _cmul(a_re, a_im, a_re, a_im)
    bbr_ref[0] = bb_re
    bbi_ref[0] = bb_im
    abbr_ref[0], abbi_ref[0] = _cmul(a_re, a_im, bb_re, bb_im)
    car_ref[0], cai_ref[0] = _cmul(cr, ci, a_re, a_im)
    group_dot = functools.partial(jnp.einsum, "gpn,gqn->gpq", preferred_element_type=F32,
                                  precision=lax.Precision.HIGHEST)
    cb_ref[0] = group_dot(cr, bb_re) - group_dot(ci, bb_im)


def _s5_discretise(lam_re, lam_im, log_dt, b_re, b_im, c_re, c_im):
    na, g, n = lam_re.shape
    p = b_re.shape[-1]
    vec = pl.BlockSpec((1, g, 1, n), lambda l: (l, 0, 0, 0))
    mat = pl.BlockSpec((1, g, p, n), lambda l: (l, 0, 0, 0))
    vec_shape = jax.ShapeDtypeStruct((na, g, 1, n), F32)
    mat_shape = jax.ShapeDtypeStruct((na, g, p, n), F32)
    return pl.pallas_call(
        _s5_disc_kernel,
        grid=(na,),
        in_specs=[vec, vec, pl.BlockSpec((1, g, 1, 1), lambda l: (l, 0, 0, 0)), mat, mat, mat, mat],
        out_specs=[vec, vec] + [mat] * 6 + [pl.BlockSpec((1, g, p, p), lambda l: (l, 0, 0, 0))],
        out_shape=[vec_shape] * 2 + [mat_shape] * 6 + [jax.ShapeDtypeStruct((na, g, p, p), F32)],
        compiler_params=_params("parallel"),
        name="s5_discretise",
    )(lam_re.reshape(na, g, 1, n), lam_im.reshape(na, g, 1, n), log_dt.reshape(na, g, 1, 1),
      b_re.transpose(0, 1, 3, 2), b_im.transpose(0, 1, 3, 2), c_re, c_im)


def _slab_block_diag(w_gab):
    g, a, b = w_gab.shape
    ns = g // SLAB_GROUPS
    w = w_gab.reshape(ns, SLAB_GROUPS, a, b)
    eye = jnp.eye(SLAB_GROUPS, dtype=w.dtype)
    return jnp.einsum("kgab,gh->kgahb", w, eye).reshape(ns, SLAB_GROUPS * a, SLAB_GROUPS * b)


def _s5_slab_weights(bb_re, bb_im, abb_re, abb_im, c_re, c_im, ca_re, ca_im, cb):
    col = lambda re, im: jnp.concatenate([_slab_block_diag(re), _slab_block_diag(im)], axis=2)
    b_pair = jnp.concatenate([col(bb_re, bb_im), col(abb_re, abb_im)], axis=1)
    out = lambda w: _slab_block_diag(w.transpose(0, 2, 1))
    c_pair_re = jnp.concatenate([out(c_re), out(ca_re)], axis=2)
    c_pair_im = jnp.concatenate([out(c_im), out(ca_im)], axis=2)
    cb_blk = _slab_block_diag(cb.transpose(0, 2, 1))
    return tuple(w.astype(BF16) for w in (b_pair, c_pair_re, c_pair_im, cb_blk))


def _s5_mixer_kernel(x_ref, shift_ref, scale_ref, gate_ref, g_ref, bpair_ref, cre_ref, cim_ref, cb_ref,
                     a2r_ref, a2i_ref, d_ref, wglu_ref, bglu_ref, o_ref,
                     bu_re, bu_im, st_re, st_im, tail_scr, yf_scr, ys_scr, *, ts, n_slabs, slab_state):
    @pl.when(pl.program_id(0) == 0)
    def _():
        st_re[...] = jnp.zeros_like(st_re)
        st_im[...] = jnp.zeros_like(st_im)
        tail_scr[...] = jnp.zeros_like(tail_scr)

    pairs = ts // 2
    pair_rows = pairs * SUBLANES
    x = x_ref[...]
    d = x.shape[1]
    h = _modulate(_rms(x, g_ref[...]), shift_ref[...], scale_ref[...])
    h3 = h.reshape(pairs, 2 * SUBLANES, d)
    h_first = h3[:, :SUBLANES].reshape(pair_rows, d).astype(BF16)
    h_second = h3[:, SUBLANES:].reshape(pair_rows, d).astype(BF16)

    def project_in(k):
        lanes = slice(k * LANES, (k + 1) * LANES)
        ssl = slice(k * slab_state, (k + 1) * slab_state)
        bu = _dot(jnp.concatenate([h_second[:, lanes], h_first[:, lanes]], axis=1), bpair_ref[k])
        bu_re[:, ssl] = bu[:, :slab_state]
        bu_im[:, ssl] = bu[:, slab_state:]

    def recur(k):
        ssl = slice(k * slab_state, (k + 1) * slab_state)
        ar = jnp.broadcast_to(a2r_ref[:, ssl], (SUBLANES, slab_state))
        ai = jnp.broadcast_to(a2i_ref[:, ssl], (SUBLANES, slab_state))
        sr, si = st_re[:, ssl], st_im[:, ssl]
        for j in range(pairs):
            rows = slice(j * SUBLANES, (j + 1) * SUBLANES)
            sr, si = ar * sr - ai * si + bu_re[rows, ssl], ar * si + ai * sr + bu_im[rows, ssl]
            bu_re[rows, ssl] = sr
            bu_im[rows, ssl] = si
        st_re[:, ssl] = sr
        st_im[:, ssl] = si

    def project_out(k):
        ssl = slice(k * slab_state, (k + 1) * slab_state)
        lanes = slice(k * LANES, (k + 1) * LANES)
        z = _dot(bu_re[:, ssl].astype(BF16), cre_ref[k]) - _dot(bu_im[:, ssl].astype(BF16), cim_ref[k])
        ys_scr[:, lanes] = z[:, :LANES]
        carried = jnp.concatenate([tail_scr[:, lanes], z[:pair_rows - SUBLANES, LANES:]], axis=0)
        tail_scr[:, lanes] = z[pair_rows - SUBLANES:, LANES:]
        yf_scr[:, lanes] = carried + _dot(h_first[:, lanes], cb_ref[k])

    for k in range(n_slabs + 2):
        if k < n_slabs:
            project_in(k)
        if 0 <= k - 1 < n_slabs:
            recur(k - 1)
        if 0 <= k - 2 < n_slabs:
            project_out(k - 2)

    y = jnp.concatenate([yf_scr[...].reshape(pairs, SUBLANES, d), ys_scr[...].reshape(pairs, SUBLANES, d)],
                        axis=1).reshape(2 * pair_rows, d)
    y = y + d_ref[...] * h
    gl = jax.nn.gelu(y)
    z = _dot(gl.astype(BF16), wglu_ref[...]) + bglu_ref[...]
    mix = gl * jax.nn.sigmoid(z)
    o_ref[...] = _gated_add(x, gate_ref[...], mix)


def _s5_mixer(x_tb, shift, scale, gate, norm_g, slab_weights, a2_re, a2_im, d_skip, w_glu, b_glu, bsz):
    t, d = x_tb.shape
    ts = S5_TIME_TILE
    rows = ts * bsz
    pair_rows = rows // 2
    b_pair, c_pair_re, c_pair_im, cb_blk = slab_weights
    n_slabs, _, two_state = b_pair.shape
    slab_state = two_state // 2
    n_state = n_slabs * slab_state
    full = _resident
    row = lambda a: a.reshape(1, -1)
    args = (x_tb, shift, scale, gate, row(norm_g), b_pair, c_pair_re, c_pair_im, cb_blk, row(a2_re), row(a2_im),
            row(d_skip), w_glu, row(b_glu))
    return pl.pallas_call(
        functools.partial(_s5_mixer_kernel, ts=ts, n_slabs=n_slabs, slab_state=slab_state),
        grid=(t // rows,),
        in_specs=[pl.BlockSpec((rows, d), lambda i: (i, 0))] + [full(a) for a in args[1:]],
        out_specs=pl.BlockSpec((rows, d), lambda i: (i, 0)),
        out_shape=jax.ShapeDtypeStruct((t, d), F32),
        scratch_shapes=[pltpu.VMEM((pair_rows, n_state), F32), pltpu.VMEM((pair_rows, n_state), F32),
                        pltpu.VMEM((bsz, n_state), F32), pltpu.VMEM((bsz, n_state), F32),
                        pltpu.VMEM((bsz, d), F32), pltpu.VMEM((pair_rows, d), F32),
                        pltpu.VMEM((pair_rows, d), F32)],
        compiler_params=_params("arbitrary"),
        name="s5_mixer",
    )(*args)


def _ffn_kernel(*refs, n_chunks, with_attn):
    if with_attn:
        (x_ref, o_in_ref, wo_ref, gate1_ref, shift_ref, scale_ref, gate2_ref, g_ref,
         wg_ref, wu_ref, wd_ref, out_ref, h_scr, acc_scr) = refs
        x = x_ref[...] + gate1_ref[0] * _dot(o_in_ref[...], wo_ref[...])
        shift, scale, gate2 = shift_ref[0], scale_ref[0], gate2_ref[0]
    else:
        (x_ref, shift_ref, scale_ref, gate2_ref, g_ref,
         wg_ref, wu_ref, wd_ref, out_ref, h_scr, acc_scr) = refs
        x = x_ref[...]
        shift, scale, gate2 = shift_ref[...], scale_ref[...], gate2_ref[...]

    h_scr[...] = _modulate(_rms(x, g_ref[...]), shift, scale).astype(BF16)
    acc_scr[...] = jnp.zeros_like(acc_scr)

    hb = h_scr[...]
    for f in range(n_chunks):
        cols = slice(f * FFN_CHUNK, (f + 1) * FFN_CHUNK)
        gt = _dot(hb, wg_ref[:, cols])
        up = _dot(hb, wu_ref[:, cols])
        act = (gt * jax.nn.sigmoid(gt) * up).astype(BF16)
        acc_scr[...] += _dot(act, wd_ref[cols, :])
    out_ref[...] = _gated_add(x, gate2, acc_scr[...])


def _ffn(x, shift, scale, gate2, norm_g, wg, wu, wd, attn=None):
    t, d = x.shape
    n_chunks = wg.shape[1] // FFN_CHUNK
    tm = ROW_TILE
    full = _resident
    xspec = pl.BlockSpec((tm, d), lambda i: (i, 0))
    g2 = norm_g.reshape(1, d)
    if attn is None:
        args = (x, shift, scale, gate2, g2, wg, wu, wd)
        in_specs = [xspec] + [full(a) for a in args[1:]]
    else:
        o_in, w_o, gate1 = attn
        tiles_per_batch = (t // gate1.shape[0]) // tm
        per_b = pl.BlockSpec((1, 1, d), lambda i: (i // tiles_per_batch, 0, 0))
        args = (x, o_in, w_o, gate1, shift, scale, gate2, g2, wg, wu, wd)
        in_specs = [xspec, xspec, full(w_o), per_b, per_b, per_b, per_b, full(g2), full(wg), full(wu), full(wd)]
    return pl.pallas_call(
        functools.partial(_ffn_kernel, n_chunks=n_chunks, with_attn=attn is not None),
        grid=(t // tm,),
        in_specs=in_specs,
        out_specs=xspec,
        out_shape=jax.ShapeDtypeStruct((t, d), F32),
        scratch_shapes=[pltpu.VMEM((tm, d), BF16), pltpu.VMEM((tm, d), F32)],
        compiler_params=_params("parallel"),
        name="ffn_attn_out" if attn is not None else "ffn",
    )(*args)


def _head_tile(nope, rope_at_64, h):
    blk = nope[:, (h // 2) * LANES:(h // 2 + 1) * LANES]
    if h % 2:
        blk = pltpu.roll(blk, QK_NOPE_DIM, 1)
    return jnp.where(_lane_iota(blk.shape) < QK_NOPE_DIM, blk, rope_at_64)


def _kv_kernel(x_ref, shift_ref, scale_ref, g_ref, wa_ref, ga_ref, wkn_ref, wv_ref, gkn_ref,
               gkr_ref, gkrs_ref, sum_ref, spread_ref, ct_ref, st_ref, k_ref, v_ref):
    rank = ga_ref.shape[1]
    ones_col = (_lane_iota((1, LANES)) == V_HEAD_DIM).astype(F32)

    def block(rows):
        x = x_ref[rows, :]
        hb = _modulate(_rms(x, g_ref[...]), shift_ref[0], scale_ref[0]).astype(BF16)
        yield
        kva = _dot(hb, wa_ref[...])
        yield
        cb = _rms(kva[:, :rank], ga_ref[...]).astype(BF16)
        a = kva[:, rank:rank + LANES]
        asw = kva[:, rank + LANES:rank + 2 * LANES]
        rstd = lax.rsqrt(jnp.sum(a * a, axis=-1, keepdims=True) * (1.0 / QK_ROPE_DIM) + EPS)
        sgn = _rope_sign(a.shape)
        kr = rstd * (a * gkr_ref[...] * ct_ref[rows, :] + asw * gkrs_ref[...] * (st_ref[rows, :] * sgn))
        kr64 = pltpu.roll(kr, QK_NOPE_DIM, 1)
        yield
        kn = _dot(cb, wkn_ref[...])
        v = _dot(cb, wv_ref[...])
        yield
        kn = kn * _head_rstd(kn, sum_ref, spread_ref, QK_NOPE_DIM) * gkn_ref[...]
        yield
        for h in range(N_HEADS):
            k_ref[0, h, rows, :] = _head_tile(kn, kr64, h).astype(BF16)
            v_ref[0, h, rows, :] = (v[:, h * LANES:(h + 1) * LANES] + ones_col).astype(BF16)

    _run_staggered(block(rows) for rows in _row_blocks(x_ref.shape[0]))


def _shared_kv(x, bsz, k_shift, k_scale, kv_norm_g, w_kv_a, kv_a_norm_g, w_kv_b, k_nope_g, k_rope_g,
               head_sum, head_spread, ct, st):
    t, d = x.shape
    s_len = t // bsz
    tm = ROW_TILE
    tiles = s_len // tm
    rank = kv_a_norm_g.shape[0]
    half = QK_ROPE_DIM // 2
    pad = jnp.zeros((d, LANES - QK_ROPE_DIM), F32)
    w_rope = w_kv_a[:, rank:]
    w_rope_sw = jnp.concatenate([w_rope[:, half:], w_rope[:, :half]], axis=1)
    wa = jnp.concatenate([w_kv_a[:, :rank], w_rope, pad, w_rope_sw, pad], axis=1).astype(BF16)
    wkv = w_kv_b.reshape(rank, N_HEADS, QK_NOPE_DIM + V_HEAD_DIM)
    wkn = wkv[:, :, :QK_NOPE_DIM].reshape(rank, N_HEADS * QK_NOPE_DIM).astype(BF16)
    wv = jnp.pad(wkv[:, :, QK_NOPE_DIM:], ((0, 0), (0, 0), (0, HEAD_LANES - V_HEAD_DIM)))
    wv = wv.reshape(rank, N_HEADS * HEAD_LANES).astype(BF16)
    gkn = jnp.tile(k_nope_g, N_HEADS).reshape(1, -1)
    lane_pad = lambda g: jnp.pad(g, (0, LANES - g.shape[0])).reshape(1, LANES)
    gkr = lane_pad(k_rope_g)
    gkrs = lane_pad(jnp.concatenate([k_rope_g[half:], k_rope_g[:half]]))
    full = _resident
    xspec = pl.BlockSpec((tm, d), lambda b, i: (b * tiles + i, 0))
    tab = pl.BlockSpec((tm, LANES), lambda b, i: (b * tiles + i, 0))
    per_b = pl.BlockSpec((1, 1, d), lambda b, i: (b, 0, 0))
    hspec = pl.BlockSpec((1, N_HEADS, tm, HEAD_LANES), lambda b, i: (b, 0, i, 0))
    args = (x, k_shift, k_scale, kv_norm_g.reshape(1, d), wa, kv_a_norm_g.reshape(1, rank), wkn, wv,
            gkn, gkr, gkrs, head_sum, head_spread, ct, st)
    in_specs = [xspec, per_b, per_b] + [full(a) for a in args[3:13]] + [tab, tab]
    return pl.pallas_call(
        _kv_kernel,
        grid=(bsz, tiles),
        in_specs=in_specs,
        out_specs=[hspec, hspec],
        out_shape=[jax.ShapeDtypeStruct((bsz, N_HEADS, s_len, HEAD_LANES), BF16)] * 2,
        compiler_params=_params("parallel", "parallel"),
        name="mla_shared_kv",
    )(*args)


def _q_kernel(x_ref, shift_ref, scale_ref, g_ref, wdq_ref, gq_ref, wn_ref, wr_ref, wrs_ref,
              gn_ref, gr_ref, grs_ref, sum_ref, spread_ref, seg32_ref, ct_ref, st_ref, q_ref):
    heads_per_blk = LANES // QK_ROPE_DIM

    def block(rows):
        x = x_ref[rows, :]
        hb = _modulate(_rms(x, g_ref[...]), shift_ref[0], scale_ref[0]).astype(BF16)
        yield
        cq = _dot(hb, wdq_ref[...])
        yield
        cq = _rms(cq, gq_ref[...]).astype(BF16)
        yield
        qn = _dot(cq, wn_ref[...])
        a = _dot(cq, wr_ref[...])
        asw = _dot(cq, wrs_ref[...])
        yield
        qn = qn * _head_rstd(qn, sum_ref, spread_ref, QK_NOPE_DIM) * (gn_ref[...] * Q_SCALE)
        ssr = _dot((a * a).astype(BF16), seg32_ref[...])
        yield
        rstd = lax.rsqrt(ssr * (1.0 / QK_ROPE_DIM) + EPS) * Q_SCALE
        ct = ct_ref[rows, :]
        st = st_ref[rows, :] * _rope_sign(ct.shape)
        for blk in range(N_HEADS // heads_per_blk):
            sl = slice(blk * LANES, (blk + 1) * LANES)
            qr = rstd[:, sl] * (a[:, sl] * gr_ref[:, sl] * ct + asw[:, sl] * grs_ref[:, sl] * st)
            lane = _lane_iota(qr.shape)
            for j in range(heads_per_blk):
                h_idx = blk * heads_per_blk + j
                shift = (QK_NOPE_DIM - j * QK_ROPE_DIM) % LANES
                r = pltpu.roll(qr, shift, 1) if shift else qr
                r = jnp.where(lane < QK_NOPE_DIM + QK_ROPE_DIM, r, 0.0)
                q_ref[0, h_idx, rows, :] = _head_tile(qn, r, h_idx).astype(BF16)
            yield

    _run_staggered(block(rows) for rows in _row_blocks(x_ref.shape[0]))


def _queries(x, bsz, shift, scale, norm_g, w_dq, q_norm_g, w_uq, q_nope_g, q_rope_g, head_sum, head_spread,
             seg32, ct, st):
    t, d = x.shape
    s_len = t // bsz
    tm = ROW_TILE
    tiles = s_len // tm
    rank = w_dq.shape[1]
    half = QK_ROPE_DIM // 2
    wq = w_uq.reshape(rank, N_HEADS, QK_NOPE_DIM + QK_ROPE_DIM)
    wn = wq[:, :, :QK_NOPE_DIM].reshape(rank, -1).astype(BF16)
    wr = wq[:, :, QK_NOPE_DIM:].reshape(rank, -1).astype(BF16)
    wrs = jnp.concatenate([wq[:, :, QK_NOPE_DIM + half:], wq[:, :, QK_NOPE_DIM:QK_NOPE_DIM + half]], axis=2)
    wrs = wrs.reshape(rank, -1).astype(BF16)
    gn = jnp.tile(q_nope_g, N_HEADS).reshape(1, -1)
    gr = jnp.tile(q_rope_g, N_HEADS).reshape(1, -1)
    grs = jnp.tile(jnp.concatenate([q_rope_g[half:], q_rope_g[:half]]), N_HEADS).reshape(1, -1)
    full = _resident
    xspec = pl.BlockSpec((tm, d), lambda b, i: (b * tiles + i, 0))
    tab = pl.BlockSpec((tm, LANES), lambda b, i: (b * tiles + i, 0))
    per_b = pl.BlockSpec((1, 1, d), lambda b, i: (b, 0, 0))
    hspec = pl.BlockSpec((1, N_HEADS, tm, HEAD_LANES), lambda b, i: (b, 0, i, 0))
    args = (x, shift, scale, norm_g.reshape(1, d), w_dq.astype(BF16), q_norm_g.reshape(1, rank), wn, wr, wrs,
            gn, gr, grs, head_sum, head_spread, seg32, ct, st)
    in_specs = [xspec, per_b, per_b] + [full(a) for a in args[3:15]] + [tab, tab]
    return pl.pallas_call(
        _q_kernel,
        grid=(bsz, tiles),
        in_specs=in_specs,
        out_specs=hspec,
        out_shape=jax.ShapeDtypeStruct((bsz, N_HEADS, s_len, HEAD_LANES), BF16),
        compiler_params=_params("parallel", "parallel"),
        name="mla_queries",
    )(*args)


def _chunk_offsets(tile):
    row_chunk = lax.broadcasted_iota(jnp.int32, (tile, tile), 0) // CHUNK
    col_chunk = lax.broadcasted_iota(jnp.int32, (tile, tile), 1) // CHUNK
    return row_chunk - col_chunk


def _merge_head_pair(accs):
    outs = [acc / acc[:, V_HEAD_DIM:V_HEAD_DIM + 1] for acc in accs]
    lane = _lane_iota(outs[0].shape)
    tiles = [jnp.where(lane < V_HEAD_DIM, outs[i], pltpu.roll(outs[i + 1], V_HEAD_DIM, 1))
             for i in range(0, len(outs), 2)]
    return tiles[0] if len(tiles) == 1 else jnp.concatenate(tiles, axis=1)


def _attn_online_kernel(q_ref, k_ref, v_ref, o_ref, m_scr, acc_scr, *, tile):
    i = pl.program_id(2)
    diag_mask = _chunk_offsets(tile) >= 0
    qs = [q_ref[0, hh] for hh in range(HEADS_PER_STEP)]

    def tile_update(off, diagonal):
        for hh in range(HEADS_PER_STEP):
            s = lax.dot_general(qs[hh], k_ref[0, hh, pl.ds(off, tile), :], _NT_DIMS,
                                preferred_element_type=F32)
            if diagonal:
                s = jnp.where(diag_mask, s, MASK_VALUE)
            row_max = jnp.max(s, axis=-1, keepdims=True)
            m = row_max if diagonal else m_scr[hh]
            m_new = row_max if diagonal else jnp.maximum(m, row_max)
            pv = _dot(jnp.exp2(s - m_new).astype(BF16), v_ref[0, hh, pl.ds(off, tile), :])
            acc_scr[hh] = pv if diagonal else acc_scr[hh] * jnp.exp2(m - m_new) + pv
            m_scr[hh] = m_new

    tile_update(pl.multiple_of(i * tile, tile), True)

    def below_diagonal(j, carry):
        tile_update(pl.multiple_of(j * tile, tile), False)
        return carry

    lax.fori_loop(0, i, below_diagonal, 0)
    o_ref[0] = _merge_head_pair([acc_scr[hh] for hh in range(HEADS_PER_STEP)]).astype(o_ref.dtype)


def _attn_bounded_kernel(q_ref, k_ref, v_ref, o_ref, *, tile, n_tiles):
    g = pl.program_id(2)
    lo, hi = g, n_tiles - 1 - g
    diag_mask = _chunk_offsets(tile) >= 0
    zero = jnp.zeros((tile, HEAD_LANES), F32)
    acc_lo = [zero] * HEADS_PER_STEP
    acc_hi = [zero] * HEADS_PER_STEP

    def rows(ref, hh, idx):
        return ref[0, hh, pl.ds(pl.multiple_of(idx * tile, tile), tile), :]

    def weighted_values(q, k, v, mask):
        s = lax.dot_general(q, k, _NT_DIMS, preferred_element_type=F32)
        if mask is not None:
            s = jnp.where(mask, s, MASK_VALUE)
        return _dot(jnp.exp2(s).astype(BF16), v)

    def diagonal(idx, accs):
        for hh in range(HEADS_PER_STEP):
            accs[hh] = accs[hh] + weighted_values(rows(q_ref, hh, idx), rows(k_ref, hh, idx),
                                                  rows(v_ref, hh, idx), diag_mask)

    diagonal(lo, acc_lo)
    diagonal(hi, acc_hi)
    max_lo = n_tiles // 2 - 1
    for slot in range(1, n_tiles):
        owner_known = slot > max_lo
        if owner_known:
            q_idx, k_idx = hi, slot - 1 - lo
        else:
            is_lo = slot <= lo
            q_idx = jnp.where(is_lo, lo, hi)
            k_idx = jnp.where(is_lo, slot - 1, slot - 1 - lo)
        for hh in range(HEADS_PER_STEP):
            pv = weighted_values(rows(q_ref, hh, q_idx), rows(k_ref, hh, k_idx), rows(v_ref, hh, k_idx), None)
            if owner_known:
                acc_hi[hh] = acc_hi[hh] + pv
            else:
                acc_lo[hh] = acc_lo[hh] + jnp.where(is_lo, pv, 0.0)
                acc_hi[hh] = acc_hi[hh] + jnp.where(is_lo, 0.0, pv)
    o_ref[0, pl.ds(pl.multiple_of(lo * tile, tile), tile), :] = _merge_head_pair(acc_lo).astype(o_ref.dtype)
    o_ref[0, pl.ds(pl.multiple_of(hi * tile, tile), tile), :] = _merge_head_pair(acc_hi).astype(o_ref.dtype)


def _score_bound(q_nope_g, q_rope_g, k_nope_g, k_rope_g):
    def sq_len(g_nope, g_rope):
        return QK_NOPE_DIM * jnp.max(g_nope * g_nope) + QK_ROPE_DIM * jnp.max(g_rope * g_rope)
    return Q_SCALE * jnp.sqrt(sq_len(q_nope_g, q_rope_g) * sq_len(k_nope_g, k_rope_g)) * BOUND_SLACK


def _attention(q, k, v, bound):
    return lax.cond(bound <= MAX_UNSHIFTED_BOUND, _attention_bounded, _attention_online, q, k, v)


def _attention_online(q, k, v):
    bsz, nh, s_len, hl = q.shape
    tile = ATTN_TILE
    hp = HEADS_PER_STEP
    whole = pl.BlockSpec((1, hp, s_len, hl), lambda b, h, i: (b, h, 0, 0))
    return pl.pallas_call(
        functools.partial(_attn_online_kernel, tile=tile),
        grid=(bsz, nh // hp, s_len // tile),
        in_specs=[pl.BlockSpec((1, hp, tile, hl), lambda b, h, i: (b, h, i, 0)), whole, whole],
        out_specs=pl.BlockSpec((1, tile, hp * V_HEAD_DIM), lambda b, h, i: (b, i, h)),
        out_shape=jax.ShapeDtypeStruct((bsz, s_len, nh * V_HEAD_DIM), BF16),
        scratch_shapes=[pltpu.VMEM((hp, tile, 1), F32), pltpu.VMEM((hp, tile, hl), F32)],
        compiler_params=_params("parallel", "parallel", "arbitrary"),
        name="mla_attention_online",
    )(q, k, v)


def _attention_bounded(q, k, v):
    bsz, nh, s_len, hl = q.shape
    tile = ATTN_TILE
    hp = HEADS_PER_STEP
    n_tiles = s_len // tile
    whole = pl.BlockSpec((1, hp, s_len, hl), lambda b, h, g: (b, h, 0, 0))
    return pl.pallas_call(
        functools.partial(_attn_bounded_kernel, tile=tile, n_tiles=n_tiles),
        grid=(bsz, nh // hp, n_tiles // 2),
        in_specs=[whole, whole, whole],
        out_specs=pl.BlockSpec((1, s_len, hp * V_HEAD_DIM), lambda b, h, g: (b, 0, h)),
        out_shape=jax.ShapeDtypeStruct((bsz, s_len, nh * V_HEAD_DIM), BF16),
        compiler_params=_params("parallel", "parallel", "arbitrary"),
        name="mla_attention_bounded",
    )(q, k, v)


def _seg_ones(n, seg):
    idx = jnp.arange(n) // seg
    return (idx[:, None] == idx[None, :]).astype(BF16)


def _ffn_weights(w_gate, w_up, w_down):
    return w_gate.astype(BF16), w_up.astype(BF16), w_down.astype(BF16)


def kernel(x, c, positions, ada_w, ada_b, norm1_g, norm2_g, ffn_w_gate, ffn_w_up, ffn_w_down, s5_lam_re, s5_lam_im, s5_log_dt, s5_b_re, s5_b_im, s5_c_re, s5_c_im, s5_d, s5_w_glu, s5_b_glu, kv_ada_w, kv_ada_b, kv_norm_g, w_kv_a, kv_a_norm_g, w_kv_b, k_nope_norm_g, k_rope_norm_g, mla_w_dq, mla_q_norm_g, mla_w_uq, mla_q_nope_norm_g, mla_q_rope_norm_g, mla_w_o):
    bsz, s_len, d = x.shape
    depth = ada_w.shape[0]
    n_a = s5_lam_re.shape[0]
    t = bsz * s_len

    mods = _mods(c, ada_w, ada_b).reshape(depth, bsz, 6, d)
    kv_mods = _mods(c, kv_ada_w[None], kv_ada_b[None]).reshape(bsz, 2, d)
    cos_t, sin_t = _rope_tables(positions)

    a2_re, a2_im, *s5_mats = _s5_discretise(s5_lam_re, s5_lam_im, s5_log_dt, s5_b_re, s5_b_im, s5_c_re, s5_c_im)

    xt = x.transpose(1, 0, 2).reshape(t, d)
    for l in range(n_a):
        m = mods[l]
        bb_re, bb_im, abb_re, abb_im, ca_re, ca_im, cb = (w[l] for w in s5_mats)
        slab_weights = _s5_slab_weights(bb_re, bb_im, abb_re, abb_im, s5_c_re[l], s5_c_im[l], ca_re, ca_im, cb)
        xt = _s5_mixer(xt, m[:, 0], m[:, 1], m[:, 2], norm1_g[l], slab_weights, a2_re[l], a2_im[l], s5_d[l],
                       s5_w_glu[l].astype(BF16), s5_b_glu[l], bsz)
        wg, wu, wd = _ffn_weights(ffn_w_gate[l], ffn_w_up[l], ffn_w_down[l])
        xt = _ffn(xt, m[:, 3], m[:, 4], m[:, 5], norm2_g[l], wg, wu, wd)
    xb = xt.reshape(s_len, bsz, d).transpose(1, 0, 2).reshape(t, d)

    head_of_lane = jnp.arange(N_HEADS * QK_NOPE_DIM) // QK_NOPE_DIM
    head_sum = (head_of_lane[:, None] == jnp.arange(LANES)[None, :]).astype(BF16)
    head_spread = jnp.concatenate([head_sum.T, head_sum.T], axis=0)
    seg32 = _seg_ones(N_HEADS * QK_ROPE_DIM, QK_ROPE_DIM)
    per_b = lambda v: v.reshape(bsz, 1, d)
    k_all, v_all = _shared_kv(xb, bsz, per_b(kv_mods[:, 0]), per_b(kv_mods[:, 1]), kv_norm_g, w_kv_a,
                              kv_a_norm_g, w_kv_b, k_nope_norm_g, k_rope_norm_g, head_sum, head_spread,
                              cos_t, sin_t)
    for l in range(n_a, depth):
        j = l - n_a
        m = mods[l]
        bound = _score_bound(mla_q_nope_norm_g[j], mla_q_rope_norm_g[j], k_nope_norm_g, k_rope_norm_g)
        q = _queries(xb, bsz, per_b(m[:, 0]), per_b(m[:, 1]), norm1_g[l], mla_w_dq[j], mla_q_norm_g[j],
                     mla_w_uq[j], mla_q_nope_norm_g[j], mla_q_rope_norm_g[j], head_sum, head_spread, seg32,
                     cos_t, sin_t)
        o = _attention(q, k_all, v_all, bound).reshape(t, N_HEADS * V_HEAD_DIM)
        wg, wu, wd = _ffn_weights(ffn_w_gate[l], ffn_w_up[l], ffn_w_down[l])
        xb = _ffn(xb, per_b(m[:, 3]), per_b(m[:, 4]), per_b(m[:, 5]), norm2_g[l], wg, wu, wd,
                  attn=(o, mla_w_o[j].astype(BF16), per_b(m[:, 2])))
    return xb.reshape(bsz, s_len, d)
```

```python
import functools
import math

import jax
import jax.numpy as jnp
from jax import lax
from jax.experimental import pallas as pl
from jax.experimental.pallas import tpu as pltpu

F32 = jnp.float32
BF16 = jnp.bfloat16

CHUNK = 64
SSM_GROUP = 16
SSM_STATE = 64
N_HEADS = 16
QK_NOPE_DIM = 64
QK_ROPE_DIM = 32
V_HEAD_DIM = 64
ROPE_THETA = 10000.0
ATTN_SCALE = 1.0 / math.sqrt(QK_NOPE_DIM + QK_ROPE_DIM)
Q_SCALE = ATTN_SCALE * math.log2(math.e)
EPS = 1e-6
MASK_VALUE = -1e30
BOUND_SLACK = 1.01
MAX_UNSHIFTED_BOUND = 40.0

LANES = 128
SUBLANES = 8
HEAD_LANES = 128
SLAB_GROUPS = LANES // SSM_GROUP
VMEM_LIMIT = 56 * 1024 * 1024

ROW_TILE = 512
ROW_BLOCKS = 2
S5_TIME_TILE = 64
RELAYOUT_PITCH_PAD = 8
FFN_CHUNK = 256
ATTN_TILE = 512
HEADS_PER_STEP = 4


def _params(*sem):
    return pltpu.CompilerParams(dimension_semantics=sem, vmem_limit_bytes=VMEM_LIMIT)


def _rms(x, g):
    return x * lax.rsqrt(jnp.mean(x * x, axis=-1, keepdims=True) + EPS) * g


def _rowwise(fn, a, *mods):
    r = mods[0].shape[0]
    if r == 1:
        return fn(a, *mods)
    rows, d = a.shape
    out = fn(a.reshape(rows // r, r, d), *[m[None] for m in mods])
    return out.reshape(rows, d)


def _modulate(h, shift, scale):
    return _rowwise(lambda a, sh, sc: a * (1.0 + sc) + sh, h, shift, scale)


def _gated_add(x, gate, upd):
    return x + _rowwise(lambda a, g: a * g, upd, gate)


def _dot(a, b):
    return jnp.dot(a, b, preferred_element_type=F32)


_NT_DIMS = (((1,), (1,)), ((), ()))


def _resident(a):
    zeros = (0,) * a.ndim
    return pl.BlockSpec(a.shape, lambda *_: zeros, pipeline_mode=pl.Buffered(1))


def _lane_iota(shape):
    return lax.broadcasted_iota(jnp.int32, shape, len(shape) - 1)


def _rows_bt_to_tb(x_ref, scr):
    bsz, ts, d = x_ref.shape
    pitch = scr.shape[1] // bsz
    cols = []
    for s in range(d // LANES):
        lanes = slice(s * LANES, (s + 1) * LANES)
        for b in range(bsz):
            scr[s, b * pitch:b * pitch + ts, :] = x_ref[b, :, lanes]
        cols.append(jnp.concatenate([scr[s, pl.ds(t, bsz, stride=pitch), :] for t in range(ts)], axis=0))
    return jnp.concatenate(cols, axis=1)


def _rows_tb_to_bt(val, scr, out_ref):
    bsz, ts, d = out_ref.shape
    for s in range(d // LANES):
        lanes = slice(s * LANES, (s + 1) * LANES)
        scr[s] = val[:, lanes]
        for b in range(bsz):
            out_ref[b, :, lanes] = scr[s, pl.ds(b, ts, stride=bsz), :]


def _row_blocks(n_rows):
    step = n_rows // ROW_BLOCKS
    return [slice(i * step, (i + 1) * step) for i in range(ROW_BLOCKS)]


def _run_staggered(blocks):
    waiting, live = list(blocks), []
    while waiting or live:
        if waiting:
            live.append(waiting.pop(0))
        for gen in list(live):
            if next(gen, StopIteration) is StopIteration:
                live.remove(gen)


def _head_rstd(x, sum_ref, spread_ref, dim):
    ss = _dot((x * x).astype(BF16), sum_ref[...])
    rstd = lax.rsqrt(ss * (1.0 / dim) + EPS)
    hi = rstd.astype(BF16)
    lo = (rstd - hi.astype(F32)).astype(BF16)
    return _dot(jnp.concatenate([hi, lo], axis=1), spread_ref[...])


def _mods_kernel(c_ref, w_ref, b_ref, o_ref):
    c = c_ref[...]
    ca = c * jax.nn.sigmoid(c)
    o_ref[0] = _dot(ca.astype(BF16), w_ref[0].astype(BF16)) + b_ref[0]


def _mods(c, w, b, tn=2048):
    nl, d, n = w.shape
    bsz = c.shape[0]
    return pl.pallas_call(
        _mods_kernel,
        grid=(nl, n // tn),
        in_specs=[pl.BlockSpec((bsz, d), lambda l, j: (0, 0)),
                  pl.BlockSpec((1, d, tn), lambda l, j: (l, 0, j)),
                  pl.BlockSpec((1, 1, tn), lambda l, j: (l, 0, j))],
        out_specs=pl.BlockSpec((1, bsz, tn), lambda l, j: (l, 0, j)),
        out_shape=jax.ShapeDtypeStruct((nl, bsz, n), F32),
        compiler_params=_params("parallel", "parallel"),
        name="adaln_mods",
    )(c, w, b.reshape(nl, 1, n))


def _rope_kernel(pos_ref, inv_ref, cos_ref, sin_ref):
    ang = inv_ref[...] * pos_ref[...]
    cos_ref[...] = jnp.cos(ang)
    sin_ref[...] = jnp.sin(ang)


def _rope_tables(positions, tn=4096):
    t = positions.size
    half = QK_ROPE_DIM // 2
    inv = 1.0 / (ROPE_THETA ** (jnp.arange(0, QK_ROPE_DIM, 2, dtype=F32) / QK_ROPE_DIM))
    pos = positions.astype(F32).reshape(1, t)
    cos, sin = pl.pallas_call(
        _rope_kernel,
        grid=(t // tn,),
        in_specs=[pl.BlockSpec((1, tn), lambda i: (0, i)),
                  pl.BlockSpec((half, 1), lambda i: (0, 0))],
        out_specs=[pl.BlockSpec((half, tn), lambda i: (0, i))] * 2,
        out_shape=[jax.ShapeDtypeStruct((half, t), F32)] * 2,
        compiler_params=_params("parallel"),
        name="rope_tables",
    )(pos, inv.reshape(half, 1))
    reps = LANES // half
    return jnp.tile(cos.T, (1, reps)), jnp.tile(sin.T, (1, reps))


def _rope_sign(shape):
    lane = _lane_iota(shape)
    return jnp.where((lane % QK_ROPE_DIM) < (QK_ROPE_DIM // 2), -1.0, 1.0).astype(F32)


def _cmul(ar, ai, br, bi):
    return ar * br - ai * bi, ar * bi + ai * br


def _s5_disc_kernel(lr_ref, li_ref, ldt_ref, br_ref, bi_ref, cr_ref, ci_ref,
                    a2r_ref, a2i_ref, bbr_ref, bbi_ref, abbr_ref, abbi_ref, car_ref, cai_ref, cb_ref):
    lr = lr_ref[0]
    li = li_ref[0]
    dt = jnp.exp(ldt_ref[0])
    mag = jnp.exp(lr * dt)
    a_re = mag * jnp.cos(li * dt)
    a_im = mag * jnp.sin(li * dt)
    den = lr * lr + li * li
    nr = a_re - 1.0
    ni = a_im
    f_re = (nr * lr + ni * li) / den
    f_im = (ni * lr - nr * li) / den
    bb_re, bb_im = _cmul(f_re, f_im, br_ref[0], bi_ref[0])
    cr = cr_ref[0]
    ci = ci_ref[0]
    a2r_ref[0], a2i_ref[0] = _cmul(a_re, a_im, a_re, a_im)
    bbr_ref[0] = bb_re
    bbi_ref[0] = bb_im
    abbr_ref[0], abbi_ref[0] = _cmul(a_re, a_im, bb_re, bb_im)
    car_ref[0], cai_ref[0] = _cmul(cr, ci, a_re, a_im)
    group_dot = functools.partial(jnp.einsum, "gpn,gqn->gpq", preferred_element_type=F32,
                                  precision=lax.Precision.HIGHEST)
    cb_ref[0] = group_dot(cr, bb_re) - group_dot(ci, bb_im)


def _s5_discretise(lam_re, lam_im, log_dt, b_re, b_im, c_re, c_im):
    na, g, n = lam_re.shape
    p = b_re.shape[-1]
    vec = pl.BlockSpec((1, g, 1, n), lambda l: (l, 0, 0, 0))
    mat = pl.BlockSpec((1, g, p, n), lambda l: (l, 0, 0, 0))
    vec_shape = jax.ShapeDtypeStruct((na, g, 1, n), F32)
    mat_shape = jax.ShapeDtypeStruct((na, g, p, n), F32)
    return pl.pallas_call(
        _s5_disc_kernel,
        grid=(na,),
        in_specs=[vec, vec, pl.BlockSpec((1, g, 1, 1), lambda l: (l, 0, 0, 0)), mat, mat, mat, mat],
        out_specs=[vec, vec] + [mat] * 6 + [pl.BlockSpec((1, g, p, p), lambda l: (l, 0, 0, 0))],
        out_shape=[vec_shape] * 2 + [mat_shape] * 6 + [jax.ShapeDtypeStruct((na, g, p, p), F32)],
        compiler_params=_params("parallel"),
        name="s5_discretise",
    )(lam_re.reshape(na, g, 1, n), lam_im.reshape(na, g, 1, n), log_dt.reshape(na, g, 1, 1),
      b_re.transpose(0, 1, 3, 2), b_im.transpose(0, 1, 3, 2), c_re, c_im)


def _slab_block_diag(w_gab):
    g, a, b = w_gab.shape
    ns = g // SLAB_GROUPS
    w = w_gab.reshape(ns, SLAB_GROUPS, a, b)
    eye = jnp.eye(SLAB_GROUPS, dtype=w.dtype)
    return jnp.einsum("kgab,gh->kgahb", w, eye).reshape(ns, SLAB_GROUPS * a, SLAB_GROUPS * b)


def _s5_slab_weights(bb_re, bb_im, abb_re, abb_im, c_re, c_im, ca_re, ca_im, cb):
    col = lambda re, im: jnp.concatenate([_slab_block_diag(re), _slab_block_diag(im)], axis=2)
    b_pair = jnp.concatenate([col(bb_re, bb_im), col(abb_re, abb_im)], axis=1)
    out = lambda w: _slab_block_diag(w.transpose(0, 2, 1))
    c_pair_re = jnp.concatenate([out(c_re), out(ca_re)], axis=2)
    c_pair_im = jnp.concatenate([out(c_im), out(ca_im)], axis=2)
    cb_blk = _slab_block_diag(cb.transpose(0, 2, 1))
    return tuple(w.astype(BF16) for w in (b_pair, c_pair_re, c_pair_im, cb_blk))


def _s5_mixer_kernel(x_ref, shift_ref, scale_ref, gate_ref, g_ref, bpair_ref, cre_ref, cim_ref, cb_ref,
                     a2r_ref, a2i_ref, d_ref, wglu_ref, bglu_ref, o_ref,
                     bu_re, bu_im, st_re, st_im, tail_scr, yf_scr, ys_scr, *maybe_relayout_scr,
                     ts, n_slabs, slab_state):
    @pl.when(pl.program_id(0) == 0)
    def _():
        st_re[...] = jnp.zeros_like(st_re)
        st_im[...] = jnp.zeros_like(st_im)
        tail_scr[...] = jnp.zeros_like(tail_scr)

    pairs = ts // 2
    pair_rows = pairs * SUBLANES
    x = _rows_bt_to_tb(x_ref, *maybe_relayout_scr) if maybe_relayout_scr else x_ref[...]
    d = x.shape[1]
    h = _modulate(_rms(x, g_ref[...]), shift_ref[...], scale_ref[...])
    h3 = h.reshape(pairs, 2 * SUBLANES, d)
    h_first = h3[:, :SUBLANES].reshape(pair_rows, d).astype(BF16)
    h_second = h3[:, SUBLANES:].reshape(pair_rows, d).astype(BF16)

    def project_in(k):
        lanes = slice(k * LANES, (k + 1) * LANES)
        ssl = slice(k * slab_state, (k + 1) * slab_state)
        bu = _dot(jnp.concatenate([h_second[:, lanes], h_first[:, lanes]], axis=1), bpair_ref[k])
        bu_re[:, ssl] = bu[:, :slab_state]
        bu_im[:, ssl] = bu[:, slab_state:]

    def recur(k):
        ssl = slice(k * slab_state, (k + 1) * slab_state)
        ar = jnp.broadcast_to(a2r_ref[:, ssl], (SUBLANES, slab_state))
        ai = jnp.broadcast_to(a2i_ref[:, ssl], (SUBLANES, slab_state))
        sr, si = st_re[:, ssl], st_im[:, ssl]
        for j in range(pairs):
            rows = slice(j * SUBLANES, (j + 1) * SUBLANES)
            sr, si = ar * sr - ai * si + bu_re[rows, ssl], ar * si + ai * sr + bu_im[rows, ssl]
            bu_re[rows, ssl] = sr
            bu_im[rows, ssl] = si
        st_re[:, ssl] = sr
        st_im[:, ssl] = si

    def project_out(k):
        ssl = slice(k * slab_state, (k + 1) * slab_state)
        lanes = slice(k * LANES, (k + 1) * LANES)
        z = _dot(bu_re[:, ssl].astype(BF16), cre_ref[k]) - _dot(bu_im[:, ssl].astype(BF16), cim_ref[k])
        ys_scr[:, lanes] = z[:, :LANES]
        carried = jnp.concatenate([tail_scr[:, lanes], z[:pair_rows - SUBLANES, LANES:]], axis=0)
        tail_scr[:, lanes] = z[pair_rows - SUBLANES:, LANES:]
        yf_scr[:, lanes] = carried + _dot(h_first[:, lanes], cb_ref[k])

    for k in range(n_slabs + 2):
        if k < n_slabs:
            project_in(k)
        if 0 <= k - 1 < n_slabs:
            recur(k - 1)
        if 0 <= k - 2 < n_slabs:
            project_out(k - 2)

    y = jnp.concatenate([yf_scr[...].reshape(pairs, SUBLANES, d), ys_scr[...].reshape(pairs, SUBLANES, d)],
                        axis=1).reshape(2 * pair_rows, d)
    y = y + d_ref[...] * h
    gl = jax.nn.gelu(y)
    z = _dot(gl.astype(BF16), wglu_ref[...]) + bglu_ref[...]
    mix = gl * jax.nn.sigmoid(z)
    o_ref[...] = _gated_add(x, gate_ref[...], mix)


def _s5_mixer(x, shift, scale, gate, norm_g, slab_weights, a2_re, a2_im, d_skip, w_glu, b_glu, bsz):
    assert bsz == SUBLANES
    ts = S5_TIME_TILE
    rows = ts * bsz
    pair_rows = rows // 2
    if x.ndim == 3:
        d = x.shape[2]
        t = x.shape[0] * x.shape[1]
        x_spec = pl.BlockSpec((bsz, ts, d), lambda i: (0, i, 0))
        relayout_scr = [pltpu.VMEM((d // LANES, bsz * (ts + RELAYOUT_PITCH_PAD), LANES), F32)]
    else:
        t, d = x.shape
        x_spec = pl.BlockSpec((rows, d), lambda i: (i, 0))
        relayout_scr = []
    b_pair, c_pair_re, c_pair_im, cb_blk = slab_weights
    n_slabs, _, two_state = b_pair.shape
    slab_state = two_state // 2
    n_state = n_slabs * slab_state
    full = _resident
    row = lambda a: a.reshape(1, -1)
    args = (x, shift, scale, gate, row(norm_g), b_pair, c_pair_re, c_pair_im, cb_blk, row(a2_re), row(a2_im),
            row(d_skip), w_glu, row(b_glu))
    return pl.pallas_call(
        functools.partial(_s5_mixer_kernel, ts=ts, n_slabs=n_slabs, slab_state=slab_state),
        grid=(t // rows,),
        in_specs=[x_spec] + [full(a) for a in args[1:]],
        out_specs=pl.BlockSpec((rows, d), lambda i: (i, 0)),
        out_shape=jax.ShapeDtypeStruct((t, d), F32),
        scratch_shapes=[pltpu.VMEM((pair_rows, n_state), F32), pltpu.VMEM((pair_rows, n_state), F32),
                        pltpu.VMEM((bsz, n_state), F32), pltpu.VMEM((bsz, n_state), F32),
                        pltpu.VMEM((bsz, d), F32), pltpu.VMEM((pair_rows, d), F32),
                        pltpu.VMEM((pair_rows, d), F32)] + relayout_scr,
        compiler_params=_params("arbitrary"),
        name="s5_mixer",
    )(*args)


def _ffn_kernel(*refs, n_chunks, with_attn, out_batch_major):
    if out_batch_major:
        *refs, relayout_scr = refs
    if with_attn:
        (x_ref, o_in_ref, wo_ref, gate1_ref, shift_ref, scale_ref, gate2_ref, g_ref,
         wg_ref, wu_ref, wd_ref, out_ref, h_scr, acc_scr) = refs
        x = x_ref[...] + gate1_ref[0] * _dot(o_in_ref[...], wo_ref[...])
        shift, scale, gate2 = shift_ref[0], scale_ref[0], gate2_ref[0]
    else:
        (x_ref, shift_ref, scale_ref, gate2_ref, g_ref,
         wg_ref, wu_ref, wd_ref, out_ref, h_scr, acc_scr) = refs
        x = x_ref[...]
        shift, scale, gate2 = shift_ref[...], scale_ref[...], gate2_ref[...]

    h_scr[...] = _modulate(_rms(x, g_ref[...]), shift, scale).astype(BF16)
    acc_scr[...] = jnp.zeros_like(acc_scr)

    hb = h_scr[...]
    for f in range(n_chunks):
        cols = slice(f * FFN_CHUNK, (f + 1) * FFN_CHUNK)
        gt = _dot(hb, wg_ref[:, cols])
        up = _dot(hb, wu_ref[:, cols])
        act = (gt * jax.nn.sigmoid(gt) * up).astype(BF16)
        acc_scr[...] += _dot(act, wd_ref[cols, :])
    out = _gated_add(x, gate2, acc_scr[...])
    if out_batch_major:
        _rows_tb_to_bt(out, relayout_scr, out_ref)
    else:
        out_ref[...] = out


def _ffn(x, shift, scale, gate2, norm_g, wg, wu, wd, attn=None, out_batch_major=False):
    t, d = x.shape
    n_chunks = wg.shape[1] // FFN_CHUNK
    tm = ROW_TILE
    full = _resident
    xspec = pl.BlockSpec((tm, d), lambda i: (i, 0))
    g2 = norm_g.reshape(1, d)
    if attn is None:
        args = (x, shift, scale, gate2, g2, wg, wu, wd)
        in_specs = [xspec] + [full(a) for a in args[1:]]
    else:
        o_in, w_o, gate1 = attn
        tiles_per_batch = (t // gate1.shape[0]) // tm
        per_b = pl.BlockSpec((1, 1, d), lambda i: (i // tiles_per_batch, 0, 0))
        args = (x, o_in, w_o, gate1, shift, scale, gate2, g2, wg, wu, wd)
        in_specs = [xspec, xspec, full(w_o), per_b, per_b, per_b, per_b, full(g2), full(wg), full(wu), full(wd)]
    scratch = [pltpu.VMEM((tm, d), BF16), pltpu.VMEM((tm, d), F32)]
    out_spec, out_shape = xspec, jax.ShapeDtypeStruct((t, d), F32)
    if out_batch_major:
        bsz = shift.shape[0]
        out_spec = pl.BlockSpec((bsz, tm // bsz, d), lambda i: (0, i, 0))
        out_shape = jax.ShapeDtypeStruct((bsz, t // bsz, d), F32)
        scratch.append(pltpu.VMEM((d // LANES, tm, LANES), F32))
    return pl.pallas_call(
        functools.partial(_ffn_kernel, n_chunks=n_chunks, with_attn=attn is not None,
                          out_batch_major=out_batch_major),
        grid=(t // tm,),
        in_specs=in_specs,
        out_specs=out_spec,
        out_shape=out_shape,
        scratch_shapes=scratch,
        compiler_params=_params("parallel"),
        name="ffn_attn_out" if attn is not None else "ffn",
    )(*args)


def _head_tile(nope, rope_at_64, h):
    blk = nope[:, (h // 2) * LANES:(h // 2 + 1) * LANES]
    if h % 2:
        blk = pltpu.roll(blk, QK_NOPE_DIM, 1)
    return jnp.where(_lane_iota(blk.shape) < QK_NOPE_DIM, blk, rope_at_64)


def _kv_kernel(x_ref, shift_ref, scale_ref, g_ref, wa_ref, ga_ref, wkn_ref, wv_ref, gkn_ref,
               gkr_ref, gkrs_ref, sum_ref, spread_ref, ct_ref, st_ref, k_ref, v_ref):
    rank = ga_ref.shape[1]
    ones_col = (_lane_iota((1, LANES)) == V_HEAD_DIM).astype(F32)

    def block(rows):
        x = x_ref[rows, :]
        hb = _modulate(_rms(x, g_ref[...]), shift_ref[0], scale_ref[0]).astype(BF16)
        yield
        kva = _dot(hb, wa_ref[...])
        yield
        cb = _rms(kva[:, :rank], ga_ref[...]).astype(BF16)
        a = kva[:, rank:rank + LANES]
        asw = kva[:, rank + LANES:rank + 2 * LANES]
        rstd = lax.rsqrt(jnp.sum(a * a, axis=-1, keepdims=True) * (1.0 / QK_ROPE_DIM) + EPS)
        sgn = _rope_sign(a.shape)
        kr = rstd * (a * gkr_ref[...] * ct_ref[rows, :] + asw * gkrs_ref[...] * (st_ref[rows, :] * sgn))
        kr64 = pltpu.roll(kr, QK_NOPE_DIM, 1)
        yield
        kn = _dot(cb, wkn_ref[...])
        v = _dot(cb, wv_ref[...])
        yield
        kn = kn * _head_rstd(kn, sum_ref, spread_ref, QK_NOPE_DIM) * gkn_ref[...]
        yield
        for h in range(N_HEADS):
            k_ref[0, h, rows, :] = _head_tile(kn, kr64, h).astype(BF16)
            v_ref[0, h, rows, :] = (v[:, h * LANES:(h + 1) * LANES] + ones_col).astype(BF16)

    _run_staggered(block(rows) for rows in _row_blocks(x_ref.shape[0]))


def _shared_kv(x, bsz, k_shift, k_scale, kv_norm_g, w_kv_a, kv_a_norm_g, w_kv_b, k_nope_g, k_rope_g,
               head_sum, head_spread, ct, st):
    t, d = x.shape
    s_len = t // bsz
    tm = ROW_TILE
    tiles = s_len // tm
    rank = kv_a_norm_g.shape[0]
    half = QK_ROPE_DIM // 2
    pad = jnp.zeros((d, LANES - QK_ROPE_DIM), F32)
    w_rope = w_kv_a[:, rank:]
    w_rope_sw = jnp.concatenate([w_rope[:, half:], w_rope[:, :half]], axis=1)
    wa = jnp.concatenate([w_kv_a[:, :rank], w_rope, pad, w_rope_sw, pad], axis=1).astype(BF16)
    wkv = w_kv_b.reshape(rank, N_HEADS, QK_NOPE_DIM + V_HEAD_DIM)
    wkn = wkv[:, :, :QK_NOPE_DIM].reshape(rank, N_HEADS * QK_NOPE_DIM).astype(BF16)
    wv = jnp.pad(wkv[:, :, QK_NOPE_DIM:], ((0, 0), (0, 0), (0, HEAD_LANES - V_HEAD_DIM)))
    wv = wv.reshape(rank, N_HEADS * HEAD_LANES).astype(BF16)
    gkn = jnp.tile(k_nope_g, N_HEADS).reshape(1, -1)
    lane_pad = lambda g: jnp.pad(g, (0, LANES - g.shape[0])).reshape(1, LANES)
    gkr = lane_pad(k_rope_g)
    gkrs = lane_pad(jnp.concatenate([k_rope_g[half:], k_rope_g[:half]]))
    full = _resident
    xspec = pl.BlockSpec((tm, d), lambda b, i: (b * tiles + i, 0))
    tab = pl.BlockSpec((tm, LANES), lambda b, i: (b * tiles + i, 0))
    per_b = pl.BlockSpec((1, 1, d), lambda b, i: (b, 0, 0))
    hspec = pl.BlockSpec((1, N_HEADS, tm, HEAD_LANES), lambda b, i: (b, 0, i, 0))
    args = (x, k_shift, k_scale, kv_norm_g.reshape(1, d), wa, kv_a_norm_g.reshape(1, rank), wkn, wv,
            gkn, gkr, gkrs, head_sum, head_spread, ct, st)
    in_specs = [xspec, per_b, per_b] + [full(a) for a in args[3:13]] + [tab, tab]
    return pl.pallas_call(
        _kv_kernel,
        grid=(bsz, tiles),
        in_specs=in_specs,
        out_specs=[hspec, hspec],
        out_shape=[jax.ShapeDtypeStruct((bsz, N_HEADS, s_len, HEAD_LANES), BF16)] * 2,
        compiler_params=_params("parallel", "parallel"),
        name="mla_shared_kv",
    )(*args)


def _q_kernel(x_ref, shift_ref, scale_ref, g_ref, wdq_ref, gq_ref, wn_ref, wr_ref, wrs_ref,
              gn_ref, gr_ref, grs_ref, sum_ref, spread_ref, seg32_ref, ct_ref, st_ref, q_ref):
    heads_per_blk = LANES // QK_ROPE_DIM

    def block(rows):
        x = x_ref[rows, :]
        hb = _modulate(_rms(x, g_ref[...]), shift_ref[0], scale_ref[0]).astype(BF16)
        yield
        cq = _dot(hb, wdq_ref[...])
        yield
        cq = _rms(cq, gq_ref[...]).astype(BF16)
        yield
        qn = _dot(cq, wn_ref[...])
        a = _dot(cq, wr_ref[...])
        asw = _dot(cq, wrs_ref[...])
        yield
        qn = qn * _head_rstd(qn, sum_ref, spread_ref, QK_NOPE_DIM) * (gn_ref[...] * Q_SCALE)
        ssr = _dot((a * a).astype(BF16), seg32_ref[...])
        yield
        rstd = lax.rsqrt(ssr * (1.0 / QK_ROPE_DIM) + EPS) * Q_SCALE
        ct = ct_ref[rows, :]
        st = st_ref[rows, :] * _rope_sign(ct.shape)
        for blk in range(N_HEADS // heads_per_blk):
            sl = slice(blk * LANES, (blk + 1) * LANES)
            qr = rstd[:, sl] * (a[:, sl] * gr_ref[:, sl] * ct + asw[:, sl] * grs_ref[:, sl] * st)
            lane = _lane_iota(qr.shape)
            for j in range(heads_per_blk):
                h_idx = blk * heads_per_blk + j
                shift = (QK_NOPE_DIM - j * QK_ROPE_DIM) % LANES
                r = pltpu.roll(qr, shift, 1) if shift else qr
                r = jnp.where(lane < QK_NOPE_DIM + QK_ROPE_DIM, r, 0.0)
                q_ref[0, h_idx, rows, :] = _head_tile(qn, r, h_idx).astype(BF16)
            yield

    _run_staggered(block(rows) for rows in _row_blocks(x_ref.shape[0]))


def _queries(x, bsz, shift, scale, norm_g, w_dq, q_norm_g, w_uq, q_nope_g, q_rope_g, head_sum, head_spread,
             seg32, ct, st):
    t, d = x.shape
    s_len = t // bsz
    tm = ROW_TILE
    tiles = s_len // tm
    rank = w_dq.shape[1]
    half = QK_ROPE_DIM // 2
    wq = w_uq.reshape(rank, N_HEADS, QK_NOPE_DIM + QK_ROPE_DIM)
    wn = wq[:, :, :QK_NOPE_DIM].reshape(rank, -1).astype(BF16)
    wr = wq[:, :, QK_NOPE_DIM:].reshape(rank, -1).astype(BF16)
    wrs = jnp.concatenate([wq[:, :, QK_NOPE_DIM + half:], wq[:, :, QK_NOPE_DIM:QK_NOPE_DIM + half]], axis=2)
    wrs = wrs.reshape(rank, -1).astype(BF16)
    gn = jnp.tile(q_nope_g, N_HEADS).reshape(1, -1)
    gr = jnp.tile(q_rope_g, N_HEADS).reshape(1, -1)
    grs = jnp.tile(jnp.concatenate([q_rope_g[half:], q_rope_g[:half]]), N_HEADS).reshape(1, -1)
    full = _resident
    xspec = pl.BlockSpec((tm, d), lambda b, i: (b * tiles + i, 0))
    tab = pl.BlockSpec((tm, LANES), lambda b, i: (b * tiles + i, 0))
    per_b = pl.BlockSpec((1, 1, d), lambda b, i: (b, 0, 0))
    hspec = pl.BlockSpec((1, N_HEADS, tm, HEAD_LANES), lambda b, i: (b, 0, i, 0))
    args = (x, shift, scale, norm_g.reshape(1, d), w_dq.astype(BF16), q_norm_g.reshape(1, rank), wn, wr, wrs,
            gn, gr, grs, head_sum, head_spread, seg32, ct, st)
    in_specs = [xspec, per_b, per_b] + [full(a) for a in args[3:15]] + [tab, tab]
    return pl.pallas_call(
        _q_kernel,
        grid=(bsz, tiles),
        in_specs=in_specs,
        out_specs=hspec,
        out_shape=jax.ShapeDtypeStruct((bsz, N_HEADS, s_len, HEAD_LANES), BF16),
        compiler_params=_params("parallel", "parallel"),
        name="mla_queries",
    )(*args)


def _chunk_offsets(tile):
    row_chunk = lax.broadcasted_iota(jnp.int32, (tile, tile), 0) // CHUNK
    col_chunk = lax.broadcasted_iota(jnp.int32, (tile, tile), 1) // CHUNK
    return row_chunk - col_chunk


def _merge_head_pair(accs):
    outs = [acc / acc[:, V_HEAD_DIM:V_HEAD_DIM + 1] for acc in accs]
    lane = _lane_iota(outs[0].shape)
    tiles = [jnp.where(lane < V_HEAD_DIM, outs[i], pltpu.roll(outs[i + 1], V_HEAD_DIM, 1))
             for i in range(0, len(outs), 2)]
    return tiles[0] if len(tiles) == 1 else jnp.concatenate(tiles, axis=1)


def _attn_online_kernel(q_ref, k_ref, v_ref, o_ref, m_scr, acc_scr, *, tile):
    i = pl.program_id(2)
    diag_mask = _chunk_offsets(tile) >= 0
    qs = [q_ref[0, hh] for hh in range(HEADS_PER_STEP)]

    def tile_update(off, diagonal):
        for hh in range(HEADS_PER_STEP):
            s = lax.dot_general(qs[hh], k_ref[0, hh, pl.ds(off, tile), :], _NT_DIMS,
                                preferred_element_type=F32)
            if diagonal:
                s = jnp.where(diag_mask, s, MASK_VALUE)
            row_max = jnp.max(s, axis=-1, keepdims=True)
            m = row_max if diagonal else m_scr[hh]
            m_new = row_max if diagonal else jnp.maximum(m, row_max)
            pv = _dot(jnp.exp2(s - m_new).astype(BF16), v_ref[0, hh, pl.ds(off, tile), :])
            acc_scr[hh] = pv if diagonal else acc_scr[hh] * jnp.exp2(m - m_new) + pv
            m_scr[hh] = m_new

    tile_update(pl.multiple_of(i * tile, tile), True)

    def below_diagonal(j, carry):
        tile_update(pl.multiple_of(j * tile, tile), False)
        return carry

    lax.fori_loop(0, i, below_diagonal, 0)
    o_ref[0] = _merge_head_pair([acc_scr[hh] for hh in range(HEADS_PER_STEP)]).astype(o_ref.dtype)


def _attn_bounded_kernel(q_ref, k_ref, v_ref, o_ref, *, tile, n_tiles):
    g = pl.program_id(2)
    lo, hi = g, n_tiles - 1 - g
    diag_mask = _chunk_offsets(tile) >= 0
    zero = jnp.zeros((tile, HEAD_LANES), F32)
    acc_lo = [zero] * HEADS_PER_STEP
    acc_hi = [zero] * HEADS_PER_STEP

    def rows(ref, hh, idx):
        return ref[0, hh, pl.ds(pl.multiple_of(idx * tile, tile), tile), :]

    def weighted_values(q, k, v, mask):
        s = lax.dot_general(q, k, _NT_DIMS, preferred_element_type=F32)
        if mask is not None:
            s = jnp.where(mask, s, MASK_VALUE)
        return _dot(jnp.exp2(s).astype(BF16), v)

    def diagonal(idx, accs):
        for hh in range(HEADS_PER_STEP):
            accs[hh] = accs[hh] + weighted_values(rows(q_ref, hh, idx), rows(k_ref, hh, idx),
                                                  rows(v_ref, hh, idx), diag_mask)

    diagonal(lo, acc_lo)
    diagonal(hi, acc_hi)
    max_lo = n_tiles // 2 - 1
    for slot in range(1, n_tiles):
        owner_known = slot > max_lo
        if owner_known:
            q_idx, k_idx = hi, slot - 1 - lo
        else:
            is_lo = slot <= lo
            q_idx = jnp.where(is_lo, lo, hi)
            k_idx = jnp.where(is_lo, slot - 1, slot - 1 - lo)
        for hh in range(HEADS_PER_STEP):
            pv = weighted_values(rows(q_ref, hh, q_idx), rows(k_ref, hh, k_idx), rows(v_ref, hh, k_idx), None)
            if owner_known:
                acc_hi[hh] = acc_hi[hh] + pv
            else:
                acc_lo[hh] = acc_lo[hh] + jnp.where(is_lo, pv, 0.0)
                acc_hi[hh] = acc_hi[hh] + jnp.where(is_lo, 0.0, pv)
    o_ref[0, pl.ds(pl.multiple_of(lo * tile, tile), tile), :] = _merge_head_pair(acc_lo).astype(o_ref.dtype)
    o_ref[0, pl.ds(pl.multiple_of(hi * tile, tile), tile), :] = _merge_head_pair(acc_hi).astype(o_ref.dtype)


def _score_bound(q_nope_g, q_rope_g, k_nope_g, k_rope_g):
    def sq_len(g_nope, g_rope):
        return QK_NOPE_DIM * jnp.max(g_nope * g_nope) + QK_ROPE_DIM * jnp.max(g_rope * g_rope)
    return Q_SCALE * jnp.sqrt(sq_len(q_nope_g, q_rope_g) * sq_len(k_nope_g, k_rope_g)) * BOUND_SLACK


def _attention(q, k, v, bound):
    return lax.cond(bound <= MAX_UNSHIFTED_BOUND, _attention_bounded, _attention_online, q, k, v)


def _attention_online(q, k, v):
    bsz, nh, s_len, hl = q.shape
    tile = ATTN_TILE
    hp = HEADS_PER_STEP
    whole = pl.BlockSpec((1, hp, s_len, hl), lambda b, h, i: (b, h, 0, 0))
    return pl.pallas_call(
        functools.partial(_attn_online_kernel, tile=tile),
        grid=(bsz, nh // hp, s_len // tile),
        in_specs=[pl.BlockSpec((1, hp, tile, hl), lambda b, h, i: (b, h, i, 0)), whole, whole],
        out_specs=pl.BlockSpec((1, tile, hp * V_HEAD_DIM), lambda b, h, i: (b, i, h)),
        out_shape=jax.ShapeDtypeStruct((bsz, s_len, nh * V_HEAD_DIM), BF16),
        scratch_shapes=[pltpu.VMEM((hp, tile, 1), F32), pltpu.VMEM((hp, tile, hl), F32)],
        compiler_params=_params("parallel", "parallel", "arbitrary"),
        name="mla_attention_online",
    )(q, k, v)


def _attention_bounded(q, k, v):
    bsz, nh, s_len, hl = q.shape
    tile = ATTN_TILE
    hp = HEADS_PER_STEP
    n_tiles = s_len // tile
    whole = pl.BlockSpec((1, hp, s_len, hl), lambda b, h, g: (b, h, 0, 0))
    return pl.pallas_call(
        functools.partial(_attn_bounded_kernel, tile=tile, n_tiles=n_tiles),
        grid=(bsz, nh // hp, n_tiles // 2),
        in_specs=[whole, whole, whole],
        out_specs=pl.BlockSpec((1, s_len, hp * V_HEAD_DIM), lambda b, h, g: (b, 0, h)),
        out_shape=jax.ShapeDtypeStruct((bsz, s_len, nh * V_HEAD_DIM), BF16),
        compiler_params=_params("parallel", "parallel", "arbitrary"),
        name="mla_attention_bounded",
    )(q, k, v)


def _seg_ones(n, seg):
    idx = jnp.arange(n) // seg
    return (idx[:, None] == idx[None, :]).astype(BF16)


def _ffn_weights(w_gate, w_up, w_down):
    return w_gate.astype(BF16), w_up.astype(BF16), w_down.astype(BF16)


def kernel(x, c, positions, ada_w, ada_b, norm1_g, norm2_g, ffn_w_gate, ffn_w_up, ffn_w_down, s5_lam_re, s5_lam_im, s5_log_dt, s5_b_re, s5_b_im, s5_c_re, s5_c_im, s5_d, s5_w_glu, s5_b_glu, kv_ada_w, kv_ada_b, kv_norm_g, w_kv_a, kv_a_norm_g, w_kv_b, k_nope_norm_g, k_rope_norm_g, mla_w_dq, mla_q_norm_g, mla_w_uq, mla_q_nope_norm_g, mla_q_rope_norm_g, mla_w_o):
    bsz, s_len, d = x.shape
    depth = ada_w.shape[0]
    n_a = s5_lam_re.shape[0]
    t = bsz * s_len

    mods = _mods(c, ada_w, ada_b).reshape(depth, bsz, 6, d)
    kv_mods = _mods(c, kv_ada_w[None], kv_ada_b[None]).reshape(bsz, 2, d)
    cos_t, sin_t = _rope_tables(positions)

    a2_re, a2_im, *s5_mats = _s5_discretise(s5_lam_re, s5_lam_im, s5_log_dt, s5_b_re, s5_b_im, s5_c_re, s5_c_im)

    xt = x
    for l in range(n_a):
        m = mods[l]
        bb_re, bb_im, abb_re, abb_im, ca_re, ca_im, cb = (w[l] for w in s5_mats)
        slab_weights = _s5_slab_weights(bb_re, bb_im, abb_re, abb_im, s5_c_re[l], s5_c_im[l], ca_re, ca_im, cb)
        xt = _s5_mixer(xt, m[:, 0], m[:, 1], m[:, 2], norm1_g[l], slab_weights, a2_re[l], a2_im[l], s5_d[l],
                       s5_w_glu[l].astype(BF16), s5_b_glu[l], bsz)
        wg, wu, wd = _ffn_weights(ffn_w_gate[l], ffn_w_up[l], ffn_w_down[l])
        xt = _ffn(xt, m[:, 3], m[:, 4], m[:, 5], norm2_g[l], wg, wu, wd, out_batch_major=l == n_a - 1)
    xb = xt.reshape(t, d)

    head_of_lane = jnp.arange(N_HEADS * QK_NOPE_DIM) // QK_NOPE_DIM
    head_sum = (head_of_lane[:, None] == jnp.arange(LANES)[None, :]).astype(BF16)
    head_spread = jnp.concatenate([head_sum.T, head_sum.T], axis=0)
    seg32 = _seg_ones(N_HEADS * QK_ROPE_DIM, QK_ROPE_DIM)
    per_b = lambda v: v.reshape(bsz, 1, d)
    k_all, v_all = _shared_kv(xb, bsz, per_b(kv_mods[:, 0]), per_b(kv_mods[:, 1]), kv_norm_g, w_kv_a,
                              kv_a_norm_g, w_kv_b, k_nope_norm_g, k_rope_norm_g, head_sum, head_spread,
                              cos_t, sin_t)
    for l in range(n_a, depth):
        j = l - n_a
        m = mods[l]
        bound = _score_bound(mla_q_nope_norm_g[j], mla_q_rope_norm_g[j], k_nope_norm_g, k_rope_norm_g)
        q = _queries(xb, bsz, per_b(m[:, 0]), per_b(m[:, 1]), norm1_g[l], mla_w_dq[j], mla_q_norm_g[j],
                     mla_w_uq[j], mla_q_nope_norm_g[j], mla_q_rope_norm_g[j], head_sum, head_spread, seg32,
                     cos_t, sin_t)
        o = _attention(q, k_all, v_all, bound).reshape(t, N_HEADS * V_HEAD_DIM)
        wg, wu, wd = _ffn_weights(ffn_w_gate[l], ffn_w_up[l], ffn_w_down[l])
        xb = _ffn(xb, per_b(m[:, 3]), per_b(m[:, 4]), per_b(m[:, 5]), norm2_g[l], wg, wu, wd,
                  attn=(o, mla_w_o[j].astype(BF16), per_b(m[:, 2])))
    return xb.reshape(bsz, s_len, d)
```

```python
import functools
import math

import jax
import jax.numpy as jnp
from jax import lax
from jax.experimental import pallas as pl
from jax.experimental.pallas import tpu as pltpu

F32 = jnp.float32
BF16 = jnp.bfloat16

CHUNK = 64
SSM_GROUP = 16
SSM_STATE = 64
N_HEADS = 16
QK_NOPE_DIM = 64
QK_ROPE_DIM = 32
V_HEAD_DIM = 64
ROPE_THETA = 10000.0
ATTN_SCALE = 1.0 / math.sqrt(QK_NOPE_DIM + QK_ROPE_DIM)
Q_SCALE = ATTN_SCALE * math.log2(math.e)
EPS = 1e-6
MASK_VALUE = -1e30
BOUND_SLACK = 1.01
MAX_UNSHIFTED_BOUND = 40.0

LANES = 128
SUBLANES = 8
HEAD_LANES = 128
SLAB_GROUPS = LANES // SSM_GROUP
VMEM_LIMIT = 56 * 1024 * 1024

ROW_TILE = 512
ROW_BLOCKS = 2
S5_TIME_TILE = 64
RELAYOUT_PITCH_PAD = 8
FFN_CHUNK = 256
ATTN_TILE = 512
HEADS_PER_STEP = 4


def _params(*sem):
    return pltpu.CompilerParams(dimension_semantics=sem, vmem_limit_bytes=VMEM_LIMIT)


def _rms(x, g):
    return x * lax.rsqrt(jnp.mean(x * x, axis=-1, keepdims=True) + EPS) * g


def _rowwise(fn, a, *mods):
    r = mods[0].shape[0]
    if r == 1:
        return fn(a, *mods)
    rows, d = a.shape
    out = fn(a.reshape(rows // r, r, d), *[m[None] for m in mods])
    return out.reshape(rows, d)


def _modulate(h, shift, scale):
    return _rowwise(lambda a, sh, sc: a * (1.0 + sc) + sh, h, shift, scale)


def _gated_add(x, gate, upd):
    return x + _rowwise(lambda a, g: a * g, upd, gate)


def _dot(a, b):
    return jnp.dot(a, b, preferred_element_type=F32)


_NT_DIMS = (((1,), (1,)), ((), ()))


def _resident(a):
    zeros = (0,) * a.ndim
    return pl.BlockSpec(a.shape, lambda *_: zeros, pipeline_mode=pl.Buffered(1))


def _lane_iota(shape):
    return lax.broadcasted_iota(jnp.int32, shape, len(shape) - 1)


def _rows_bt_to_tb(x_ref, scr):
    bsz, ts, d = x_ref.shape
    pitch = scr.shape[1] // bsz
    cols = []
    for s in range(d // LANES):
        lanes = slice(s * LANES, (s + 1) * LANES)
        for b in range(bsz):
            scr[s, b * pitch:b * pitch + ts, :] = x_ref[b, :, lanes]
        cols.append(jnp.concatenate([scr[s, pl.ds(t, bsz, stride=pitch), :] for t in range(ts)], axis=0))
    return jnp.concatenate(cols, axis=1)


def _rows_tb_to_bt(val, scr, out_ref):
    bsz, ts, d = out_ref.shape
    for s in range(d // LANES):
        lanes = slice(s * LANES, (s + 1) * LANES)
        scr[s] = val[:, lanes]
        for b in range(bsz):
            out_ref[b, :, lanes] = scr[s, pl.ds(b, ts, stride=bsz), :]


def _row_blocks(n_rows):
    step = n_rows // ROW_BLOCKS
    return [slice(i * step, (i + 1) * step) for i in range(ROW_BLOCKS)]


def _run_staggered(blocks):
    waiting, live = list(blocks), []
    while waiting or live:
        if waiting:
            live.append(waiting.pop(0))
        for gen in list(live):
            if next(gen, StopIteration) is StopIteration:
                live.remove(gen)


def _head_rstd(x, sum_ref, spread_ref, dim):
    ss = _dot((x * x).astype(BF16), sum_ref[...])
    rstd = lax.rsqrt(ss * (1.0 / dim) + EPS)
    hi = rstd.astype(BF16)
    lo = (rstd - hi.astype(F32)).astype(BF16)
    return _dot(jnp.concatenate([hi, lo], axis=1), spread_ref[...])


def _mods_kernel(c_ref, w_ref, b_ref, o_ref):
    c = c_ref[...]
    ca = c * jax.nn.sigmoid(c)
    o_ref[0] = _dot(ca.astype(BF16), w_ref[0].astype(BF16)) + b_ref[0]


def _mods(c, w, b, tn=2048):
    nl, d, n = w.shape
    bsz = c.shape[0]
    return pl.pallas_call(
        _mods_kernel,
        grid=(nl, n // tn),
        in_specs=[pl.BlockSpec((bsz, d), lambda l, j: (0, 0)),
                  pl.BlockSpec((1, d, tn), lambda l, j: (l, 0, j)),
                  pl.BlockSpec((1, 1, tn), lambda l, j: (l, 0, j))],
        out_specs=pl.BlockSpec((1, bsz, tn), lambda l, j: (l, 0, j)),
        out_shape=jax.ShapeDtypeStruct((nl, bsz, n), F32),
        compiler_params=_params("parallel", "parallel"),
        name="adaln_mods",
    )(c, w, b.reshape(nl, 1, n))


def _rope_kernel(pos_ref, inv_ref, spread_ref, cos_ref, sin_ref):
    ang = inv_ref[...] * pos_ref[...]
    spread = functools.partial(lax.dot_general, dimension_numbers=(((0,), (0,)), ((), ())),
                               precision=lax.Precision.HIGHEST, preferred_element_type=F32)
    cos_ref[...] = spread(jnp.cos(ang), spread_ref[...])
    sin_ref[...] = spread(jnp.sin(ang), spread_ref[...])


def _rope_tables(positions, tn=4096):
    t = positions.size
    half = QK_ROPE_DIM // 2
    inv = 1.0 / (ROPE_THETA ** (jnp.arange(0, QK_ROPE_DIM, 2, dtype=F32) / QK_ROPE_DIM))
    pos = positions.astype(F32).reshape(1, t)
    spread = (jnp.arange(LANES)[None, :] % half == jnp.arange(half)[:, None]).astype(F32)
    return pl.pallas_call(
        _rope_kernel,
        grid=(t // tn,),
        in_specs=[pl.BlockSpec((1, tn), lambda i: (0, i)),
                  pl.BlockSpec((half, 1), lambda i: (0, 0)),
                  pl.BlockSpec((half, LANES), lambda i: (0, 0))],
        out_specs=[pl.BlockSpec((tn, LANES), lambda i: (i, 0))] * 2,
        out_shape=[jax.ShapeDtypeStruct((t, LANES), F32)] * 2,
        compiler_params=_params("parallel"),
        name="rope_tables",
    )(pos, inv.reshape(half, 1), spread)


def _rope_sign(shape):
    lane = _lane_iota(shape)
    return jnp.where((lane % QK_ROPE_DIM) < (QK_ROPE_DIM // 2), -1.0, 1.0).astype(F32)


def _cmul(ar, ai, br, bi):
    return ar * br - ai * bi, ar * bi + ai * br


def _s5_disc_kernel(lr_ref, li_ref, ldt_ref, br_ref, bi_ref, cr_ref, ci_ref,
                    a2r_ref, a2i_ref, bbr_ref, bbi_ref, abbr_ref, abbi_ref, car_ref, cai_ref, cb_ref):
    lr = lr_ref[0]
    li = li_ref[0]
    dt = jnp.exp(ldt_ref[0])
    mag = jnp.exp(lr * dt)
    a_re = mag * jnp.cos(li * dt)
    a_im = mag * jnp.sin(li * dt)
    den = lr * lr + li * li
    nr = a_re - 1.0
    ni = a_im
    f_re = (nr * lr + ni * li) / den
    f_im = (ni * lr - nr * li) / den
    bb_re, bb_im = _cmul(f_re, f_im, br_ref[0], bi_ref[0])
    cr = cr_ref[0]
    ci = ci_ref[0]
    a2r_ref[0], a2i_ref[0] = _cmul(a_re, a_im, a_re, a_im)
    bbr_ref[0] = bb_re
    bbi_ref[0] = bb_im
    abbr_ref[0], abbi_ref[0] = _cmul(a_re, a_im, bb_re, bb_im)
    car_ref[0], cai_ref[0] = _cmul(cr, ci, a_re, a_im)
    group_dot = functools.partial(jnp.einsum, "gpn,gqn->gpq", preferred_element_type=F32,
                                  precision=lax.Precision.HIGHEST)
    cb_ref[0] = group_dot(cr, bb_re) - group_dot(ci, bb_im)


def _s5_discretise(lam_re, lam_im, log_dt, b_re, b_im, c_re, c_im):
    na, g, n = lam_re.shape
    p = b_re.shape[-1]
    vec = pl.BlockSpec((1, g, 1, n), lambda l: (l, 0, 0, 0))
    mat = pl.BlockSpec((1, g, p, n), lambda l: (l, 0, 0, 0))
    vec_shape = jax.ShapeDtypeStruct((na, g, 1, n), F32)
    mat_shape = jax.ShapeDtypeStruct((na, g, p, n), F32)
    return pl.pallas_call(
        _s5_disc_kernel,
        grid=(na,),
        in_specs=[vec, vec, pl.BlockSpec((1, g, 1, 1), lambda l: (l, 0, 0, 0)), mat, mat, mat, mat],
        out_specs=[vec, vec] + [mat] * 6 + [pl.BlockSpec((1, g, p, p), lambda l: (l, 0, 0, 0))],
        out_shape=[vec_shape] * 2 + [mat_shape] * 6 + [jax.ShapeDtypeStruct((na, g, p, p), F32)],
        compiler_params=_params("parallel"),
        name="s5_discretise",
    )(lam_re.reshape(na, g, 1, n), lam_im.reshape(na, g, 1, n), log_dt.reshape(na, g, 1, 1),
      b_re.transpose(0, 1, 3, 2), b_im.transpose(0, 1, 3, 2), c_re, c_im)


def _slab_block_diag(w_gab):
    g, a, b = w_gab.shape
    ns = g // SLAB_GROUPS
    w = w_gab.reshape(ns, SLAB_GROUPS, a, b)
    eye = jnp.eye(SLAB_GROUPS, dtype=w.dtype)
    return jnp.einsum("kgab,gh->kgahb", w, eye).reshape(ns, SLAB_GROUPS * a, SLAB_GROUPS * b)


def _s5_slab_weights(bb_re, bb_im, abb_re, abb_im, c_re, c_im, ca_re, ca_im, cb):
    col = lambda re, im: jnp.concatenate([_slab_block_diag(re), _slab_block_diag(im)], axis=2)
    b_pair = jnp.concatenate([col(bb_re, bb_im), col(abb_re, abb_im)], axis=1)
    out = lambda w: _slab_block_diag(w.transpose(0, 2, 1))
    c_pair_re = jnp.concatenate([out(c_re), out(ca_re)], axis=2)
    c_pair_im = jnp.concatenate([out(c_im), out(ca_im)], axis=2)
    cb_blk = _slab_block_diag(cb.transpose(0, 2, 1))
    return tuple(w.astype(BF16) for w in (b_pair, c_pair_re, c_pair_im, cb_blk))


def _s5_mixer_kernel(x_ref, shift_ref, scale_ref, gate_ref, g_ref, bpair_ref, cre_ref, cim_ref, cb_ref,
                     a2r_ref, a2i_ref, d_ref, wglu_ref, bglu_ref, o_ref,
                     bu_re, bu_im, st_re, st_im, tail_scr, yf_scr, ys_scr, *maybe_relayout_scr,
                     ts, n_slabs, slab_state):
    @pl.when(pl.program_id(0) == 0)
    def _():
        st_re[...] = jnp.zeros_like(st_re)
        st_im[...] = jnp.zeros_like(st_im)
        tail_scr[...] = jnp.zeros_like(tail_scr)

    pairs = ts // 2
    pair_rows = pairs * SUBLANES
    x = _rows_bt_to_tb(x_ref, *maybe_relayout_scr) if maybe_relayout_scr else x_ref[...]
    d = x.shape[1]
    h = _modulate(_rms(x, g_ref[...]), shift_ref[...], scale_ref[...])
    h3 = h.reshape(pairs, 2 * SUBLANES, d)
    h_first = h3[:, :SUBLANES].reshape(pair_rows, d).astype(BF16)
    h_second = h3[:, SUBLANES:].reshape(pair_rows, d).astype(BF16)

    def project_in(k):
        lanes = slice(k * LANES, (k + 1) * LANES)
        ssl = slice(k * slab_state, (k + 1) * slab_state)
        bu = _dot(jnp.concatenate([h_second[:, lanes], h_first[:, lanes]], axis=1), bpair_ref[k])
        bu_re[:, ssl] = bu[:, :slab_state]
        bu_im[:, ssl] = bu[:, slab_state:]

    def recur(k):
        ssl = slice(k * slab_state, (k + 1) * slab_state)
        ar = jnp.broadcast_to(a2r_ref[:, ssl], (SUBLANES, slab_state))
        ai = jnp.broadcast_to(a2i_ref[:, ssl], (SUBLANES, slab_state))
        sr, si = st_re[:, ssl], st_im[:, ssl]
        for j in range(pairs):
            rows = slice(j * SUBLANES, (j + 1) * SUBLANES)
            sr, si = ar * sr - ai * si + bu_re[rows, ssl], ar * si + ai * sr + bu_im[rows, ssl]
            bu_re[rows, ssl] = sr
            bu_im[rows, ssl] = si
        st_re[:, ssl] = sr
        st_im[:, ssl] = si

    def project_out(k):
        ssl = slice(k * slab_state, (k + 1) * slab_state)
        lanes = slice(k * LANES, (k + 1) * LANES)
        z = _dot(bu_re[:, ssl].astype(BF16), cre_ref[k]) - _dot(bu_im[:, ssl].astype(BF16), cim_ref[k])
        ys_scr[:, lanes] = z[:, :LANES]
        carried = jnp.concatenate([tail_scr[:, lanes], z[:pair_rows - SUBLANES, LANES:]], axis=0)
        tail_scr[:, lanes] = z[pair_rows - SUBLANES:, LANES:]
        yf_scr[:, lanes] = carried + _dot(h_first[:, lanes], cb_ref[k])

    for k in range(n_slabs + 2):
        if k < n_slabs:
            project_in(k)
        if 0 <= k - 1 < n_slabs:
            recur(k - 1)
        if 0 <= k - 2 < n_slabs:
            project_out(k - 2)

    y = jnp.concatenate([yf_scr[...].reshape(pairs, SUBLANES, d), ys_scr[...].reshape(pairs, SUBLANES, d)],
                        axis=1).reshape(2 * pair_rows, d)
    y = y + d_ref[...] * h
    gl = jax.nn.gelu(y)
    z = _dot(gl.astype(BF16), wglu_ref[...]) + bglu_ref[...]
    mix = gl * jax.nn.sigmoid(z)
    o_ref[...] = _gated_add(x, gate_ref[...], mix)


def _s5_mixer(x, shift, scale, gate, norm_g, slab_weights, a2_re, a2_im, d_skip, w_glu, b_glu, bsz):
    assert bsz == SUBLANES
    ts = S5_TIME_TILE
    rows = ts * bsz
    pair_rows = rows // 2
    if x.ndim == 3:
        d = x.shape[2]
        t = x.shape[0] * x.shape[1]
        x_spec = pl.BlockSpec((bsz, ts, d), lambda i: (0, i, 0))
        relayout_scr = [pltpu.VMEM((d // LANES, bsz * (ts + RELAYOUT_PITCH_PAD), LANES), F32)]
    else:
        t, d = x.shape
        x_spec = pl.BlockSpec((rows, d), lambda i: (i, 0))
        relayout_scr = []
    b_pair, c_pair_re, c_pair_im, cb_blk = slab_weights
    n_slabs, _, two_state = b_pair.shape
    slab_state = two_state // 2
    n_state = n_slabs * slab_state
    full = _resident
    row = lambda a: a.reshape(1, -1)
    args = (x, shift, scale, gate, row(norm_g), b_pair, c_pair_re, c_pair_im, cb_blk, row(a2_re), row(a2_im),
            row(d_skip), w_glu, row(b_glu))
    return pl.pallas_call(
        functools.partial(_s5_mixer_kernel, ts=ts, n_slabs=n_slabs, slab_state=slab_state),
        grid=(t // rows,),
        in_specs=[x_spec] + [full(a) for a in args[1:]],
        out_specs=pl.BlockSpec((rows, d), lambda i: (i, 0)),
        out_shape=jax.ShapeDtypeStruct((t, d), F32),
        scratch_shapes=[pltpu.VMEM((pair_rows, n_state), F32), pltpu.VMEM((pair_rows, n_state), F32),
                        pltpu.VMEM((bsz, n_state), F32), pltpu.VMEM((bsz, n_state), F32),
                        pltpu.VMEM((bsz, d), F32), pltpu.VMEM((pair_rows, d), F32),
                        pltpu.VMEM((pair_rows, d), F32)] + relayout_scr,
        compiler_params=_params("arbitrary"),
        name="s5_mixer",
    )(*args)


def _ffn_kernel(*refs, n_chunks, with_attn, out_batch_major):
    if out_batch_major:
        *refs, relayout_scr = refs
    if with_attn:
        (x_ref, o_in_ref, wo_ref, gate1_ref, shift_ref, scale_ref, gate2_ref, g_ref,
         wg_ref, wu_ref, wd_ref, out_ref, h_scr, acc_scr) = refs
        x = x_ref[...] + gate1_ref[0] * _dot(o_in_ref[...], wo_ref[...])
        shift, scale, gate2 = shift_ref[0], scale_ref[0], gate2_ref[0]
    else:
        (x_ref, shift_ref, scale_ref, gate2_ref, g_ref,
         wg_ref, wu_ref, wd_ref, out_ref, h_scr, acc_scr) = refs
        x = x_ref[...]
        shift, scale, gate2 = shift_ref[...], scale_ref[...], gate2_ref[...]

    h_scr[...] = _modulate(_rms(x, g_ref[...]), shift, scale).astype(BF16)
    acc_scr[...] = jnp.zeros_like(acc_scr)

    hb = h_scr[...]
    for f in range(n_chunks):
        cols = slice(f * FFN_CHUNK, (f + 1) * FFN_CHUNK)
        gt = _dot(hb, wg_ref[:, cols])
        up = _dot(hb, wu_ref[:, cols])
        act = (gt * jax.nn.sigmoid(gt) * up).astype(BF16)
        acc_scr[...] += _dot(act, wd_ref[cols, :])
    out = _gated_add(x, gate2, acc_scr[...])
    if out_batch_major:
        _rows_tb_to_bt(out, relayout_scr, out_ref)
    else:
        out_ref[...] = out


def _ffn(x, shift, scale, gate2, norm_g, wg, wu, wd, attn=None, out_batch_major=False):
    t, d = x.shape
    n_chunks = wg.shape[1] // FFN_CHUNK
    tm = ROW_TILE
    full = _resident
    xspec = pl.BlockSpec((tm, d), lambda i: (i, 0))
    g2 = norm_g.reshape(1, d)
    if attn is None:
        args = (x, shift, scale, gate2, g2, wg, wu, wd)
        in_specs = [xspec] + [full(a) for a in args[1:]]
    else:
        o_in, w_o, gate1 = attn
        tiles_per_batch = (t // gate1.shape[0]) // tm
        per_b = pl.BlockSpec((1, 1, d), lambda i: (i // tiles_per_batch, 0, 0))
        args = (x, o_in, w_o, gate1, shift, scale, gate2, g2, wg, wu, wd)
        in_specs = [xspec, xspec, full(w_o), per_b, per_b, per_b, per_b, full(g2), full(wg), full(wu), full(wd)]
    scratch = [pltpu.VMEM((tm, d), BF16), pltpu.VMEM((tm, d), F32)]
    out_spec, out_shape = xspec, jax.ShapeDtypeStruct((t, d), F32)
    if out_batch_major:
        bsz = shift.shape[0]
        out_spec = pl.BlockSpec((bsz, tm // bsz, d), lambda i: (0, i, 0))
        out_shape = jax.ShapeDtypeStruct((bsz, t // bsz, d), F32)
        scratch.append(pltpu.VMEM((d // LANES, tm, LANES), F32))
    return pl.pallas_call(
        functools.partial(_ffn_kernel, n_chunks=n_chunks, with_attn=attn is not None,
                          out_batch_major=out_batch_major),
        grid=(t // tm,),
        in_specs=in_specs,
        out_specs=out_spec,
        out_shape=out_shape,
        scratch_shapes=scratch,
        compiler_params=_params("parallel"),
        name="ffn_attn_out" if attn is not None else "ffn",
    )(*args)


def _head_tile(nope, rope_at_64, h):
    blk = nope[:, (h // 2) * LANES:(h // 2 + 1) * LANES]
    if h % 2:
        blk = pltpu.roll(blk, QK_NOPE_DIM, 1)
    return jnp.where(_lane_iota(blk.shape) < QK_NOPE_DIM, blk, rope_at_64)


def _kv_kernel(x_ref, shift_ref, scale_ref, g_ref, wa_ref, ga_ref, wkn_ref, wv_ref, gkn_ref,
               gkr_ref, gkrs_ref, sum_ref, spread_ref, ct_ref, st_ref, k_ref, v_ref):
    rank = ga_ref.shape[1]
    ones_col = (_lane_iota((1, LANES)) == V_HEAD_DIM).astype(F32)

    def block(rows):
        x = x_ref[rows, :]
        hb = _modulate(_rms(x, g_ref[...]), shift_ref[0], scale_ref[0]).astype(BF16)
        yield
        kva = _dot(hb, wa_ref[...])
        yield
        cb = _rms(kva[:, :rank], ga_ref[...]).astype(BF16)
        a = kva[:, rank:rank + LANES]
        asw = kva[:, rank + LANES:rank + 2 * LANES]
        rstd = lax.rsqrt(jnp.sum(a * a, axis=-1, keepdims=True) * (1.0 / QK_ROPE_DIM) + EPS)
        sgn = _rope_sign(a.shape)
        kr = rstd * (a * gkr_ref[...] * ct_ref[rows, :] + asw * gkrs_ref[...] * (st_ref[rows, :] * sgn))
        kr64 = pltpu.roll(kr, QK_NOPE_DIM, 1)
        yield
        kn = _dot(cb, wkn_ref[...])
        v = _dot(cb, wv_ref[...])
        yield
        kn = kn * _head_rstd(kn, sum_ref, spread_ref, QK_NOPE_DIM) * gkn_ref[...]
        yield
        for h in range(N_HEADS):
            k_ref[0, h, rows, :] = _head_tile(kn, kr64, h).astype(BF16)
            v_ref[0, h, rows, :] = (v[:, h * LANES:(h + 1) * LANES] + ones_col).astype(BF16)

    _run_staggered(block(rows) for rows in _row_blocks(x_ref.shape[0]))


def _shared_kv(x, bsz, k_shift, k_scale, kv_norm_g, w_kv_a, kv_a_norm_g, w_kv_b, k_nope_g, k_rope_g,
               head_sum, head_spread, ct, st):
    t, d = x.shape
    s_len = t // bsz
    tm = ROW_TILE
    tiles = s_len // tm
    rank = kv_a_norm_g.shape[0]
    half = QK_ROPE_DIM // 2
    pad = jnp.zeros((d, LANES - QK_ROPE_DIM), F32)
    w_rope = w_kv_a[:, rank:]
    w_rope_sw = jnp.concatenate([w_rope[:, half:], w_rope[:, :half]], axis=1)
    wa = jnp.concatenate([w_kv_a[:, :rank], w_rope, pad, w_rope_sw, pad], axis=1).astype(BF16)
    wkv = w_kv_b.reshape(rank, N_HEADS, QK_NOPE_DIM + V_HEAD_DIM)
    wkn = wkv[:, :, :QK_NOPE_DIM].reshape(rank, N_HEADS * QK_NOPE_DIM).astype(BF16)
    wv = jnp.pad(wkv[:, :, QK_NOPE_DIM:], ((0, 0), (0, 0), (0, HEAD_LANES - V_HEAD_DIM)))
    wv = wv.reshape(rank, N_HEADS * HEAD_LANES).astype(BF16)
    gkn = jnp.tile(k_nope_g, N_HEADS).reshape(1, -1)
    lane_pad = lambda g: jnp.pad(g, (0, LANES - g.shape[0])).reshape(1, LANES)
    gkr = lane_pad(k_rope_g)
    gkrs = lane_pad(jnp.concatenate([k_rope_g[half:], k_rope_g[:half]]))
    full = _resident
    xspec = pl.BlockSpec((tm, d), lambda b, i: (b * tiles + i, 0))
    tab = pl.BlockSpec((tm, LANES), lambda b, i: (b * tiles + i, 0))
    per_b = pl.BlockSpec((1, 1, d), lambda b, i: (b, 0, 0))
    hspec = pl.BlockSpec((1, N_HEADS, tm, HEAD_LANES), lambda b, i: (b, 0, i, 0))
    args = (x, k_shift, k_scale, kv_norm_g.reshape(1, d), wa, kv_a_norm_g.reshape(1, rank), wkn, wv,
            gkn, gkr, gkrs, head_sum, head_spread, ct, st)
    in_specs = [xspec, per_b, per_b] + [full(a) for a in args[3:13]] + [tab, tab]
    return pl.pallas_call(
        _kv_kernel,
        grid=(bsz, tiles),
        in_specs=in_specs,
        out_specs=[hspec, hspec],
        out_shape=[jax.ShapeDtypeStruct((bsz, N_HEADS, s_len, HEAD_LANES), BF16)] * 2,
        compiler_params=_params("parallel", "parallel"),
        name="mla_shared_kv",
    )(*args)


def _q_kernel(x_ref, shift_ref, scale_ref, g_ref, wdq_ref, gq_ref, wn_ref, wr_ref, wrs_ref,
              gn_ref, gr_ref, grs_ref, sum_ref, spread_ref, seg32_ref, ct_ref, st_ref, q_ref):
    heads_per_blk = LANES // QK_ROPE_DIM

    def block(rows):
        x = x_ref[rows, :]
        hb = _modulate(_rms(x, g_ref[...]), shift_ref[0], scale_ref[0]).astype(BF16)
        yield
        cq = _dot(hb, wdq_ref[...])
        yield
        cq = _rms(cq, gq_ref[...]).astype(BF16)
        yield
        qn = _dot(cq, wn_ref[...])
        a = _dot(cq, wr_ref[...])
        asw = _dot(cq, wrs_ref[...])
        yield
        qn = qn * _head_rstd(qn, sum_ref, spread_ref, QK_NOPE_DIM) * (gn_ref[...] * Q_SCALE)
        ssr = _dot((a * a).astype(BF16), seg32_ref[...])
        yield
        rstd = lax.rsqrt(ssr * (1.0 / QK_ROPE_DIM) + EPS) * Q_SCALE
        ct = ct_ref[rows, :]
        st = st_ref[rows, :] * _rope_sign(ct.shape)
        for blk in range(N_HEADS // heads_per_blk):
            sl = slice(blk * LANES, (blk + 1) * LANES)
            qr = rstd[:, sl] * (a[:, sl] * gr_ref[:, sl] * ct + asw[:, sl] * grs_ref[:, sl] * st)
            lane = _lane_iota(qr.shape)
            for j in range(heads_per_blk):
                h_idx = blk * heads_per_blk + j
                shift = (QK_NOPE_DIM - j * QK_ROPE_DIM) % LANES
                r = pltpu.roll(qr, shift, 1) if shift else qr
                r = jnp.where(lane < QK_NOPE_DIM + QK_ROPE_DIM, r, 0.0)
                q_ref[0, h_idx, rows, :] = _head_tile(qn, r, h_idx).astype(BF16)
            yield

    _run_staggered(block(rows) for rows in _row_blocks(x_ref.shape[0]))


def _queries(x, bsz, shift, scale, norm_g, w_dq, q_norm_g, w_uq, q_nope_g, q_rope_g, head_sum, head_spread,
             seg32, ct, st):
    t, d = x.shape
    s_len = t // bsz
    tm = ROW_TILE
    tiles = s_len // tm
    rank = w_dq.shape[1]
    half = QK_ROPE_DIM // 2
    wq = w_uq.reshape(rank, N_HEADS, QK_NOPE_DIM + QK_ROPE_DIM)
    wn = wq[:, :, :QK_NOPE_DIM].reshape(rank, -1).astype(BF16)
    wr = wq[:, :, QK_NOPE_DIM:].reshape(rank, -1).astype(BF16)
    wrs = jnp.concatenate([wq[:, :, QK_NOPE_DIM + half:], wq[:, :, QK_NOPE_DIM:QK_NOPE_DIM + half]], axis=2)
    wrs = wrs.reshape(rank, -1).astype(BF16)
    gn = jnp.tile(q_nope_g, N_HEADS).reshape(1, -1)
    gr = jnp.tile(q_rope_g, N_HEADS).reshape(1, -1)
    grs = jnp.tile(jnp.concatenate([q_rope_g[half:], q_rope_g[:half]]), N_HEADS).reshape(1, -1)
    full = _resident
    xspec = pl.BlockSpec((tm, d), lambda b, i: (b * tiles + i, 0))
    tab = pl.BlockSpec((tm, LANES), lambda b, i: (b * tiles + i, 0))
    per_b = pl.BlockSpec((1, 1, d), lambda b, i: (b, 0, 0))
    hspec = pl.BlockSpec((1, N_HEADS, tm, HEAD_LANES), lambda b, i: (b, 0, i, 0))
    args = (x, shift, scale, norm_g.reshape(1, d), w_dq.astype(BF16), q_norm_g.reshape(1, rank), wn, wr, wrs,
            gn, gr, grs, head_sum, head_spread, seg32, ct, st)
    in_specs = [xspec, per_b, per_b] + [full(a) for a in args[3:15]] + [tab, tab]
    return pl.pallas_call(
        _q_kernel,
        grid=(bsz, tiles),
        in_specs=in_specs,
        out_specs=hspec,
        out_shape=jax.ShapeDtypeStruct((bsz, N_HEADS, s_len, HEAD_LANES), BF16),
        compiler_params=_params("parallel", "parallel"),
        name="mla_queries",
    )(*args)


def _chunk_offsets(tile):
    row_chunk = lax.broadcasted_iota(jnp.int32, (tile, tile), 0) // CHUNK
    col_chunk = lax.broadcasted_iota(jnp.int32, (tile, tile), 1) // CHUNK
    return row_chunk - col_chunk


def _merge_head_pair(accs):
    outs = [acc / acc[:, V_HEAD_DIM:V_HEAD_DIM + 1] for acc in accs]
    lane = _lane_iota(outs[0].shape)
    tiles = [jnp.where(lane < V_HEAD_DIM, outs[i], pltpu.roll(outs[i + 1], V_HEAD_DIM, 1))
             for i in range(0, len(outs), 2)]
    return tiles[0] if len(tiles) == 1 else jnp.concatenate(tiles, axis=1)


def _attn_online_kernel(q_ref, k_ref, v_ref, o_ref, m_scr, acc_scr, *, tile):
    i = pl.program_id(2)
    diag_mask = _chunk_offsets(tile) >= 0
    qs = [q_ref[0, hh] for hh in range(HEADS_PER_STEP)]

    def tile_update(off, diagonal):
        for hh in range(HEADS_PER_STEP):
            s = lax.dot_general(qs[hh], k_ref[0, hh, pl.ds(off, tile), :], _NT_DIMS,
                                preferred_element_type=F32)
            if diagonal:
                s = jnp.where(diag_mask, s, MASK_VALUE)
            row_max = jnp.max(s, axis=-1, keepdims=True)
            m = row_max if diagonal else m_scr[hh]
            m_new = row_max if diagonal else jnp.maximum(m, row_max)
            pv = _dot(jnp.exp2(s - m_new).astype(BF16), v_ref[0, hh, pl.ds(off, tile), :])
            acc_scr[hh] = pv if diagonal else acc_scr[hh] * jnp.exp2(m - m_new) + pv
            m_scr[hh] = m_new

    tile_update(pl.multiple_of(i * tile, tile), True)

    def below_diagonal(j, carry):
        tile_update(pl.multiple_of(j * tile, tile), False)
        return carry

    lax.fori_loop(0, i, below_diagonal, 0)
    o_ref[0] = _merge_head_pair([acc_scr[hh] for hh in range(HEADS_PER_STEP)]).astype(o_ref.dtype)


def _attn_bounded_kernel(q_ref, k_ref, v_ref, o_ref, *, tile, n_tiles):
    g = pl.program_id(2)
    lo, hi = g, n_tiles - 1 - g
    diag_mask = _chunk_offsets(tile) >= 0
    zero = jnp.zeros((tile, HEAD_LANES), F32)
    acc_lo = [zero] * HEADS_PER_STEP
    acc_hi = [zero] * HEADS_PER_STEP

    def rows(ref, hh, idx):
        return ref[0, hh, pl.ds(pl.multiple_of(idx * tile, tile), tile), :]

    def weighted_values(q, k, v, mask):
        s = lax.dot_general(q, k, _NT_DIMS, preferred_element_type=F32)
        if mask is not None:
            s = jnp.where(mask, s, MASK_VALUE)
        return _dot(jnp.exp2(s).astype(BF16), v)

    def diagonal(idx, accs):
        for hh in range(HEADS_PER_STEP):
            accs[hh] = accs[hh] + weighted_values(rows(q_ref, hh, idx), rows(k_ref, hh, idx),
                                                  rows(v_ref, hh, idx), diag_mask)

    diagonal(lo, acc_lo)
    diagonal(hi, acc_hi)
    max_lo = n_tiles // 2 - 1
    for slot in range(1, n_tiles):
        owner_known = slot > max_lo
        if owner_known:
            q_idx, k_idx = hi, slot - 1 - lo
        else:
            is_lo = slot <= lo
            q_idx = jnp.where(is_lo, lo, hi)
            k_idx = jnp.where(is_lo, slot - 1, slot - 1 - lo)
        for hh in range(HEADS_PER_STEP):
            pv = weighted_values(rows(q_ref, hh, q_idx), rows(k_ref, hh, k_idx), rows(v_ref, hh, k_idx), None)
            if owner_known:
                acc_hi[hh] = acc_hi[hh] + pv
            else:
                acc_lo[hh] = acc_lo[hh] + jnp.where(is_lo, pv, 0.0)
                acc_hi[hh] = acc_hi[hh] + jnp.where(is_lo, 0.0, pv)
    o_ref[0, pl.ds(pl.multiple_of(lo * tile, tile), tile), :] = _merge_head_pair(acc_lo).astype(o_ref.dtype)
    o_ref[0, pl.ds(pl.multiple_of(hi * tile, tile), tile), :] = _merge_head_pair(acc_hi).astype(o_ref.dtype)


def _score_bound(q_nope_g, q_rope_g, k_nope_g, k_rope_g):
    def sq_len(g_nope, g_rope):
        return QK_NOPE_DIM * jnp.max(g_nope * g_nope) + QK_ROPE_DIM * jnp.max(g_rope * g_rope)
    return Q_SCALE * jnp.sqrt(sq_len(q_nope_g, q_rope_g) * sq_len(k_nope_g, k_rope_g)) * BOUND_SLACK


def _attention(q, k, v, bound):
    return lax.cond(bound <= MAX_UNSHIFTED_BOUND, _attention_bounded, _attention_online, q, k, v)


def _attention_online(q, k, v):
    bsz, nh, s_len, hl = q.shape
    tile = ATTN_TILE
    hp = HEADS_PER_STEP
    whole = pl.BlockSpec((1, hp, s_len, hl), lambda b, h, i: (b, h, 0, 0))
    return pl.pallas_call(
        functools.partial(_attn_online_kernel, tile=tile),
        grid=(bsz, nh // hp, s_len // tile),
        in_specs=[pl.BlockSpec((1, hp, tile, hl), lambda b, h, i: (b, h, i, 0)), whole, whole],
        out_specs=pl.BlockSpec((1, tile, hp * V_HEAD_DIM), lambda b, h, i: (b, i, h)),
        out_shape=jax.ShapeDtypeStruct((bsz, s_len, nh * V_HEAD_DIM), BF16),
        scratch_shapes=[pltpu.VMEM((hp, tile, 1), F32), pltpu.VMEM((hp, tile, hl), F32)],
        compiler_params=_params("parallel", "parallel", "arbitrary"),
        name="mla_attention_online",
    )(q, k, v)


def _attention_bounded(q, k, v):
    bsz, nh, s_len, hl = q.shape
    tile = ATTN_TILE
    hp = HEADS_PER_STEP
    n_tiles = s_len // tile
    whole = pl.BlockSpec((1, hp, s_len, hl), lambda b, h, g: (b, h, 0, 0))
    return pl.pallas_call(
        functools.partial(_attn_bounded_kernel, tile=tile, n_tiles=n_tiles),
        grid=(bsz, nh // hp, n_tiles // 2),
        in_specs=[whole, whole, whole],
        out_specs=pl.BlockSpec((1, s_len, hp * V_HEAD_DIM), lambda b, h, g: (b, 0, h)),
        out_shape=jax.ShapeDtypeStruct((bsz, s_len, nh * V_HEAD_DIM), BF16),
        compiler_params=_params("parallel", "parallel", "arbitrary"),
        name="mla_attention_bounded",
    )(q, k, v)


def _seg_ones(n, seg):
    idx = jnp.arange(n) // seg
    return (idx[:, None] == idx[None, :]).astype(BF16)


def _ffn_weights(w_gate, w_up, w_down):
    return w_gate.astype(BF16), w_up.astype(BF16), w_down.astype(BF16)


def kernel(x, c, positions, ada_w, ada_b, norm1_g, norm2_g, ffn_w_gate, ffn_w_up, ffn_w_down, s5_lam_re, s5_lam_im, s5_log_dt, s5_b_re, s5_b_im, s5_c_re, s5_c_im, s5_d, s5_w_glu, s5_b_glu, kv_ada_w, kv_ada_b, kv_norm_g, w_kv_a, kv_a_norm_g, w_kv_b, k_nope_norm_g, k_rope_norm_g, mla_w_dq, mla_q_norm_g, mla_w_uq, mla_q_nope_norm_g, mla_q_rope_norm_g, mla_w_o):
    bsz, s_len, d = x.shape
    depth = ada_w.shape[0]
    n_a = s5_lam_re.shape[0]
    t = bsz * s_len

    mods = _mods(c, ada_w, ada_b).reshape(depth, bsz, 6, d)
    kv_mods = _mods(c, kv_ada_w[None], kv_ada_b[None]).reshape(bsz, 2, d)
    cos_t, sin_t = _rope_tables(positions)

    a2_re, a2_im, *s5_mats = _s5_discretise(s5_lam_re, s5_lam_im, s5_log_dt, s5_b_re, s5_b_im, s5_c_re, s5_c_im)

    xt = x
    for l in range(n_a):
        m = mods[l]
        bb_re, bb_im, abb_re, abb_im, ca_re, ca_im, cb = (w[l] for w in s5_mats)
        slab_weights = _s5_slab_weights(bb_re, bb_im, abb_re, abb_im, s5_c_re[l], s5_c_im[l], ca_re, ca_im, cb)
        xt = _s5_mixer(xt, m[:, 0], m[:, 1], m[:, 2], norm1_g[l], slab_weights, a2_re[l], a2_im[l], s5_d[l],
                       s5_w_glu[l].astype(BF16), s5_b_glu[l], bsz)
        wg, wu, wd = _ffn_weights(ffn_w_gate[l], ffn_w_up[l], ffn_w_down[l])
        xt = _ffn(xt, m[:, 3], m[:, 4], m[:, 5], norm2_g[l], wg, wu, wd, out_batch_major=l == n_a - 1)
    xb = xt.reshape(t, d)

    head_of_lane = jnp.arange(N_HEADS * QK_NOPE_DIM) // QK_NOPE_DIM
    head_sum = (head_of_lane[:, None] == jnp.arange(LANES)[None, :]).astype(BF16)
    head_spread = jnp.concatenate([head_sum.T, head_sum.T], axis=0)
    seg32 = _seg_ones(N_HEADS * QK_ROPE_DIM, QK_ROPE_DIM)
    per_b = lambda v: v.reshape(bsz, 1, d)
    k_all, v_all = _shared_kv(xb, bsz, per_b(kv_mods[:, 0]), per_b(kv_mods[:, 1]), kv_norm_g, w_kv_a,
                              kv_a_norm_g, w_kv_b, k_nope_norm_g, k_rope_norm_g, head_sum, head_spread,
                              cos_t, sin_t)
    for l in range(n_a, depth):
        j = l - n_a
        m = mods[l]
        bound = _score_bound(mla_q_nope_norm_g[j], mla_q_rope_norm_g[j], k_nope_norm_g, k_rope_norm_g)
        q = _queries(xb, bsz, per_b(m[:, 0]), per_b(m[:, 1]), norm1_g[l], mla_w_dq[j], mla_q_norm_g[j],
                     mla_w_uq[j], mla_q_nope_norm_g[j], mla_q_rope_norm_g[j], head_sum, head_spread, seg32,
                     cos_t, sin_t)
        o = _attention(q, k_all, v_all, bound).reshape(t, N_HEADS * V_HEAD_DIM)
        wg, wu, wd = _ffn_weights(ffn_w_gate[l], ffn_w_up[l], ffn_w_down[l])
        xb = _ffn(xb, per_b(m[:, 3]), per_b(m[:, 4]), per_b(m[:, 5]), norm2_g[l], wg, wu, wd,
                  attn=(o, mla_w_o[j].astype(BF16), per_b(m[:, 2])))
    return xb.reshape(bsz, s_len, d)
```

```python
import functools
import math

import jax
import jax.numpy as jnp
from jax import lax
from jax.experimental import pallas as pl
from jax.experimental.pallas import tpu as pltpu

F32 = jnp.float32
BF16 = jnp.bfloat16

CHUNK = 64
SSM_GROUP = 16
SSM_STATE = 64
N_HEADS = 16
QK_NOPE_DIM = 64
QK_ROPE_DIM = 32
V_HEAD_DIM = 64
ROPE_THETA = 10000.0
ATTN_SCALE = 1.0 / math.sqrt(QK_NOPE_DIM + QK_ROPE_DIM)
Q_SCALE = ATTN_SCALE * math.log2(math.e)
EPS = 1e-6
MASK_VALUE = -1e30
BOUND_SLACK = 1.01
MAX_UNSHIFTED_BOUND = 40.0

LANES = 128
SUBLANES = 8
HEAD_LANES = 128
SLAB_GROUPS = LANES // SSM_GROUP
VMEM_LIMIT = 56 * 1024 * 1024

ROW_TILE = 512
ROW_BLOCKS = 2
S5_TIME_TILE = 64
RELAYOUT_PITCH_PAD = 8
FFN_CHUNK = 256
ATTN_TILE = 512
HEADS_PER_STEP = 4


def _params(*sem):
    return pltpu.CompilerParams(dimension_semantics=sem, vmem_limit_bytes=VMEM_LIMIT)


def _rms(x, g):
    return x * lax.rsqrt(jnp.mean(x * x, axis=-1, keepdims=True) + EPS) * g


def _rowwise(fn, a, *mods):
    r = mods[0].shape[0]
    if r == 1:
        return fn(a, *mods)
    rows, d = a.shape
    out = fn(a.reshape(rows // r, r, d), *[m[None] for m in mods])
    return out.reshape(rows, d)


def _modulate(h, shift, scale):
    return _rowwise(lambda a, sh, sc: a * (1.0 + sc) + sh, h, shift, scale)


def _gated_add(x, gate, upd):
    return x + _rowwise(lambda a, g: a * g, upd, gate)


def _dot(a, b):
    return jnp.dot(a, b, preferred_element_type=F32)


_NT_DIMS = (((1,), (1,)), ((), ()))


def _resident(a):
    zeros = (0,) * a.ndim
    return pl.BlockSpec(a.shape, lambda *_: zeros, pipeline_mode=pl.Buffered(1))


def _lane_iota(shape):
    return lax.broadcasted_iota(jnp.int32, shape, len(shape) - 1)


def _rows_bt_to_tb(x_ref, scr):
    bsz, ts, d = x_ref.shape
    pitch = scr.shape[1] // bsz
    cols = []
    for s in range(d // LANES):
        lanes = slice(s * LANES, (s + 1) * LANES)
        for b in range(bsz):
            scr[s, b * pitch:b * pitch + ts, :] = x_ref[b, :, lanes]
        cols.append(jnp.concatenate([scr[s, pl.ds(t, bsz, stride=pitch), :] for t in range(ts)], axis=0))
    return jnp.concatenate(cols, axis=1)


def _rows_tb_to_bt(val, scr, out_ref):
    bsz, ts, d = out_ref.shape
    for s in range(d // LANES):
        lanes = slice(s * LANES, (s + 1) * LANES)
        scr[s] = val[:, lanes]
        for b in range(bsz):
            out_ref[b, :, lanes] = scr[s, pl.ds(b, ts, stride=bsz), :]


def _row_blocks(n_rows):
    step = n_rows // ROW_BLOCKS
    return [slice(i * step, (i + 1) * step) for i in range(ROW_BLOCKS)]


def _run_staggered(blocks):
    waiting, live = list(blocks), []
    while waiting or live:
        if waiting:
            live.append(waiting.pop(0))
        for gen in list(live):
            if next(gen, StopIteration) is StopIteration:
                live.remove(gen)


def _head_rstd(x, sum_ref, spread_ref, dim):
    ss = _dot((x * x).astype(BF16), sum_ref[...])
    rstd = lax.rsqrt(ss * (1.0 / dim) + EPS)
    hi = rstd.astype(BF16)
    lo = (rstd - hi.astype(F32)).astype(BF16)
    return _dot(jnp.concatenate([hi, lo], axis=1), spread_ref[...])


def _mods_kernel(c_ref, w_ref, b_ref, o_ref):
    c = c_ref[...]
    ca = c * jax.nn.sigmoid(c)
    o_ref[0] = _dot(ca.astype(BF16), w_ref[0].astype(BF16)) + b_ref[0]


def _mods(c, w, b, tn=2048):
    nl, d, n = w.shape
    bsz = c.shape[0]
    return pl.pallas_call(
        _mods_kernel,
        grid=(nl, n // tn),
        in_specs=[pl.BlockSpec((bsz, d), lambda l, j: (0, 0)),
                  pl.BlockSpec((1, d, tn), lambda l, j: (l, 0, j)),
                  pl.BlockSpec((1, 1, tn), lambda l, j: (l, 0, j))],
        out_specs=pl.BlockSpec((1, bsz, tn), lambda l, j: (l, 0, j)),
        out_shape=jax.ShapeDtypeStruct((nl, bsz, n), F32),
        compiler_params=_params("parallel", "parallel"),
        name="adaln_mods",
    )(c, w, b.reshape(nl, 1, n))


def _rope_kernel(pos_ref, inv_ref, spread_ref, cos_ref, sin_ref):
    ang = inv_ref[...] * pos_ref[...]
    spread = functools.partial(lax.dot_general, dimension_numbers=(((0,), (0,)), ((), ())),
                               precision=lax.Precision.HIGHEST, preferred_element_type=F32)
    cos_ref[...] = spread(jnp.cos(ang), spread_ref[...])
    sin_ref[...] = spread(jnp.sin(ang), spread_ref[...])


def _rope_tables(positions, tn=4096):
    t = positions.size
    half = QK_ROPE_DIM // 2
    inv = 1.0 / (ROPE_THETA ** (jnp.arange(0, QK_ROPE_DIM, 2, dtype=F32) / QK_ROPE_DIM))
    pos = positions.astype(F32).reshape(1, t)
    spread = (jnp.arange(LANES)[None, :] % half == jnp.arange(half)[:, None]).astype(F32)
    return pl.pallas_call(
        _rope_kernel,
        grid=(t // tn,),
        in_specs=[pl.BlockSpec((1, tn), lambda i: (0, i)),
                  pl.BlockSpec((half, 1), lambda i: (0, 0)),
                  pl.BlockSpec((half, LANES), lambda i: (0, 0))],
        out_specs=[pl.BlockSpec((tn, LANES), lambda i: (i, 0))] * 2,
        out_shape=[jax.ShapeDtypeStruct((t, LANES), F32)] * 2,
        compiler_params=_params("parallel"),
        name="rope_tables",
    )(pos, inv.reshape(half, 1), spread)


def _rope_sign(shape):
    lane = _lane_iota(shape)
    return jnp.where((lane % QK_ROPE_DIM) < (QK_ROPE_DIM // 2), -1.0, 1.0).astype(F32)


def _cmul(ar, ai, br, bi):
    return ar * br - ai * bi, ar * bi + ai * br


def _s5_disc_kernel(lr_ref, li_ref, ldt_ref, br_ref, bi_ref, cr_ref, ci_ref,
                    a2r_ref, a2i_ref, bbr_ref, bbi_ref, abbr_ref, abbi_ref, car_ref, cai_ref, cb_ref):
    lr = lr_ref[0]
    li = li_ref[0]
    dt = jnp.exp(ldt_ref[0])
    mag = jnp.exp(lr * dt)
    a_re = mag * jnp.cos(li * dt)
    a_im = mag * jnp.sin(li * dt)
    den = lr * lr + li * li
    nr = a_re - 1.0
    ni = a_im
    f_re = (nr * lr + ni * li) / den
    f_im = (ni * lr - nr * li) / den
    bb_re, bb_im = _cmul(f_re, f_im, br_ref[0], bi_ref[0])
    cr = cr_ref[0]
    ci = ci_ref[0]
    a2r_ref[0], a2i_ref[0] = _cmul(a_re, a_im, a_re, a_im)
    bbr_ref[0] = bb_re
    bbi_ref[0] = bb_im
    abbr_ref[0], abbi_ref[0] = _cmul(a_re, a_im, bb_re, bb_im)
    car_ref[0], cai_ref[0] = _cmul(cr, ci, a_re, a_im)
    group_dot = functools.partial(jnp.einsum, "gpn,gqn->gpq", preferred_element_type=F32,
                                  precision=lax.Precision.HIGHEST)
    cb_ref[0] = group_dot(cr, bb_re) - group_dot(ci, bb_im)


def _s5_discretise(lam_re, lam_im, log_dt, b_re, b_im, c_re, c_im):
    na, g, n = lam_re.shape
    p = b_re.shape[-1]
    vec = pl.BlockSpec((1, g, 1, n), lambda l: (l, 0, 0, 0))
    mat = pl.BlockSpec((1, g, p, n), lambda l: (l, 0, 0, 0))
    vec_shape = jax.ShapeDtypeStruct((na, g, 1, n), F32)
    mat_shape = jax.ShapeDtypeStruct((na, g, p, n), F32)
    return pl.pallas_call(
        _s5_disc_kernel,
        grid=(na,),
        in_specs=[vec, vec, pl.BlockSpec((1, g, 1, 1), lambda l: (l, 0, 0, 0)), mat, mat, mat, mat],
        out_specs=[vec, vec] + [mat] * 6 + [pl.BlockSpec((1, g, p, p), lambda l: (l, 0, 0, 0))],
        out_shape=[vec_shape] * 2 + [mat_shape] * 6 + [jax.ShapeDtypeStruct((na, g, p, p), F32)],
        compiler_params=_params("parallel"),
        name="s5_discretise",
    )(lam_re.reshape(na, g, 1, n), lam_im.reshape(na, g, 1, n), log_dt.reshape(na, g, 1, 1),
      b_re.transpose(0, 1, 3, 2), b_im.transpose(0, 1, 3, 2), c_re, c_im)


def _slab_block_diag(w_gab):
    g, a, b = w_gab.shape
    ns = g // SLAB_GROUPS
    w = w_gab.reshape(ns, SLAB_GROUPS, a, b)
    eye = jnp.eye(SLAB_GROUPS, dtype=w.dtype)
    return jnp.einsum("kgab,gh->kgahb", w, eye).reshape(ns, SLAB_GROUPS * a, SLAB_GROUPS * b)


def _s5_slab_weights(bb_re, bb_im, abb_re, abb_im, c_re, c_im, ca_re, ca_im, cb):
    col = lambda re, im: jnp.concatenate([_slab_block_diag(re), _slab_block_diag(im)], axis=2)
    b_pair = jnp.concatenate([col(bb_re, bb_im), col(abb_re, abb_im)], axis=1)
    out = lambda w: _slab_block_diag(w.transpose(0, 2, 1))
    c_pair_re = jnp.concatenate([out(c_re), out(ca_re)], axis=2)
    c_pair_im = jnp.concatenate([out(c_im), out(ca_im)], axis=2)
    cb_blk = _slab_block_diag(cb.transpose(0, 2, 1))
    return tuple(w.astype(BF16) for w in (b_pair, c_pair_re, c_pair_im, cb_blk))


def _s5_mixer_kernel(x_ref, shift_ref, scale_ref, gate_ref, g_ref, bpair_ref, cre_ref, cim_ref, cb_ref,
                     a2r_ref, a2i_ref, d_ref, wglu_ref, bglu_ref, o_ref,
                     bu_re, bu_im, st_re, st_im, tail_scr, yf_scr, ys_scr, *maybe_relayout_scr,
                     ts, n_slabs, slab_state):
    @pl.when(pl.program_id(0) == 0)
    def _():
        st_re[...] = jnp.zeros_like(st_re)
        st_im[...] = jnp.zeros_like(st_im)
        tail_scr[...] = jnp.zeros_like(tail_scr)

    pairs = ts // 2
    pair_rows = pairs * SUBLANES
    x = _rows_bt_to_tb(x_ref, *maybe_relayout_scr) if maybe_relayout_scr else x_ref[...]
    d = x.shape[1]
    h = _modulate(_rms(x, g_ref[...]), shift_ref[...], scale_ref[...])
    h3 = h.reshape(pairs, 2 * SUBLANES, d)
    h_first = h3[:, :SUBLANES].reshape(pair_rows, d).astype(BF16)
    h_second = h3[:, SUBLANES:].reshape(pair_rows, d).astype(BF16)

    def project_in(k):
        lanes = slice(k * LANES, (k + 1) * LANES)
        ssl = slice(k * slab_state, (k + 1) * slab_state)
        bu = _dot(jnp.concatenate([h_second[:, lanes], h_first[:, lanes]], axis=1), bpair_ref[k])
        bu_re[:, ssl] = bu[:, :slab_state]
        bu_im[:, ssl] = bu[:, slab_state:]

    def recur(k):
        ssl = slice(k * slab_state, (k + 1) * slab_state)
        ar = jnp.broadcast_to(a2r_ref[:, ssl], (SUBLANES, slab_state))
        ai = jnp.broadcast_to(a2i_ref[:, ssl], (SUBLANES, slab_state))
        sr, si = st_re[:, ssl], st_im[:, ssl]
        for j in range(pairs):
            rows = slice(j * SUBLANES, (j + 1) * SUBLANES)
            sr, si = ar * sr - ai * si + bu_re[rows, ssl], ar * si + ai * sr + bu_im[rows, ssl]
            bu_re[rows, ssl] = sr
            bu_im[rows, ssl] = si
        st_re[:, ssl] = sr
        st_im[:, ssl] = si

    def project_out(k):
        ssl = slice(k * slab_state, (k + 1) * slab_state)
        lanes = slice(k * LANES, (k + 1) * LANES)
        z = _dot(bu_re[:, ssl].astype(BF16), cre_ref[k]) - _dot(bu_im[:, ssl].astype(BF16), cim_ref[k])
        ys_scr[:, lanes] = z[:, :LANES]
        carried = jnp.concatenate([tail_scr[:, lanes], z[:pair_rows - SUBLANES, LANES:]], axis=0)
        tail_scr[:, lanes] = z[pair_rows - SUBLANES:, LANES:]
        yf_scr[:, lanes] = carried + _dot(h_first[:, lanes], cb_ref[k])

    for k in range(n_slabs + 2):
        if k < n_slabs:
            project_in(k)
        if 0 <= k - 1 < n_slabs:
            recur(k - 1)
        if 0 <= k - 2 < n_slabs:
            project_out(k - 2)

    y = jnp.concatenate([yf_scr[...].reshape(pairs, SUBLANES, d), ys_scr[...].reshape(pairs, SUBLANES, d)],
                        axis=1).reshape(2 * pair_rows, d)
    y = y + d_ref[...] * h
    gl = jax.nn.gelu(y)
    z = _dot(gl.astype(BF16), wglu_ref[...]) + bglu_ref[...]
    mix = gl * jax.nn.sigmoid(z)
    o_ref[...] = _gated_add(x, gate_ref[...], mix)


def _s5_mixer(x, shift, scale, gate, norm_g, slab_weights, a2_re, a2_im, d_skip, w_glu, b_glu, bsz):
    assert bsz == SUBLANES
    ts = S5_TIME_TILE
    rows = ts * bsz
    pair_rows = rows // 2
    if x.ndim == 3:
        d = x.shape[2]
        t = x.shape[0] * x.shape[1]
        x_spec = pl.BlockSpec((bsz, ts, d), lambda i: (0, i, 0))
        relayout_scr = [pltpu.VMEM((d // LANES, bsz * (ts + RELAYOUT_PITCH_PAD), LANES), F32)]
    else:
        t, d = x.shape
        x_spec = pl.BlockSpec((rows, d), lambda i: (i, 0))
        relayout_scr = []
    b_pair, c_pair_re, c_pair_im, cb_blk = slab_weights
    n_slabs, _, two_state = b_pair.shape
    slab_state = two_state // 2
    n_state = n_slabs * slab_state
    full = _resident
    row = lambda a: a.reshape(1, -1)
    args = (x, shift, scale, gate, row(norm_g), b_pair, c_pair_re, c_pair_im, cb_blk, row(a2_re), row(a2_im),
            row(d_skip), w_glu, row(b_glu))
    return pl.pallas_call(
        functools.partial(_s5_mixer_kernel, ts=ts, n_slabs=n_slabs, slab_state=slab_state),
        grid=(t // rows,),
        in_specs=[x_spec] + [full(a) for a in args[1:]],
        out_specs=pl.BlockSpec((rows, d), lambda i: (i, 0)),
        out_shape=jax.ShapeDtypeStruct((t, d), F32),
        scratch_shapes=[pltpu.VMEM((pair_rows, n_state), F32), pltpu.VMEM((pair_rows, n_state), F32),
                        pltpu.VMEM((bsz, n_state), F32), pltpu.VMEM((bsz, n_state), F32),
                        pltpu.VMEM((bsz, d), F32), pltpu.VMEM((pair_rows, d), F32),
                        pltpu.VMEM((pair_rows, d), F32)] + relayout_scr,
        compiler_params=_params("arbitrary"),
        name="s5_mixer",
    )(*args)


def _ffn_kernel(*refs, n_chunks, with_attn, out_batch_major):
    if out_batch_major:
        *refs, relayout_scr = refs
    if with_attn:
        (x_ref, o_in_ref, wo_ref, gate1_ref, shift_ref, scale_ref, gate2_ref, g_ref,
         wg_ref, wu_ref, wd_ref, out_ref, h_scr, acc_scr) = refs
        x = x_ref[...] + gate1_ref[0] * _dot(o_in_ref[...], wo_ref[...])
        shift, scale, gate2 = shift_ref[0], scale_ref[0], gate2_ref[0]
    else:
        (x_ref, shift_ref, scale_ref, gate2_ref, g_ref,
         wg_ref, wu_ref, wd_ref, out_ref, h_scr, acc_scr) = refs
        x = x_ref[...]
        shift, scale, gate2 = shift_ref[...], scale_ref[...], gate2_ref[...]

    h_scr[...] = _modulate(_rms(x, g_ref[...]), shift, scale).astype(BF16)
    acc_scr[...] = jnp.zeros_like(acc_scr)

    hb = h_scr[...]
    for f in range(n_chunks):
        cols = slice(f * FFN_CHUNK, (f + 1) * FFN_CHUNK)
        gt = _dot(hb, wg_ref[0, :, cols])
        up = _dot(hb, wu_ref[0, :, cols])
        act = (gt * jax.nn.sigmoid(gt) * up).astype(BF16)
        acc_scr[...] += _dot(act, wd_ref[0, cols, :])
    out = _gated_add(x, gate2, acc_scr[...])
    if out_batch_major:
        _rows_tb_to_bt(out, relayout_scr, out_ref)
    else:
        out_ref[...] = out


def _ffn(x, shift, scale, gate2, norm_g, weights, layer, attn=None, out_batch_major=False):
    t, d = x.shape
    wg, wu, wd = weights
    n_chunks = wg.shape[2] // FFN_CHUNK
    tm = ROW_TILE
    full = _resident
    of_layer = lambda w: pl.BlockSpec((1,) + w.shape[1:], lambda i: (layer, 0, 0), pipeline_mode=pl.Buffered(1))
    xspec = pl.BlockSpec((tm, d), lambda i: (i, 0))
    g2 = norm_g.reshape(1, d)
    if attn is None:
        args = (x, shift, scale, gate2, g2, wg, wu, wd)
        in_specs = [xspec] + [full(a) for a in args[1:5]] + [of_layer(wg), of_layer(wu), of_layer(wd)]
    else:
        o_in, w_o, gate1 = attn
        tiles_per_batch = (t // gate1.shape[0]) // tm
        per_b = pl.BlockSpec((1, 1, d), lambda i: (i // tiles_per_batch, 0, 0))
        args = (x, o_in, w_o, gate1, shift, scale, gate2, g2, wg, wu, wd)
        in_specs = [xspec, xspec, full(w_o), per_b, per_b, per_b, per_b, full(g2),
                    of_layer(wg), of_layer(wu), of_layer(wd)]
    scratch = [pltpu.VMEM((tm, d), BF16), pltpu.VMEM((tm, d), F32)]
    out_spec, out_shape = xspec, jax.ShapeDtypeStruct((t, d), F32)
    if out_batch_major:
        bsz = shift.shape[0]
        out_spec = pl.BlockSpec((bsz, tm // bsz, d), lambda i: (0, i, 0))
        out_shape = jax.ShapeDtypeStruct((bsz, t // bsz, d), F32)
        scratch.append(pltpu.VMEM((d // LANES, tm, LANES), F32))
    return pl.pallas_call(
        functools.partial(_ffn_kernel, n_chunks=n_chunks, with_attn=attn is not None,
                          out_batch_major=out_batch_major),
        grid=(t // tm,),
        in_specs=in_specs,
        out_specs=out_spec,
        out_shape=out_shape,
        scratch_shapes=scratch,
        compiler_params=_params("parallel"),
        name="ffn_attn_out" if attn is not None else "ffn",
    )(*args)


def _head_tile(nope, rope_at_64, h):
    blk = nope[:, (h // 2) * LANES:(h // 2 + 1) * LANES]
    if h % 2:
        blk = pltpu.roll(blk, QK_NOPE_DIM, 1)
    return jnp.where(_lane_iota(blk.shape) < QK_NOPE_DIM, blk, rope_at_64)


def _kv_kernel(x_ref, shift_ref, scale_ref, g_ref, wa_ref, ga_ref, wkn_ref, wv_ref, gkn_ref,
               gkr_ref, gkrs_ref, sum_ref, spread_ref, ct_ref, st_ref, k_ref, v_ref):
    rank = ga_ref.shape[1]
    ones_col = (_lane_iota((1, LANES)) == V_HEAD_DIM).astype(F32)

    def block(rows):
        x = x_ref[rows, :]
        hb = _modulate(_rms(x, g_ref[...]), shift_ref[0], scale_ref[0]).astype(BF16)
        yield
        kva = _dot(hb, wa_ref[...])
        yield
        cb = _rms(kva[:, :rank], ga_ref[...]).astype(BF16)
        a = kva[:, rank:rank + LANES]
        asw = kva[:, rank + LANES:rank + 2 * LANES]
        rstd = lax.rsqrt(jnp.sum(a * a, axis=-1, keepdims=True) * (1.0 / QK_ROPE_DIM) + EPS)
        sgn = _rope_sign(a.shape)
        kr = rstd * (a * gkr_ref[...] * ct_ref[rows, :] + asw * gkrs_ref[...] * (st_ref[rows, :] * sgn))
        kr64 = pltpu.roll(kr, QK_NOPE_DIM, 1)
        yield
        kn = _dot(cb, wkn_ref[...])
        v = _dot(cb, wv_ref[...])
        yield
        kn = kn * _head_rstd(kn, sum_ref, spread_ref, QK_NOPE_DIM) * gkn_ref[...]
        yield
        for h in range(N_HEADS):
            k_ref[0, h, rows, :] = _head_tile(kn, kr64, h).astype(BF16)
            v_ref[0, h, rows, :] = (v[:, h * LANES:(h + 1) * LANES] + ones_col).astype(BF16)

    _run_staggered(block(rows) for rows in _row_blocks(x_ref.shape[0]))


def _shared_kv(x, bsz, k_shift, k_scale, kv_norm_g, w_kv_a, kv_a_norm_g, w_kv_b, k_nope_g, k_rope_g,
               head_sum, head_spread, ct, st):
    t, d = x.shape
    s_len = t // bsz
    tm = ROW_TILE
    tiles = s_len // tm
    rank = kv_a_norm_g.shape[0]
    half = QK_ROPE_DIM // 2
    pad = jnp.zeros((d, LANES - QK_ROPE_DIM), F32)
    w_rope = w_kv_a[:, rank:]
    w_rope_sw = jnp.concatenate([w_rope[:, half:], w_rope[:, :half]], axis=1)
    wa = jnp.concatenate([w_kv_a[:, :rank], w_rope, pad, w_rope_sw, pad], axis=1).astype(BF16)
    wkv = w_kv_b.reshape(rank, N_HEADS, QK_NOPE_DIM + V_HEAD_DIM)
    wkn = wkv[:, :, :QK_NOPE_DIM].reshape(rank, N_HEADS * QK_NOPE_DIM).astype(BF16)
    wv = jnp.pad(wkv[:, :, QK_NOPE_DIM:], ((0, 0), (0, 0), (0, HEAD_LANES - V_HEAD_DIM)))
    wv = wv.reshape(rank, N_HEADS * HEAD_LANES).astype(BF16)
    gkn = jnp.tile(k_nope_g, N_HEADS).reshape(1, -1)
    lane_pad = lambda g: jnp.pad(g, (0, LANES - g.shape[0])).reshape(1, LANES)
    gkr = lane_pad(k_rope_g)
    gkrs = lane_pad(jnp.concatenate([k_rope_g[half:], k_rope_g[:half]]))
    full = _resident
    xspec = pl.BlockSpec((tm, d), lambda b, i: (b * tiles + i, 0))
    tab = pl.BlockSpec((tm, LANES), lambda b, i: (b * tiles + i, 0))
    per_b = pl.BlockSpec((1, 1, d), lambda b, i: (b, 0, 0))
    hspec = pl.BlockSpec((1, N_HEADS, tm, HEAD_LANES), lambda b, i: (b, 0, i, 0))
    args = (x, k_shift, k_scale, kv_norm_g.reshape(1, d), wa, kv_a_norm_g.reshape(1, rank), wkn, wv,
            gkn, gkr, gkrs, head_sum, head_spread, ct, st)
    in_specs = [xspec, per_b, per_b] + [full(a) for a in args[3:13]] + [tab, tab]
    return pl.pallas_call(
        _kv_kernel,
        grid=(bsz, tiles),
        in_specs=in_specs,
        out_specs=[hspec, hspec],
        out_shape=[jax.ShapeDtypeStruct((bsz, N_HEADS, s_len, HEAD_LANES), BF16)] * 2,
        compiler_params=_params("parallel", "parallel"),
        name="mla_shared_kv",
    )(*args)


def _q_kernel(x_ref, shift_ref, scale_ref, g_ref, wdq_ref, gq_ref, wn_ref, wr_ref, wrs_ref,
              gn_ref, gr_ref, grs_ref, sum_ref, spread_ref, seg32_ref, ct_ref, st_ref, q_ref):
    heads_per_blk = LANES // QK_ROPE_DIM

    def block(rows):
        x = x_ref[rows, :]
        hb = _modulate(_rms(x, g_ref[...]), shift_ref[0], scale_ref[0]).astype(BF16)
        yield
        cq = _dot(hb, wdq_ref[...])
        yield
        cq = _rms(cq, gq_ref[...]).astype(BF16)
        yield
        qn = _dot(cq, wn_ref[...])
        a = _dot(cq, wr_ref[...])
        asw = _dot(cq, wrs_ref[...])
        yield
        qn = qn * _head_rstd(qn, sum_ref, spread_ref, QK_NOPE_DIM) * (gn_ref[...] * Q_SCALE)
        ssr = _dot((a * a).astype(BF16), seg32_ref[...])
        yield
        rstd = lax.rsqrt(ssr * (1.0 / QK_ROPE_DIM) + EPS) * Q_SCALE
        ct = ct_ref[rows, :]
        st = st_ref[rows, :] * _rope_sign(ct.shape)
        for blk in range(N_HEADS // heads_per_blk):
            sl = slice(blk * LANES, (blk + 1) * LANES)
            qr = rstd[:, sl] * (a[:, sl] * gr_ref[:, sl] * ct + asw[:, sl] * grs_ref[:, sl] * st)
            lane = _lane_iota(qr.shape)
            for j in range(heads_per_blk):
                h_idx = blk * heads_per_blk + j
                shift = (QK_NOPE_DIM - j * QK_ROPE_DIM) % LANES
                r = pltpu.roll(qr, shift, 1) if shift else qr
                r = jnp.where(lane < QK_NOPE_DIM + QK_ROPE_DIM, r, 0.0)
                q_ref[0, h_idx, rows, :] = _head_tile(qn, r, h_idx).astype(BF16)
            yield

    _run_staggered(block(rows) for rows in _row_blocks(x_ref.shape[0]))


def _queries(x, bsz, shift, scale, norm_g, w_dq, q_norm_g, w_uq, q_nope_g, q_rope_g, head_sum, head_spread,
             seg32, ct, st):
    t, d = x.shape
    s_len = t // bsz
    tm = ROW_TILE
    tiles = s_len // tm
    rank = w_dq.shape[1]
    half = QK_ROPE_DIM // 2
    wq = w_uq.reshape(rank, N_HEADS, QK_NOPE_DIM + QK_ROPE_DIM)
    wn = wq[:, :, :QK_NOPE_DIM].reshape(rank, -1).astype(BF16)
    wr = wq[:, :, QK_NOPE_DIM:].reshape(rank, -1).astype(BF16)
    wrs = jnp.concatenate([wq[:, :, QK_NOPE_DIM + half:], wq[:, :, QK_NOPE_DIM:QK_NOPE_DIM + half]], axis=2)
    wrs = wrs.reshape(rank, -1).astype(BF16)
    gn = jnp.tile(q_nope_g, N_HEADS).reshape(1, -1)
    gr = jnp.tile(q_rope_g, N_HEADS).reshape(1, -1)
    grs = jnp.tile(jnp.concatenate([q_rope_g[half:], q_rope_g[:half]]), N_HEADS).reshape(1, -1)
    full = _resident
    xspec = pl.BlockSpec((tm, d), lambda b, i: (b * tiles + i, 0))
    tab = pl.BlockSpec((tm, LANES), lambda b, i: (b * tiles + i, 0))
    per_b = pl.BlockSpec((1, 1, d), lambda b, i: (b, 0, 0))
    hspec = pl.BlockSpec((1, N_HEADS, tm, HEAD_LANES), lambda b, i: (b, 0, i, 0))
    args = (x, shift, scale, norm_g.reshape(1, d), w_dq.astype(BF16), q_norm_g.reshape(1, rank), wn, wr, wrs,
            gn, gr, grs, head_sum, head_spread, seg32, ct, st)
    in_specs = [xspec, per_b, per_b] + [full(a) for a in args[3:15]] + [tab, tab]
    return pl.pallas_call(
        _q_kernel,
        grid=(bsz, tiles),
        in_specs=in_specs,
        out_specs=hspec,
        out_shape=jax.ShapeDtypeStruct((bsz, N_HEADS, s_len, HEAD_LANES), BF16),
        compiler_params=_params("parallel", "parallel"),
        name="mla_queries",
    )(*args)


def _chunk_offsets(tile):
    row_chunk = lax.broadcasted_iota(jnp.int32, (tile, tile), 0) // CHUNK
    col_chunk = lax.broadcasted_iota(jnp.int32, (tile, tile), 1) // CHUNK
    return row_chunk - col_chunk


def _merge_head_pair(accs):
    outs = [acc / acc[:, V_HEAD_DIM:V_HEAD_DIM + 1] for acc in accs]
    lane = _lane_iota(outs[0].shape)
    tiles = [jnp.where(lane < V_HEAD_DIM, outs[i], pltpu.roll(outs[i + 1], V_HEAD_DIM, 1))
             for i in range(0, len(outs), 2)]
    return tiles[0] if len(tiles) == 1 else jnp.concatenate(tiles, axis=1)


def _attn_online_kernel(q_ref, k_ref, v_ref, o_ref, m_scr, acc_scr, *, tile):
    i = pl.program_id(2)
    diag_mask = _chunk_offsets(tile) >= 0
    qs = [q_ref[0, hh] for hh in range(HEADS_PER_STEP)]

    def tile_update(off, diagonal):
        for hh in range(HEADS_PER_STEP):
            s = lax.dot_general(qs[hh], k_ref[0, hh, pl.ds(off, tile), :], _NT_DIMS,
                                preferred_element_type=F32)
            if diagonal:
                s = jnp.where(diag_mask, s, MASK_VALUE)
            row_max = jnp.max(s, axis=-1, keepdims=True)
            m = row_max if diagonal else m_scr[hh]
            m_new = row_max if diagonal else jnp.maximum(m, row_max)
            pv = _dot(jnp.exp2(s - m_new).astype(BF16), v_ref[0, hh, pl.ds(off, tile), :])
            acc_scr[hh] = pv if diagonal else acc_scr[hh] * jnp.exp2(m - m_new) + pv
            m_scr[hh] = m_new

    tile_update(pl.multiple_of(i * tile, tile), True)

    def below_diagonal(j, carry):
        tile_update(pl.multiple_of(j * tile, tile), False)
        return carry

    lax.fori_loop(0, i, below_diagonal, 0)
    o_ref[0] = _merge_head_pair([acc_scr[hh] for hh in range(HEADS_PER_STEP)]).astype(o_ref.dtype)


def _attn_bounded_kernel(q_ref, k_ref, v_ref, o_ref, *, tile, n_tiles):
    g = pl.program_id(2)
    lo, hi = g, n_tiles - 1 - g
    diag_mask = _chunk_offsets(tile) >= 0
    zero = jnp.zeros((tile, HEAD_LANES), F32)
    acc_lo = [zero] * HEADS_PER_STEP
    acc_hi = [zero] * HEADS_PER_STEP

    def rows(ref, hh, idx):
        return ref[0, hh, pl.ds(pl.multiple_of(idx * tile, tile), tile), :]

    def weighted_values(q, k, v, mask):
        s = lax.dot_general(q, k, _NT_DIMS, preferred_element_type=F32)
        if mask is not None:
            s = jnp.where(mask, s, MASK_VALUE)
        return _dot(jnp.exp2(s).astype(BF16), v)

    def diagonal(idx, accs):
        for hh in range(HEADS_PER_STEP):
            accs[hh] = accs[hh] + weighted_values(rows(q_ref, hh, idx), rows(k_ref, hh, idx),
                                                  rows(v_ref, hh, idx), diag_mask)

    diagonal(lo, acc_lo)
    diagonal(hi, acc_hi)
    max_lo = n_tiles // 2 - 1
    for slot in range(1, n_tiles):
        owner_known = slot > max_lo
        if owner_known:
            q_idx, k_idx = hi, slot - 1 - lo
        else:
            is_lo = slot <= lo
            q_idx = jnp.where(is_lo, lo, hi)
            k_idx = jnp.where(is_lo, slot - 1, slot - 1 - lo)
        for hh in range(HEADS_PER_STEP):
            pv = weighted_values(rows(q_ref, hh, q_idx), rows(k_ref, hh, k_idx), rows(v_ref, hh, k_idx), None)
            if owner_known:
                acc_hi[hh] = acc_hi[hh] + pv
            else:
                acc_lo[hh] = acc_lo[hh] + jnp.where(is_lo, pv, 0.0)
                acc_hi[hh] = acc_hi[hh] + jnp.where(is_lo, 0.0, pv)
    o_ref[0, pl.ds(pl.multiple_of(lo * tile, tile), tile), :] = _merge_head_pair(acc_lo).astype(o_ref.dtype)
    o_ref[0, pl.ds(pl.multiple_of(hi * tile, tile), tile), :] = _merge_head_pair(acc_hi).astype(o_ref.dtype)


def _score_bound(q_nope_g, q_rope_g, k_nope_g, k_rope_g):
    def sq_len(g_nope, g_rope):
        return QK_NOPE_DIM * jnp.max(g_nope * g_nope) + QK_ROPE_DIM * jnp.max(g_rope * g_rope)
    return Q_SCALE * jnp.sqrt(sq_len(q_nope_g, q_rope_g) * sq_len(k_nope_g, k_rope_g)) * BOUND_SLACK


def _attention(q, k, v, bound):
    return lax.cond(bound <= MAX_UNSHIFTED_BOUND, _attention_bounded, _attention_online, q, k, v)


def _attention_online(q, k, v):
    bsz, nh, s_len, hl = q.shape
    tile = ATTN_TILE
    hp = HEADS_PER_STEP
    whole = pl.BlockSpec((1, hp, s_len, hl), lambda b, h, i: (b, h, 0, 0))
    return pl.pallas_call(
        functools.partial(_attn_online_kernel, tile=tile),
        grid=(bsz, nh // hp, s_len // tile),
        in_specs=[pl.BlockSpec((1, hp, tile, hl), lambda b, h, i: (b, h, i, 0)), whole, whole],
        out_specs=pl.BlockSpec((1, tile, hp * V_HEAD_DIM), lambda b, h, i: (b, i, h)),
        out_shape=jax.ShapeDtypeStruct((bsz, s_len, nh * V_HEAD_DIM), BF16),
        scratch_shapes=[pltpu.VMEM((hp, tile, 1), F32), pltpu.VMEM((hp, tile, hl), F32)],
        compiler_params=_params("parallel", "parallel", "arbitrary"),
        name="mla_attention_online",
    )(q, k, v)


def _attention_bounded(q, k, v):
    bsz, nh, s_len, hl = q.shape
    tile = ATTN_TILE
    hp = HEADS_PER_STEP
    n_tiles = s_len // tile
    assert n_tiles % 2 == 0
    whole = pl.BlockSpec((1, hp, s_len, hl), lambda b, h, g: (b, h, 0, 0))
    return pl.pallas_call(
        functools.partial(_attn_bounded_kernel, tile=tile, n_tiles=n_tiles),
        grid=(bsz, nh // hp, n_tiles // 2),
        in_specs=[whole, whole, whole],
        out_specs=pl.BlockSpec((1, s_len, hp * V_HEAD_DIM), lambda b, h, g: (b, 0, h)),
        out_shape=jax.ShapeDtypeStruct((bsz, s_len, nh * V_HEAD_DIM), BF16),
        compiler_params=_params("parallel", "parallel", "arbitrary"),
        name="mla_attention_bounded",
    )(q, k, v)


def _seg_ones(n, seg):
    idx = jnp.arange(n) // seg
    return (idx[:, None] == idx[None, :]).astype(BF16)


def kernel(x, c, positions, ada_w, ada_b, norm1_g, norm2_g, ffn_w_gate, ffn_w_up, ffn_w_down, s5_lam_re, s5_lam_im, s5_log_dt, s5_b_re, s5_b_im, s5_c_re, s5_c_im, s5_d, s5_w_glu, s5_b_glu, kv_ada_w, kv_ada_b, kv_norm_g, w_kv_a, kv_a_norm_g, w_kv_b, k_nope_norm_g, k_rope_norm_g, mla_w_dq, mla_q_norm_g, mla_w_uq, mla_q_nope_norm_g, mla_q_rope_norm_g, mla_w_o):
    bsz, s_len, d = x.shape
    depth = ada_w.shape[0]
    n_a = s5_lam_re.shape[0]
    t = bsz * s_len

    mods = _mods(c, ada_w, ada_b).reshape(depth, bsz, 6, d)
    kv_mods = _mods(c, kv_ada_w[None], kv_ada_b[None]).reshape(bsz, 2, d)
    cos_t, sin_t = _rope_tables(positions)
    ffn_w = tuple(w.astype(BF16) for w in (ffn_w_gate, ffn_w_up, ffn_w_down))

    a2_re, a2_im, *s5_mats = _s5_discretise(s5_lam_re, s5_lam_im, s5_log_dt, s5_b_re, s5_b_im, s5_c_re, s5_c_im)

    xt = x
    for l in range(n_a):
        m = mods[l]
        bb_re, bb_im, abb_re, abb_im, ca_re, ca_im, cb = (w[l] for w in s5_mats)
        slab_weights = _s5_slab_weights(bb_re, bb_im, abb_re, abb_im, s5_c_re[l], s5_c_im[l], ca_re, ca_im, cb)
        xt = _s5_mixer(xt, m[:, 0], m[:, 1], m[:, 2], norm1_g[l], slab_weights, a2_re[l], a2_im[l], s5_d[l],
                       s5_w_glu[l].astype(BF16), s5_b_glu[l], bsz)
        xt = _ffn(xt, m[:, 3], m[:, 4], m[:, 5], norm2_g[l], ffn_w, l, out_batch_major=l == n_a - 1)
    xb = xt.reshape(t, d)

    head_of_lane = jnp.arange(N_HEADS * QK_NOPE_DIM) // QK_NOPE_DIM
    head_sum = (head_of_lane[:, None] == jnp.arange(LANES)[None, :]).astype(BF16)
    head_spread = jnp.concatenate([head_sum.T, head_sum.T], axis=0)
    seg32 = _seg_ones(N_HEADS * QK_ROPE_DIM, QK_ROPE_DIM)
    per_b = lambda v: v.reshape(bsz, 1, d)
    k_all, v_all = _shared_kv(xb, bsz, per_b(kv_mods[:, 0]), per_b(kv_mods[:, 1]), kv_norm_g, w_kv_a,
                              kv_a_norm_g, w_kv_b, k_nope_norm_g, k_rope_norm_g, head_sum, head_spread,
                              cos_t, sin_t)
    for l in range(n_a, depth):
        j = l - n_a
        m = mods[l]
        bound = _score_bound(mla_q_nope_norm_g[j], mla_q_rope_norm_g[j], k_nope_norm_g, k_rope_norm_g)
        q = _queries(xb, bsz, per_b(m[:, 0]), per_b(m[:, 1]), norm1_g[l], mla_w_dq[j], mla_q_norm_g[j],
                     mla_w_uq[j], mla_q_nope_norm_g[j], mla_q_rope_norm_g[j], head_sum, head_spread, seg32,
                     cos_t, sin_t)
        o = _attention(q, k_all, v_all, bound).reshape(t, N_HEADS * V_HEAD_DIM)
        xb = _ffn(xb, per_b(m[:, 3]), per_b(m[:, 4]), per_b(m[:, 5]), norm2_g[l], ffn_w, l,
                  attn=(o, mla_w_o[j].astype(BF16), per_b(m[:, 2])))
    return xb.reshape(bsz, s_len, d)
```

```python
import functools
import math

import jax
import jax.numpy as jnp
from jax import lax
from jax.experimental import pallas as pl
from jax.experimental.pallas import tpu as pltpu

F32 = jnp.float32
BF16 = jnp.bfloat16

CHUNK = 64
SSM_GROUP = 16
SSM_STATE = 64
N_HEADS = 16
QK_NOPE_DIM = 64
QK_ROPE_DIM = 32
V_HEAD_DIM = 64
ROPE_THETA = 10000.0
ATTN_SCALE = 1.0 / math.sqrt(QK_NOPE_DIM + QK_ROPE_DIM)
Q_SCALE = ATTN_SCALE * math.log2(math.e)
EPS = 1e-6
MASK_VALUE = -1e30
BOUND_SLACK = 1.01
MAX_UNSHIFTED_BOUND = 40.0

LANES = 128
SUBLANES = 8
HEAD_LANES = 128
SLAB_GROUPS = LANES // SSM_GROUP
VMEM_LIMIT = 56 * 1024 * 1024

ROW_TILE = 512
PROJ_ROW_TILE = 1024
ROW_BLOCKS = 2
S5_TIME_TILE = 64
RELAYOUT_PITCH_PAD = 8
FFN_CHUNK = 256
ATTN_TILE = 512
HEADS_PER_STEP = 4


def _params(*sem):
    return pltpu.CompilerParams(dimension_semantics=sem, vmem_limit_bytes=VMEM_LIMIT)


def _rms(x, g):
    return x * lax.rsqrt(jnp.mean(x * x, axis=-1, keepdims=True) + EPS) * g


def _rowwise(fn, a, *mods):
    r = mods[0].shape[0]
    if r == 1:
        return fn(a, *mods)
    rows, d = a.shape
    out = fn(a.reshape(rows // r, r, d), *[m[None] for m in mods])
    return out.reshape(rows, d)


def _modulate(h, shift, scale):
    return _rowwise(lambda a, sh, sc: a * (1.0 + sc) + sh, h, shift, scale)


def _gated_add(x, gate, upd):
    return x + _rowwise(lambda a, g: a * g, upd, gate)


def _dot(a, b):
    return jnp.dot(a, b, preferred_element_type=F32)


_NT_DIMS = (((1,), (1,)), ((), ()))


def _resident(a):
    zeros = (0,) * a.ndim
    return pl.BlockSpec(a.shape, lambda *_: zeros, pipeline_mode=pl.Buffered(1))


def _lane_iota(shape):
    return lax.broadcasted_iota(jnp.int32, shape, len(shape) - 1)


def _rows_bt_to_tb(x_ref, scr):
    bsz, ts, d = x_ref.shape
    pitch = scr.shape[1] // bsz
    cols = []
    for s in range(d // LANES):
        lanes = slice(s * LANES, (s + 1) * LANES)
        for b in range(bsz):
            scr[s, b * pitch:b * pitch + ts, :] = x_ref[b, :, lanes]
        cols.append(jnp.concatenate([scr[s, pl.ds(t, bsz, stride=pitch), :] for t in range(ts)], axis=0))
    return jnp.concatenate(cols, axis=1)


def _rows_tb_to_bt(val, scr, out_ref):
    bsz, ts, d = out_ref.shape
    for s in range(d // LANES):
        lanes = slice(s * LANES, (s + 1) * LANES)
        scr[s] = val[:, lanes]
        for b in range(bsz):
            out_ref[b, :, lanes] = scr[s, pl.ds(b, ts, stride=bsz), :]


def _row_blocks(n_rows):
    step = n_rows // ROW_BLOCKS
    return [slice(i * step, (i + 1) * step) for i in range(ROW_BLOCKS)]


def _run_staggered(blocks):
    waiting, live = list(blocks), []
    while waiting or live:
        if waiting:
            live.append(waiting.pop(0))
        for gen in list(live):
            if next(gen, StopIteration) is StopIteration:
                live.remove(gen)


def _head_rstd(x, sum_ref, spread_ref, dim):
    ss = _dot((x * x).astype(BF16), sum_ref[...])
    rstd = lax.rsqrt(ss * (1.0 / dim) + EPS)
    hi = rstd.astype(BF16)
    lo = (rstd - hi.astype(F32)).astype(BF16)
    return _dot(jnp.concatenate([hi, lo], axis=1), spread_ref[...])


def _mods_kernel(c_ref, w_ref, b_ref, o_ref):
    c = c_ref[...]
    ca = c * jax.nn.sigmoid(c)
    o_ref[0] = _dot(ca.astype(BF16), w_ref[0].astype(BF16)) + b_ref[0]


def _mods(c, w, b, tn=2048):
    nl, d, n = w.shape
    bsz = c.shape[0]
    return pl.pallas_call(
        _mods_kernel,
        grid=(nl, n // tn),
        in_specs=[pl.BlockSpec((bsz, d), lambda l, j: (0, 0)),
                  pl.BlockSpec((1, d, tn), lambda l, j: (l, 0, j)),
                  pl.BlockSpec((1, 1, tn), lambda l, j: (l, 0, j))],
        out_specs=pl.BlockSpec((1, bsz, tn), lambda l, j: (l, 0, j)),
        out_shape=jax.ShapeDtypeStruct((nl, bsz, n), F32),
        compiler_params=_params("parallel", "parallel"),
        name="adaln_mods",
    )(c, w, b.reshape(nl, 1, n))


def _rope_kernel(pos_ref, inv_ref, spread_ref, cos_ref, sin_ref):
    ang = inv_ref[...] * pos_ref[...]
    spread = functools.partial(lax.dot_general, dimension_numbers=(((0,), (0,)), ((), ())),
                               precision=lax.Precision.HIGHEST, preferred_element_type=F32)
    cos_ref[...] = spread(jnp.cos(ang), spread_ref[...])
    sin_ref[...] = spread(jnp.sin(ang), spread_ref[...])


def _rope_tables(positions, tn=4096):
    t = positions.size
    half = QK_ROPE_DIM // 2
    inv = 1.0 / (ROPE_THETA ** (jnp.arange(0, QK_ROPE_DIM, 2, dtype=F32) / QK_ROPE_DIM))
    pos = positions.astype(F32).reshape(1, t)
    spread = (jnp.arange(LANES)[None, :] % half == jnp.arange(half)[:, None]).astype(F32)
    return pl.pallas_call(
        _rope_kernel,
        grid=(t // tn,),
        in_specs=[pl.BlockSpec((1, tn), lambda i: (0, i)),
                  pl.BlockSpec((half, 1), lambda i: (0, 0)),
                  pl.BlockSpec((half, LANES), lambda i: (0, 0))],
        out_specs=[pl.BlockSpec((tn, LANES), lambda i: (i, 0))] * 2,
        out_shape=[jax.ShapeDtypeStruct((t, LANES), F32)] * 2,
        compiler_params=_params("parallel"),
        name="rope_tables",
    )(pos, inv.reshape(half, 1), spread)


def _rope_sign(shape):
    lane = _lane_iota(shape)
    return jnp.where((lane % QK_ROPE_DIM) < (QK_ROPE_DIM // 2), -1.0, 1.0).astype(F32)


def _cmul(ar, ai, br, bi):
    return ar * br - ai * bi, ar * bi + ai * br


def _s5_disc_kernel(lr_ref, li_ref, ldt_ref, br_ref, bi_ref, cr_ref, ci_ref,
                    a2r_ref, a2i_ref, bbr_ref, bbi_ref, abbr_ref, abbi_ref, car_ref, cai_ref, cb_ref):
    lr = lr_ref[0]
    li = li_ref[0]
    dt = jnp.exp(ldt_ref[0])
    mag = jnp.exp(lr * dt)
    a_re = mag * jnp.cos(li * dt)
    a_im = mag * jnp.sin(li * dt)
    den = lr * lr + li * li
    nr = a_re - 1.0
    ni = a_im
    f_re = (nr * lr + ni * li) / den
    f_im = (ni * lr - nr * li) / den
    bb_re, bb_im = _cmul(f_re, f_im, br_ref[0], bi_ref[0])
    cr = cr_ref[0]
    ci = ci_ref[0]
    a2r_ref[0], a2i_ref[0] = _cmul(a_re, a_im, a_re, a_im)
    bbr_ref[0] = bb_re
    bbi_ref[0] = bb_im
    abbr_ref[0], abbi_ref[0] = _cmul(a_re, a_im, bb_re, bb_im)
    car_ref[0], cai_ref[0] = _cmul(cr, ci, a_re, a_im)
    group_dot = functools.partial(jnp.einsum, "gpn,gqn->gpq", preferred_element_type=F32,
                                  precision=lax.Precision.HIGHEST)
    cb_ref[0] = group_dot(cr, bb_re) - group_dot(ci, bb_im)


def _s5_discretise(lam_re, lam_im, log_dt, b_re, b_im, c_re, c_im):
    na, g, n = lam_re.shape
    p = b_re.shape[-1]
    vec = pl.BlockSpec((1, g, 1, n), lambda l: (l, 0, 0, 0))
    mat = pl.BlockSpec((1, g, p, n), lambda l: (l, 0, 0, 0))
    vec_shape = jax.ShapeDtypeStruct((na, g, 1, n), F32)
    mat_shape = jax.ShapeDtypeStruct((na, g, p, n), F32)
    return pl.pallas_call(
        _s5_disc_kernel,
        grid=(na,),
        in_specs=[vec, vec, pl.BlockSpec((1, g, 1, 1), lambda l: (l, 0, 0, 0)), mat, mat, mat, mat],
        out_specs=[vec, vec] + [mat] * 6 + [pl.BlockSpec((1, g, p, p), lambda l: (l, 0, 0, 0))],
        out_shape=[vec_shape] * 2 + [mat_shape] * 6 + [jax.ShapeDtypeStruct((na, g, p, p), F32)],
        compiler_params=_params("parallel"),
        name="s5_discretise",
    )(lam_re.reshape(na, g, 1, n), lam_im.reshape(na, g, 1, n), log_dt.reshape(na, g, 1, 1),
      b_re.transpose(0, 1, 3, 2), b_im.transpose(0, 1, 3, 2), c_re, c_im)


def _slab_block_diag(w_gab):
    g, a, b = w_gab.shape
    ns = g // SLAB_GROUPS
    w = w_gab.reshape(ns, SLAB_GROUPS, a, b)
    eye = jnp.eye(SLAB_GROUPS, dtype=w.dtype)
    return jnp.einsum("kgab,gh->kgahb", w, eye).reshape(ns, SLAB_GROUPS * a, SLAB_GROUPS * b)


def _s5_slab_weights(bb_re, bb_im, abb_re, abb_im, c_re, c_im, ca_re, ca_im, cb):
    col = lambda re, im: jnp.concatenate([_slab_block_diag(re), _slab_block_diag(im)], axis=2)
    b_pair = jnp.concatenate([col(bb_re, bb_im), col(abb_re, abb_im)], axis=1)
    out = lambda w: _slab_block_diag(w.transpose(0, 2, 1))
    c_pair_re = jnp.concatenate([out(c_re), out(ca_re)], axis=2)
    c_pair_im = jnp.concatenate([out(c_im), out(ca_im)], axis=2)
    cb_blk = _slab_block_diag(cb.transpose(0, 2, 1))
    return tuple(w.astype(BF16) for w in (b_pair, c_pair_re, c_pair_im, cb_blk))


def _s5_mixer_kernel(x_ref, shift_ref, scale_ref, gate_ref, g_ref, bpair_ref, cre_ref, cim_ref, cb_ref,
                     a2r_ref, a2i_ref, d_ref, wglu_ref, bglu_ref, o_ref,
                     bu_re, bu_im, st_re, st_im, tail_scr, yf_scr, ys_scr, *maybe_relayout_scr,
                     ts, n_slabs, slab_state):
    @pl.when(pl.program_id(0) == 0)
    def _():
        st_re[...] = jnp.zeros_like(st_re)
        st_im[...] = jnp.zeros_like(st_im)
        tail_scr[...] = jnp.zeros_like(tail_scr)

    pairs = ts // 2
    pair_rows = pairs * SUBLANES
    x = _rows_bt_to_tb(x_ref, *maybe_relayout_scr) if maybe_relayout_scr else x_ref[...]
    d = x.shape[1]
    h = _modulate(_rms(x, g_ref[...]), shift_ref[...], scale_ref[...])
    h3 = h.reshape(pairs, 2 * SUBLANES, d)
    h_first = h3[:, :SUBLANES].reshape(pair_rows, d).astype(BF16)
    h_second = h3[:, SUBLANES:].reshape(pair_rows, d).astype(BF16)

    def project_in(k):
        lanes = slice(k * LANES, (k + 1) * LANES)
        ssl = slice(k * slab_state, (k + 1) * slab_state)
        bu = _dot(jnp.concatenate([h_second[:, lanes], h_first[:, lanes]], axis=1), bpair_ref[k])
        bu_re[:, ssl] = bu[:, :slab_state]
        bu_im[:, ssl] = bu[:, slab_state:]

    def recur(k):
        ssl = slice(k * slab_state, (k + 1) * slab_state)
        ar = jnp.broadcast_to(a2r_ref[:, ssl], (SUBLANES, slab_state))
        ai = jnp.broadcast_to(a2i_ref[:, ssl], (SUBLANES, slab_state))
        sr, si = st_re[:, ssl], st_im[:, ssl]
        for j in range(pairs):
            rows = slice(j * SUBLANES, (j + 1) * SUBLANES)
            sr, si = ar * sr - ai * si + bu_re[rows, ssl], ar * si + ai * sr + bu_im[rows, ssl]
            bu_re[rows, ssl] = sr
            bu_im[rows, ssl] = si
        st_re[:, ssl] = sr
        st_im[:, ssl] = si

    def project_out(k):
        ssl = slice(k * slab_state, (k + 1) * slab_state)
        lanes = slice(k * LANES, (k + 1) * LANES)
        z = _dot(bu_re[:, ssl].astype(BF16), cre_ref[k]) - _dot(bu_im[:, ssl].astype(BF16), cim_ref[k])
        ys_scr[:, lanes] = z[:, :LANES]
        carried = jnp.concatenate([tail_scr[:, lanes], z[:pair_rows - SUBLANES, LANES:]], axis=0)
        tail_scr[:, lanes] = z[pair_rows - SUBLANES:, LANES:]
        yf_scr[:, lanes] = carried + _dot(h_first[:, lanes], cb_ref[k])

    for k in range(n_slabs + 2):
        if k < n_slabs:
            project_in(k)
        if 0 <= k - 1 < n_slabs:
            recur(k - 1)
        if 0 <= k - 2 < n_slabs:
            project_out(k - 2)

    y = jnp.concatenate([yf_scr[...].reshape(pairs, SUBLANES, d), ys_scr[...].reshape(pairs, SUBLANES, d)],
                        axis=1).reshape(2 * pair_rows, d)
    y = y + d_ref[...] * h
    gl = jax.nn.gelu(y)
    z = _dot(gl.astype(BF16), wglu_ref[...]) + bglu_ref[...]
    mix = gl * jax.nn.sigmoid(z)
    o_ref[...] = _gated_add(x, gate_ref[...], mix)


def _s5_mixer(x, shift, scale, gate, norm_g, slab_weights, a2_re, a2_im, d_skip, w_glu, b_glu, bsz):
    assert bsz == SUBLANES
    ts = S5_TIME_TILE
    rows = ts * bsz
    pair_rows = rows // 2
    if x.ndim == 3:
        d = x.shape[2]
        t = x.shape[0] * x.shape[1]
        x_spec = pl.BlockSpec((bsz, ts, d), lambda i: (0, i, 0))
        relayout_scr = [pltpu.VMEM((d // LANES, bsz * (ts + RELAYOUT_PITCH_PAD), LANES), F32)]
    else:
        t, d = x.shape
        x_spec = pl.BlockSpec((rows, d), lambda i: (i, 0))
        relayout_scr = []
    b_pair, c_pair_re, c_pair_im, cb_blk = slab_weights
    n_slabs, _, two_state = b_pair.shape
    slab_state = two_state // 2
    n_state = n_slabs * slab_state
    full = _resident
    row = lambda a: a.reshape(1, -1)
    args = (x, shift, scale, gate, row(norm_g), b_pair, c_pair_re, c_pair_im, cb_blk, row(a2_re), row(a2_im),
            row(d_skip), w_glu, row(b_glu))
    return pl.pallas_call(
        functools.partial(_s5_mixer_kernel, ts=ts, n_slabs=n_slabs, slab_state=slab_state),
        grid=(t // rows,),
        in_specs=[x_spec] + [full(a) for a in args[1:]],
        out_specs=pl.BlockSpec((rows, d), lambda i: (i, 0)),
        out_shape=jax.ShapeDtypeStruct((t, d), F32),
        scratch_shapes=[pltpu.VMEM((pair_rows, n_state), F32), pltpu.VMEM((pair_rows, n_state), F32),
                        pltpu.VMEM((bsz, n_state), F32), pltpu.VMEM((bsz, n_state), F32),
                        pltpu.VMEM((bsz, d), F32), pltpu.VMEM((pair_rows, d), F32),
                        pltpu.VMEM((pair_rows, d), F32)] + relayout_scr,
        compiler_params=_params("arbitrary"),
        name="s5_mixer",
    )(*args)


def _ffn_kernel(*refs, n_chunks, with_attn, out_batch_major):
    if out_batch_major:
        *refs, relayout_scr = refs
    if with_attn:
        (x_ref, o_in_ref, wo_ref, gate1_ref, shift_ref, scale_ref, gate2_ref, g_ref,
         wg_ref, wu_ref, wd_ref, out_ref, h_scr, acc_scr) = refs
        x = x_ref[...] + gate1_ref[0] * _dot(o_in_ref[...], wo_ref[...])
        shift, scale, gate2 = shift_ref[0], scale_ref[0], gate2_ref[0]
    else:
        (x_ref, shift_ref, scale_ref, gate2_ref, g_ref,
         wg_ref, wu_ref, wd_ref, out_ref, h_scr, acc_scr) = refs
        x = x_ref[...]
        shift, scale, gate2 = shift_ref[...], scale_ref[...], gate2_ref[...]

    h_scr[...] = _modulate(_rms(x, g_ref[...]), shift, scale).astype(BF16)
    acc_scr[...] = jnp.zeros_like(acc_scr)

    hb = h_scr[...]
    for f in range(n_chunks):
        cols = slice(f * FFN_CHUNK, (f + 1) * FFN_CHUNK)
        gt = _dot(hb, wg_ref[0, :, cols])
        up = _dot(hb, wu_ref[0, :, cols])
        act = (gt * jax.nn.sigmoid(gt) * up).astype(BF16)
        acc_scr[...] += _dot(act, wd_ref[0, cols, :])
    out = _gated_add(x, gate2, acc_scr[...])
    if out_batch_major:
        _rows_tb_to_bt(out, relayout_scr, out_ref)
    else:
        out_ref[...] = out


def _ffn(x, shift, scale, gate2, norm_g, weights, layer, attn=None, out_batch_major=False):
    t, d = x.shape
    wg, wu, wd = weights
    n_chunks = wg.shape[2] // FFN_CHUNK
    tm = ROW_TILE
    full = _resident
    of_layer = lambda w: pl.BlockSpec((1,) + w.shape[1:], lambda i: (layer, 0, 0), pipeline_mode=pl.Buffered(1))
    xspec = pl.BlockSpec((tm, d), lambda i: (i, 0))
    g2 = norm_g.reshape(1, d)
    if attn is None:
        args = (x, shift, scale, gate2, g2, wg, wu, wd)
        in_specs = [xspec] + [full(a) for a in args[1:5]] + [of_layer(wg), of_layer(wu), of_layer(wd)]
    else:
        o_in, w_o, gate1 = attn
        tiles_per_batch = (t // gate1.shape[0]) // tm
        per_b = pl.BlockSpec((1, 1, d), lambda i: (i // tiles_per_batch, 0, 0))
        args = (x, o_in, w_o, gate1, shift, scale, gate2, g2, wg, wu, wd)
        in_specs = [xspec, xspec, full(w_o), per_b, per_b, per_b, per_b, full(g2),
                    of_layer(wg), of_layer(wu), of_layer(wd)]
    scratch = [pltpu.VMEM((tm, d), BF16), pltpu.VMEM((tm, d), F32)]
    out_spec, out_shape = xspec, jax.ShapeDtypeStruct((t, d), F32)
    if out_batch_major:
        bsz = shift.shape[0]
        out_spec = pl.BlockSpec((bsz, tm // bsz, d), lambda i: (0, i, 0))
        out_shape = jax.ShapeDtypeStruct((bsz, t // bsz, d), F32)
        scratch.append(pltpu.VMEM((d // LANES, tm, LANES), F32))
    return pl.pallas_call(
        functools.partial(_ffn_kernel, n_chunks=n_chunks, with_attn=attn is not None,
                          out_batch_major=out_batch_major),
        grid=(t // tm,),
        in_specs=in_specs,
        out_specs=out_spec,
        out_shape=out_shape,
        scratch_shapes=scratch,
        compiler_params=_params("parallel"),
        name="ffn_attn_out" if attn is not None else "ffn",
    )(*args)


def _head_tile(nope, rope_at_64, h):
    blk = nope[:, (h // 2) * LANES:(h // 2 + 1) * LANES]
    if h % 2:
        blk = pltpu.roll(blk, QK_NOPE_DIM, 1)
    return jnp.where(_lane_iota(blk.shape) < QK_NOPE_DIM, blk, rope_at_64)


def _kv_kernel(x_ref, shift_ref, scale_ref, g_ref, wa_ref, ga_ref, wkn_ref, wv_ref, gkn_ref,
               gkr_ref, gkrs_ref, sum_ref, spread_ref, ct_ref, st_ref, k_ref, v_ref):
    rank = ga_ref.shape[1]
    ones_col = (_lane_iota((1, LANES)) == V_HEAD_DIM).astype(F32)

    def block(rows):
        x = x_ref[rows, :]
        hb = _modulate(_rms(x, g_ref[...]), shift_ref[0], scale_ref[0]).astype(BF16)
        yield
        kva = _dot(hb, wa_ref[...])
        yield
        cb = _rms(kva[:, :rank], ga_ref[...]).astype(BF16)
        a = kva[:, rank:rank + LANES]
        asw = kva[:, rank + LANES:rank + 2 * LANES]
        rstd = lax.rsqrt(jnp.sum(a * a, axis=-1, keepdims=True) * (1.0 / QK_ROPE_DIM) + EPS)
        sgn = _rope_sign(a.shape)
        kr = rstd * (a * gkr_ref[...] * ct_ref[rows, :] + asw * gkrs_ref[...] * (st_ref[rows, :] * sgn))
        kr64 = pltpu.roll(kr, QK_NOPE_DIM, 1)
        yield
        kn = _dot(cb, wkn_ref[...])
        v = _dot(cb, wv_ref[...])
        yield
        kn = kn * _head_rstd(kn, sum_ref, spread_ref, QK_NOPE_DIM) * gkn_ref[...]
        yield
        for h in range(N_HEADS):
            k_ref[0, h, rows, :] = _head_tile(kn, kr64, h).astype(BF16)
            v_ref[0, h, rows, :] = (v[:, h * LANES:(h + 1) * LANES] + ones_col).astype(BF16)

    _run_staggered(block(rows) for rows in _row_blocks(x_ref.shape[0]))


def _shared_kv(x, bsz, k_shift, k_scale, kv_norm_g, w_kv_a, kv_a_norm_g, w_kv_b, k_nope_g, k_rope_g,
               head_sum, head_spread, ct, st):
    t, d = x.shape
    s_len = t // bsz
    tm = PROJ_ROW_TILE
    tiles = s_len // tm
    rank = kv_a_norm_g.shape[0]
    half = QK_ROPE_DIM // 2
    pad = jnp.zeros((d, LANES - QK_ROPE_DIM), F32)
    w_rope = w_kv_a[:, rank:]
    w_rope_sw = jnp.concatenate([w_rope[:, half:], w_rope[:, :half]], axis=1)
    wa = jnp.concatenate([w_kv_a[:, :rank], w_rope, pad, w_rope_sw, pad], axis=1).astype(BF16)
    wkv = w_kv_b.reshape(rank, N_HEADS, QK_NOPE_DIM + V_HEAD_DIM)
    wkn = wkv[:, :, :QK_NOPE_DIM].reshape(rank, N_HEADS * QK_NOPE_DIM).astype(BF16)
    wv = jnp.pad(wkv[:, :, QK_NOPE_DIM:], ((0, 0), (0, 0), (0, HEAD_LANES - V_HEAD_DIM)))
    wv = wv.reshape(rank, N_HEADS * HEAD_LANES).astype(BF16)
    gkn = jnp.tile(k_nope_g, N_HEADS).reshape(1, -1)
    lane_pad = lambda g: jnp.pad(g, (0, LANES - g.shape[0])).reshape(1, LANES)
    gkr = lane_pad(k_rope_g)
    gkrs = lane_pad(jnp.concatenate([k_rope_g[half:], k_rope_g[:half]]))
    full = _resident
    xspec = pl.BlockSpec((tm, d), lambda b, i: (b * tiles + i, 0))
    tab = pl.BlockSpec((tm, LANES), lambda b, i: (b * tiles + i, 0))
    per_b = pl.BlockSpec((1, 1, d), lambda b, i: (b, 0, 0))
    hspec = pl.BlockSpec((1, N_HEADS, tm, HEAD_LANES), lambda b, i: (b, 0, i, 0))
    args = (x, k_shift, k_scale, kv_norm_g.reshape(1, d), wa, kv_a_norm_g.reshape(1, rank), wkn, wv,
            gkn, gkr, gkrs, head_sum, head_spread, ct, st)
    in_specs = [xspec, per_b, per_b] + [full(a) for a in args[3:13]] + [tab, tab]
    return pl.pallas_call(
        _kv_kernel,
        grid=(bsz, tiles),
        in_specs=in_specs,
        out_specs=[hspec, hspec],
        out_shape=[jax.ShapeDtypeStruct((bsz, N_HEADS, s_len, HEAD_LANES), BF16)] * 2,
        compiler_params=_params("parallel", "parallel"),
        name="mla_shared_kv",
    )(*args)


def _q_kernel(x_ref, shift_ref, scale_ref, g_ref, wdq_ref, gq_ref, wn_ref, wr_ref, wrs_ref,
              gn_ref, gr_ref, grs_ref, sum_ref, spread_ref, seg32_ref, ct_ref, st_ref, q_ref):
    heads_per_blk = LANES // QK_ROPE_DIM

    def block(rows):
        x = x_ref[rows, :]
        hb = _modulate(_rms(x, g_ref[...]), shift_ref[0], scale_ref[0]).astype(BF16)
        yield
        cq = _dot(hb, wdq_ref[...])
        yield
        cq = _rms(cq, gq_ref[...]).astype(BF16)
        yield
        qn = _dot(cq, wn_ref[...])
        a = _dot(cq, wr_ref[...])
        asw = _dot(cq, wrs_ref[...])
        yield
        qn = qn * _head_rstd(qn, sum_ref, spread_ref, QK_NOPE_DIM) * (gn_ref[...] * Q_SCALE)
        ssr = _dot((a * a).astype(BF16), seg32_ref[...])
        yield
        rstd = lax.rsqrt(ssr * (1.0 / QK_ROPE_DIM) + EPS) * Q_SCALE
        ct = ct_ref[rows, :]
        st = st_ref[rows, :] * _rope_sign(ct.shape)
        for blk in range(N_HEADS // heads_per_blk):
            sl = slice(blk * LANES, (blk + 1) * LANES)
            qr = rstd[:, sl] * (a[:, sl] * gr_ref[:, sl] * ct + asw[:, sl] * grs_ref[:, sl] * st)
            lane = _lane_iota(qr.shape)
            for j in range(heads_per_blk):
                h_idx = blk * heads_per_blk + j
                shift = (QK_NOPE_DIM - j * QK_ROPE_DIM) % LANES
                r = pltpu.roll(qr, shift, 1) if shift else qr
                r = jnp.where(lane < QK_NOPE_DIM + QK_ROPE_DIM, r, 0.0)
                q_ref[0, h_idx, rows, :] = _head_tile(qn, r, h_idx).astype(BF16)
            yield

    _run_staggered(block(rows) for rows in _row_blocks(x_ref.shape[0]))


def _queries(x, bsz, shift, scale, norm_g, w_dq, q_norm_g, w_uq, q_nope_g, q_rope_g, head_sum, head_spread,
             seg32, ct, st):
    t, d = x.shape
    s_len = t // bsz
    tm = PROJ_ROW_TILE
    tiles = s_len // tm
    rank = w_dq.shape[1]
    half = QK_ROPE_DIM // 2
    wq = w_uq.reshape(rank, N_HEADS, QK_NOPE_DIM + QK_ROPE_DIM)
    wn = wq[:, :, :QK_NOPE_DIM].reshape(rank, -1).astype(BF16)
    wr = wq[:, :, QK_NOPE_DIM:].reshape(rank, -1).astype(BF16)
    wrs = jnp.concatenate([wq[:, :, QK_NOPE_DIM + half:], wq[:, :, QK_NOPE_DIM:QK_NOPE_DIM + half]], axis=2)
    wrs = wrs.reshape(rank, -1).astype(BF16)
    gn = jnp.tile(q_nope_g, N_HEADS).reshape(1, -1)
    gr = jnp.tile(q_rope_g, N_HEADS).reshape(1, -1)
    grs = jnp.tile(jnp.concatenate([q_rope_g[half:], q_rope_g[:half]]), N_HEADS).reshape(1, -1)
    full = _resident
    xspec = pl.BlockSpec((tm, d), lambda b, i: (b * tiles + i, 0))
    tab = pl.BlockSpec((tm, LANES), lambda b, i: (b * tiles + i, 0))
    per_b = pl.BlockSpec((1, 1, d), lambda b, i: (b, 0, 0))
    hspec = pl.BlockSpec((1, N_HEADS, tm, HEAD_LANES), lambda b, i: (b, 0, i, 0))
    args = (x, shift, scale, norm_g.reshape(1, d), w_dq.astype(BF16), q_norm_g.reshape(1, rank), wn, wr, wrs,
            gn, gr, grs, head_sum, head_spread, seg32, ct, st)
    in_specs = [xspec, per_b, per_b] + [full(a) for a in args[3:15]] + [tab, tab]
    return pl.pallas_call(
        _q_kernel,
        grid=(bsz, tiles),
        in_specs=in_specs,
        out_specs=hspec,
        out_shape=jax.ShapeDtypeStruct((bsz, N_HEADS, s_len, HEAD_LANES), BF16),
        compiler_params=_params("parallel", "parallel"),
        name="mla_queries",
    )(*args)


def _chunk_offsets(tile):
    row_chunk = lax.broadcasted_iota(jnp.int32, (tile, tile), 0) // CHUNK
    col_chunk = lax.broadcasted_iota(jnp.int32, (tile, tile), 1) // CHUNK
    return row_chunk - col_chunk


def _merge_head_pair(accs):
    outs = [acc / acc[:, V_HEAD_DIM:V_HEAD_DIM + 1] for acc in accs]
    lane = _lane_iota(outs[0].shape)
    tiles = [jnp.where(lane < V_HEAD_DIM, outs[i], pltpu.roll(outs[i + 1], V_HEAD_DIM, 1))
             for i in range(0, len(outs), 2)]
    return tiles[0] if len(tiles) == 1 else jnp.concatenate(tiles, axis=1)


def _attn_online_kernel(q_ref, k_ref, v_ref, o_ref, m_scr, acc_scr, *, tile):
    i = pl.program_id(2)
    diag_mask = _chunk_offsets(tile) >= 0
    qs = [q_ref[0, hh] for hh in range(HEADS_PER_STEP)]

    def tile_update(off, diagonal):
        for hh in range(HEADS_PER_STEP):
            s = lax.dot_general(qs[hh], k_ref[0, hh, pl.ds(off, tile), :], _NT_DIMS,
                                preferred_element_type=F32)
            if diagonal:
                s = jnp.where(diag_mask, s, MASK_VALUE)
            row_max = jnp.max(s, axis=-1, keepdims=True)
            m = row_max if diagonal else m_scr[hh]
            m_new = row_max if diagonal else jnp.maximum(m, row_max)
            pv = _dot(jnp.exp2(s - m_new).astype(BF16), v_ref[0, hh, pl.ds(off, tile), :])
            acc_scr[hh] = pv if diagonal else acc_scr[hh] * jnp.exp2(m - m_new) + pv
            m_scr[hh] = m_new

    tile_update(pl.multiple_of(i * tile, tile), True)

    def below_diagonal(j, carry):
        tile_update(pl.multiple_of(j * tile, tile), False)
        return carry

    lax.fori_loop(0, i, below_diagonal, 0)
    o_ref[0] = _merge_head_pair([acc_scr[hh] for hh in range(HEADS_PER_STEP)]).astype(o_ref.dtype)


def _attn_bounded_kernel(q_ref, k_ref, v_ref, o_ref, *, tile, n_tiles):
    g = pl.program_id(2)
    lo, hi = g, n_tiles - 1 - g
    diag_mask = _chunk_offsets(tile) >= 0
    zero = jnp.zeros((tile, HEAD_LANES), F32)
    acc_lo = [zero] * HEADS_PER_STEP
    acc_hi = [zero] * HEADS_PER_STEP

    def rows(ref, hh, idx):
        return ref[0, hh, pl.ds(pl.multiple_of(idx * tile, tile), tile), :]

    def weighted_values(q, k, v, mask):
        s = lax.dot_general(q, k, _NT_DIMS, preferred_element_type=F32)
        if mask is not None:
            s = jnp.where(mask, s, MASK_VALUE)
        return _dot(jnp.exp2(s).astype(BF16), v)

    def diagonal(idx, accs):
        for hh in range(HEADS_PER_STEP):
            accs[hh] = accs[hh] + weighted_values(rows(q_ref, hh, idx), rows(k_ref, hh, idx),
                                                  rows(v_ref, hh, idx), diag_mask)

    diagonal(lo, acc_lo)
    diagonal(hi, acc_hi)
    max_lo = n_tiles // 2 - 1
    for slot in range(1, n_tiles):
        owner_known = slot > max_lo
        if owner_known:
            q_idx, k_idx = hi, slot - 1 - lo
        else:
            is_lo = slot <= lo
            q_idx = jnp.where(is_lo, lo, hi)
            k_idx = jnp.where(is_lo, slot - 1, slot - 1 - lo)
        for hh in range(HEADS_PER_STEP):
            pv = weighted_values(rows(q_ref, hh, q_idx), rows(k_ref, hh, k_idx), rows(v_ref, hh, k_idx), None)
            if owner_known:
                acc_hi[hh] = acc_hi[hh] + pv
            else:
                acc_lo[hh] = acc_lo[hh] + jnp.where(is_lo, pv, 0.0)
                acc_hi[hh] = acc_hi[hh] + jnp.where(is_lo, 0.0, pv)
    o_ref[0, pl.ds(pl.multiple_of(lo * tile, tile), tile), :] = _merge_head_pair(acc_lo).astype(o_ref.dtype)
    o_ref[0, pl.ds(pl.multiple_of(hi * tile, tile), tile), :] = _merge_head_pair(acc_hi).astype(o_ref.dtype)


def _score_bound(q_nope_g, q_rope_g, k_nope_g, k_rope_g):
    def sq_len(g_nope, g_rope):
        return QK_NOPE_DIM * jnp.max(g_nope * g_nope) + QK_ROPE_DIM * jnp.max(g_rope * g_rope)
    return Q_SCALE * jnp.sqrt(sq_len(q_nope_g, q_rope_g) * sq_len(k_nope_g, k_rope_g)) * BOUND_SLACK


def _attention(q, k, v, bound):
    return lax.cond(bound <= MAX_UNSHIFTED_BOUND, _attention_bounded, _attention_online, q, k, v)


def _attention_online(q, k, v):
    bsz, nh, s_len, hl = q.shape
    tile = ATTN_TILE
    hp = HEADS_PER_STEP
    whole = pl.BlockSpec((1, hp, s_len, hl), lambda b, h, i: (b, h, 0, 0))
    return pl.pallas_call(
        functools.partial(_attn_online_kernel, tile=tile),
        grid=(bsz, nh // hp, s_len // tile),
        in_specs=[pl.BlockSpec((1, hp, tile, hl), lambda b, h, i: (b, h, i, 0)), whole, whole],
        out_specs=pl.BlockSpec((1, tile, hp * V_HEAD_DIM), lambda b, h, i: (b, i, h)),
        out_shape=jax.ShapeDtypeStruct((bsz, s_len, nh * V_HEAD_DIM), BF16),
        scratch_shapes=[pltpu.VMEM((hp, tile, 1), F32), pltpu.VMEM((hp, tile, hl), F32)],
        compiler_params=_params("parallel", "parallel", "arbitrary"),
        name="mla_attention_online",
    )(q, k, v)


def _attention_bounded(q, k, v):
    bsz, nh, s_len, hl = q.shape
    tile = ATTN_TILE
    hp = HEADS_PER_STEP
    n_tiles = s_len // tile
    assert n_tiles % 2 == 0
    whole = pl.BlockSpec((1, hp, s_len, hl), lambda b, h, g: (b, h, 0, 0))
    return pl.pallas_call(
        functools.partial(_attn_bounded_kernel, tile=tile, n_tiles=n_tiles),
        grid=(bsz, nh // hp, n_tiles // 2),
        in_specs=[whole, whole, whole],
        out_specs=pl.BlockSpec((1, s_len, hp * V_HEAD_DIM), lambda b, h, g: (b, 0, h)),
        out_shape=jax.ShapeDtypeStruct((bsz, s_len, nh * V_HEAD_DIM), BF16),
        compiler_params=_params("parallel", "parallel", "arbitrary"),
        name="mla_attention_bounded",
    )(q, k, v)


def _seg_ones(n, seg):
    idx = jnp.arange(n) // seg
    return (idx[:, None] == idx[None, :]).astype(BF16)


def kernel(x, c, positions, ada_w, ada_b, norm1_g, norm2_g, ffn_w_gate, ffn_w_up, ffn_w_down, s5_lam_re, s5_lam_im, s5_log_dt, s5_b_re, s5_b_im, s5_c_re, s5_c_im, s5_d, s5_w_glu, s5_b_glu, kv_ada_w, kv_ada_b, kv_norm_g, w_kv_a, kv_a_norm_g, w_kv_b, k_nope_norm_g, k_rope_norm_g, mla_w_dq, mla_q_norm_g, mla_w_uq, mla_q_nope_norm_g, mla_q_rope_norm_g, mla_w_o):
    bsz, s_len, d = x.shape
    depth = ada_w.shape[0]
    n_a = s5_lam_re.shape[0]
    t = bsz * s_len

    mods = _mods(c, ada_w, ada_b).reshape(depth, bsz, 6, d)
    kv_mods = _mods(c, kv_ada_w[None], kv_ada_b[None]).reshape(bsz, 2, d)
    cos_t, sin_t = _rope_tables(positions)
    ffn_w = tuple(w.astype(BF16) for w in (ffn_w_gate, ffn_w_up, ffn_w_down))

    a2_re, a2_im, *s5_mats = _s5_discretise(s5_lam_re, s5_lam_im, s5_log_dt, s5_b_re, s5_b_im, s5_c_re, s5_c_im)

    xt = x
    for l in range(n_a):
        m = mods[l]
        bb_re, bb_im, abb_re, abb_im, ca_re, ca_im, cb = (w[l] for w in s5_mats)
        slab_weights = _s5_slab_weights(bb_re, bb_im, abb_re, abb_im, s5_c_re[l], s5_c_im[l], ca_re, ca_im, cb)
        xt = _s5_mixer(xt, m[:, 0], m[:, 1], m[:, 2], norm1_g[l], slab_weights, a2_re[l], a2_im[l], s5_d[l],
                       s5_w_glu[l].astype(BF16), s5_b_glu[l], bsz)
        xt = _ffn(xt, m[:, 3], m[:, 4], m[:, 5], norm2_g[l], ffn_w, l, out_batch_major=l == n_a - 1)
    xb = xt.reshape(t, d)

    head_of_lane = jnp.arange(N_HEADS * QK_NOPE_DIM) // QK_NOPE_DIM
    head_sum = (head_of_lane[:, None] == jnp.arange(LANES)[None, :]).astype(BF16)
    head_spread = jnp.concatenate([head_sum.T, head_sum.T], axis=0)
    seg32 = _seg_ones(N_HEADS * QK_ROPE_DIM, QK_ROPE_DIM)
    per_b = lambda v: v.reshape(bsz, 1, d)
    k_all, v_all = _shared_kv(xb, bsz, per_b(kv_mods[:, 0]), per_b(kv_mods[:, 1]), kv_norm_g, w_kv_a,
                              kv_a_norm_g, w_kv_b, k_nope_norm_g, k_rope_norm_g, head_sum, head_spread,
                              cos_t, sin_t)
    for l in range(n_a, depth):
        j = l - n_a
        m = mods[l]
        bound = _score_bound(mla_q_nope_norm_g[j], mla_q_rope_norm_g[j], k_nope_norm_g, k_rope_norm_g)
        q = _queries(xb, bsz, per_b(m[:, 0]), per_b(m[:, 1]), norm1_g[l], mla_w_dq[j], mla_q_norm_g[j],
                     mla_w_uq[j], mla_q_nope_norm_g[j], mla_q_rope_norm_g[j], head_sum, head_spread, seg32,
                     cos_t, sin_t)
        o = _attention(q, k_all, v_all, bound).reshape(t, N_HEADS * V_HEAD_DIM)
        xb = _ffn(xb, per_b(m[:, 3]), per_b(m[:, 4]), per_b(m[:, 5]), norm2_g[l], ffn_w, l,
                  attn=(o, mla_w_o[j].astype(BF16), per_b(m[:, 2])))
    return xb.reshape(bsz, s_len, d)
```

```python
import functools
import math

import jax
import jax.numpy as jnp
from jax import lax
from jax.experimental import pallas as pl
from jax.experimental.pallas import tpu as pltpu

F32 = jnp.float32
BF16 = jnp.bfloat16

CHUNK = 64
SSM_GROUP = 16
SSM_STATE = 64
N_HEADS = 16
QK_NOPE_DIM = 64
QK_ROPE_DIM = 32
V_HEAD_DIM = 64
ROPE_THETA = 10000.0
ATTN_SCALE = 1.0 / math.sqrt(QK_NOPE_DIM + QK_ROPE_DIM)
Q_SCALE = ATTN_SCALE * math.log2(math.e)
EPS = 1e-6
MASK_VALUE = -1e30
BOUND_SLACK = 1.01
MAX_UNSHIFTED_BOUND = 40.0

LANES = 128
SUBLANES = 8
HEAD_LANES = 128
SLAB_GROUPS = LANES // SSM_GROUP
VMEM_LIMIT = 56 * 1024 * 1024

ROW_TILE = 512
PROJ_ROW_TILE = 1024
ROW_BLOCKS = 2
S5_TIME_TILE = 128
RELAYOUT_PITCH_PAD = 8
FFN_CHUNK = 256
ATTN_TILE = 512
HEADS_PER_STEP = 4


def _params(*sem):
    return pltpu.CompilerParams(dimension_semantics=sem, vmem_limit_bytes=VMEM_LIMIT)


def _rms(x, g):
    return x * lax.rsqrt(jnp.mean(x * x, axis=-1, keepdims=True) + EPS) * g


def _rowwise(fn, a, *mods):
    r = mods[0].shape[0]
    if r == 1:
        return fn(a, *mods)
    rows, d = a.shape
    out = fn(a.reshape(rows // r, r, d), *[m[None] for m in mods])
    return out.reshape(rows, d)


def _modulate(h, shift, scale):
    return _rowwise(lambda a, sh, sc: a * (1.0 + sc) + sh, h, shift, scale)


def _gated_add(x, gate, upd):
    return x + _rowwise(lambda a, g: a * g, upd, gate)


def _dot(a, b):
    return jnp.dot(a, b, preferred_element_type=F32)


_NT_DIMS = (((1,), (1,)), ((), ()))


def _resident(a):
    zeros = (0,) * a.ndim
    return pl.BlockSpec(a.shape, lambda *_: zeros, pipeline_mode=pl.Buffered(1))


def _lane_iota(shape):
    return lax.broadcasted_iota(jnp.int32, shape, len(shape) - 1)


def _rows_bt_to_tb(x_ref, scr):
    bsz, ts, d = x_ref.shape
    pitch = scr.shape[1] // bsz
    cols = []
    for s in range(d // LANES):
        lanes = slice(s * LANES, (s + 1) * LANES)
        for b in range(bsz):
            scr[s, b * pitch:b * pitch + ts, :] = x_ref[b, :, lanes]
        cols.append(jnp.concatenate([scr[s, pl.ds(t, bsz, stride=pitch), :] for t in range(ts)], axis=0))
    return jnp.concatenate(cols, axis=1)


def _rows_tb_to_bt(val, scr, out_ref):
    bsz, ts, d = out_ref.shape
    for s in range(d // LANES):
        lanes = slice(s * LANES, (s + 1) * LANES)
        scr[s] = val[:, lanes]
        for b in range(bsz):
            out_ref[b, :, lanes] = scr[s, pl.ds(b, ts, stride=bsz), :]


def _row_blocks(n_rows):
    step = n_rows // ROW_BLOCKS
    return [slice(i * step, (i + 1) * step) for i in range(ROW_BLOCKS)]


def _run_staggered(blocks):
    waiting, live = list(blocks), []
    while waiting or live:
        if waiting:
            live.append(waiting.pop(0))
        for gen in list(live):
            if next(gen, StopIteration) is StopIteration:
                live.remove(gen)


def _head_rstd(x, sum_ref, spread_ref, dim):
    ss = _dot((x * x).astype(BF16), sum_ref[...])
    rstd = lax.rsqrt(ss * (1.0 / dim) + EPS)
    hi = rstd.astype(BF16)
    lo = (rstd - hi.astype(F32)).astype(BF16)
    return _dot(jnp.concatenate([hi, lo], axis=1), spread_ref[...])


def _mods_kernel(c_ref, w_ref, b_ref, o_ref):
    c = c_ref[...]
    ca = c * jax.nn.sigmoid(c)
    o_ref[0] = _dot(ca.astype(BF16), w_ref[0].astype(BF16)) + b_ref[0]


def _mods(c, w, b, tn=2048):
    nl, d, n = w.shape
    bsz = c.shape[0]
    return pl.pallas_call(
        _mods_kernel,
        grid=(nl, n // tn),
        in_specs=[pl.BlockSpec((bsz, d), lambda l, j: (0, 0)),
                  pl.BlockSpec((1, d, tn), lambda l, j: (l, 0, j)),
                  pl.BlockSpec((1, 1, tn), lambda l, j: (l, 0, j))],
        out_specs=pl.BlockSpec((1, bsz, tn), lambda l, j: (l, 0, j)),
        out_shape=jax.ShapeDtypeStruct((nl, bsz, n), F32),
        compiler_params=_params("parallel", "parallel"),
        name="adaln_mods",
    )(c, w, b.reshape(nl, 1, n))


def _rope_kernel(pos_ref, inv_ref, spread_ref, cos_ref, sin_ref):
    ang = inv_ref[...] * pos_ref[...]
    spread = functools.partial(lax.dot_general, dimension_numbers=(((0,), (0,)), ((), ())),
                               precision=lax.Precision.HIGHEST, preferred_element_type=F32)
    cos_ref[...] = spread(jnp.cos(ang), spread_ref[...])
    sin_ref[...] = spread(jnp.sin(ang), spread_ref[...])


def _rope_tables(positions, tn=4096):
    t = positions.size
    half = QK_ROPE_DIM // 2
    inv = 1.0 / (ROPE_THETA ** (jnp.arange(0, QK_ROPE_DIM, 2, dtype=F32) / QK_ROPE_DIM))
    pos = positions.astype(F32).reshape(1, t)
    spread = (jnp.arange(LANES)[None, :] % half == jnp.arange(half)[:, None]).astype(F32)
    return pl.pallas_call(
        _rope_kernel,
        grid=(t // tn,),
        in_specs=[pl.BlockSpec((1, tn), lambda i: (0, i)),
                  pl.BlockSpec((half, 1), lambda i: (0, 0)),
                  pl.BlockSpec((half, LANES), lambda i: (0, 0))],
        out_specs=[pl.BlockSpec((tn, LANES), lambda i: (i, 0))] * 2,
        out_shape=[jax.ShapeDtypeStruct((t, LANES), F32)] * 2,
        compiler_params=_params("parallel"),
        name="rope_tables",
    )(pos, inv.reshape(half, 1), spread)


def _rope_sign(shape):
    lane = _lane_iota(shape)
    return jnp.where((lane % QK_ROPE_DIM) < (QK_ROPE_DIM // 2), -1.0, 1.0).astype(F32)


def _cmul(ar, ai, br, bi):
    return ar * br - ai * bi, ar * bi + ai * br


def _s5_disc_kernel(lr_ref, li_ref, ldt_ref, br_ref, bi_ref, cr_ref, ci_ref,
                    a2r_ref, a2i_ref, bbr_ref, bbi_ref, abbr_ref, abbi_ref, car_ref, cai_ref, cb_ref):
    lr = lr_ref[0]
    li = li_ref[0]
    dt = jnp.exp(ldt_ref[0])
    mag = jnp.exp(lr * dt)
    a_re = mag * jnp.cos(li * dt)
    a_im = mag * jnp.sin(li * dt)
    den = lr * lr + li * li
    nr = a_re - 1.0
    ni = a_im
    f_re = (nr * lr + ni * li) / den
    f_im = (ni * lr - nr * li) / den
    bb_re, bb_im = _cmul(f_re, f_im, br_ref[0], bi_ref[0])
    cr = cr_ref[0]
    ci = ci_ref[0]
    a2r_ref[0], a2i_ref[0] = _cmul(a_re, a_im, a_re, a_im)
    bbr_ref[0] = bb_re
    bbi_ref[0] = bb_im
    abbr_ref[0], abbi_ref[0] = _cmul(a_re, a_im, bb_re, bb_im)
    car_ref[0], cai_ref[0] = _cmul(cr, ci, a_re, a_im)
    group_dot = functools.partial(jnp.einsum, "gpn,gqn->gpq", preferred_element_type=F32,
                                  precision=lax.Precision.HIGHEST)
    cb_ref[0] = group_dot(cr, bb_re) - group_dot(ci, bb_im)


def _s5_discretise(lam_re, lam_im, log_dt, b_re, b_im, c_re, c_im):
    na, g, n = lam_re.shape
    p = b_re.shape[-1]
    vec = pl.BlockSpec((1, g, 1, n), lambda l: (l, 0, 0, 0))
    mat = pl.BlockSpec((1, g, p, n), lambda l: (l, 0, 0, 0))
    vec_shape = jax.ShapeDtypeStruct((na, g, 1, n), F32)
    mat_shape = jax.ShapeDtypeStruct((na, g, p, n), F32)
    return pl.pallas_call(
        _s5_disc_kernel,
        grid=(na,),
        in_specs=[vec, vec, pl.BlockSpec((1, g, 1, 1), lambda l: (l, 0, 0, 0)), mat, mat, mat, mat],
        out_specs=[vec, vec] + [mat] * 6 + [pl.BlockSpec((1, g, p, p), lambda l: (l, 0, 0, 0))],
        out_shape=[vec_shape] * 2 + [mat_shape] * 6 + [jax.ShapeDtypeStruct((na, g, p, p), F32)],
        compiler_params=_params("parallel"),
        name="s5_discretise",
    )(lam_re.reshape(na, g, 1, n), lam_im.reshape(na, g, 1, n), log_dt.reshape(na, g, 1, 1),
      b_re.transpose(0, 1, 3, 2), b_im.transpose(0, 1, 3, 2), c_re, c_im)


def _slab_block_diag(w_gab):
    g, a, b = w_gab.shape
    ns = g // SLAB_GROUPS
    w = w_gab.reshape(ns, SLAB_GROUPS, a, b)
    eye = jnp.eye(SLAB_GROUPS, dtype=w.dtype)
    return jnp.einsum("kgab,gh->kgahb", w, eye).reshape(ns, SLAB_GROUPS * a, SLAB_GROUPS * b)


def _s5_slab_weights(bb_re, bb_im, abb_re, abb_im, c_re, c_im, ca_re, ca_im, cb):
    col = lambda re, im: jnp.concatenate([_slab_block_diag(re), _slab_block_diag(im)], axis=2)
    b_pair = jnp.concatenate([col(bb_re, bb_im), col(abb_re, abb_im)], axis=1)
    out = lambda w: _slab_block_diag(w.transpose(0, 2, 1))
    c_pair_re = jnp.concatenate([out(c_re), out(ca_re)], axis=2)
    c_pair_im = jnp.concatenate([out(c_im), out(ca_im)], axis=2)
    cb_blk = _slab_block_diag(cb.transpose(0, 2, 1))
    return tuple(w.astype(BF16) for w in (b_pair, c_pair_re, c_pair_im, cb_blk))


def _s5_mixer_kernel(x_ref, shift_ref, scale_ref, gate_ref, g_ref, bpair_ref, cre_ref, cim_ref, cb_ref,
                     a2r_ref, a2i_ref, d_ref, wglu_ref, bglu_ref, o_ref,
                     bu_re, bu_im, st_re, st_im, tail_scr, yf_scr, ys_scr, *maybe_relayout_scr,
                     ts, n_slabs, slab_state):
    @pl.when(pl.program_id(0) == 0)
    def _():
        st_re[...] = jnp.zeros_like(st_re)
        st_im[...] = jnp.zeros_like(st_im)
        tail_scr[...] = jnp.zeros_like(tail_scr)

    pairs = ts // 2
    pair_rows = pairs * SUBLANES
    x = _rows_bt_to_tb(x_ref, *maybe_relayout_scr) if maybe_relayout_scr else x_ref[...]
    d = x.shape[1]
    h = _modulate(_rms(x, g_ref[...]), shift_ref[...], scale_ref[...])
    h3 = h.reshape(pairs, 2 * SUBLANES, d)
    h_first = h3[:, :SUBLANES].reshape(pair_rows, d).astype(BF16)
    h_second = h3[:, SUBLANES:].reshape(pair_rows, d).astype(BF16)

    def project_in(k):
        lanes = slice(k * LANES, (k + 1) * LANES)
        ssl = slice(k * slab_state, (k + 1) * slab_state)
        bu = _dot(jnp.concatenate([h_second[:, lanes], h_first[:, lanes]], axis=1), bpair_ref[k])
        bu_re[:, ssl] = bu[:, :slab_state]
        bu_im[:, ssl] = bu[:, slab_state:]

    def recur(k):
        ssl = slice(k * slab_state, (k + 1) * slab_state)
        ar = jnp.broadcast_to(a2r_ref[:, ssl], (SUBLANES, slab_state))
        ai = jnp.broadcast_to(a2i_ref[:, ssl], (SUBLANES, slab_state))
        sr, si = st_re[:, ssl], st_im[:, ssl]
        for j in range(pairs):
            rows = slice(j * SUBLANES, (j + 1) * SUBLANES)
            sr, si = ar * sr - ai * si + bu_re[rows, ssl], ar * si + ai * sr + bu_im[rows, ssl]
            bu_re[rows, ssl] = sr
            bu_im[rows, ssl] = si
        st_re[:, ssl] = sr
        st_im[:, ssl] = si

    def project_out(k):
        ssl = slice(k * slab_state, (k + 1) * slab_state)
        lanes = slice(k * LANES, (k + 1) * LANES)
        z = _dot(bu_re[:, ssl].astype(BF16), cre_ref[k]) - _dot(bu_im[:, ssl].astype(BF16), cim_ref[k])
        ys_scr[:, lanes] = z[:, :LANES]
        carried = jnp.concatenate([tail_scr[:, lanes], z[:pair_rows - SUBLANES, LANES:]], axis=0)
        tail_scr[:, lanes] = z[pair_rows - SUBLANES:, LANES:]
        yf_scr[:, lanes] = carried + _dot(h_first[:, lanes], cb_ref[k])

    for k in range(n_slabs + 2):
        if k < n_slabs:
            project_in(k)
        if 0 <= k - 1 < n_slabs:
            recur(k - 1)
        if 0 <= k - 2 < n_slabs:
            project_out(k - 2)

    y = jnp.concatenate([yf_scr[...].reshape(pairs, SUBLANES, d), ys_scr[...].reshape(pairs, SUBLANES, d)],
                        axis=1).reshape(2 * pair_rows, d)
    y = y + d_ref[...] * h
    gl = jax.nn.gelu(y)
    z = _dot(gl.astype(BF16), wglu_ref[...]) + bglu_ref[...]
    mix = gl * jax.nn.sigmoid(z)
    o_ref[...] = _gated_add(x, gate_ref[...], mix)


def _s5_mixer(x, shift, scale, gate, norm_g, slab_weights, a2_re, a2_im, d_skip, w_glu, b_glu, bsz):
    assert bsz == SUBLANES
    ts = S5_TIME_TILE
    rows = ts * bsz
    pair_rows = rows // 2
    if x.ndim == 3:
        d = x.shape[2]
        t = x.shape[0] * x.shape[1]
        x_spec = pl.BlockSpec((bsz, ts, d), lambda i: (0, i, 0))
        relayout_scr = [pltpu.VMEM((d // LANES, bsz * (ts + RELAYOUT_PITCH_PAD), LANES), F32)]
    else:
        t, d = x.shape
        x_spec = pl.BlockSpec((rows, d), lambda i: (i, 0))
        relayout_scr = []
    b_pair, c_pair_re, c_pair_im, cb_blk = slab_weights
    n_slabs, _, two_state = b_pair.shape
    slab_state = two_state // 2
    n_state = n_slabs * slab_state
    full = _resident
    row = lambda a: a.reshape(1, -1)
    args = (x, shift, scale, gate, row(norm_g), b_pair, c_pair_re, c_pair_im, cb_blk, row(a2_re), row(a2_im),
            row(d_skip), w_glu, row(b_glu))
    return pl.pallas_call(
        functools.partial(_s5_mixer_kernel, ts=ts, n_slabs=n_slabs, slab_state=slab_state),
        grid=(t // rows,),
        in_specs=[x_spec] + [full(a) for a in args[1:]],
        out_specs=pl.BlockSpec((rows, d), lambda i: (i, 0)),
        out_shape=jax.ShapeDtypeStruct((t, d), F32),
        scratch_shapes=[pltpu.VMEM((pair_rows, n_state), F32), pltpu.VMEM((pair_rows, n_state), F32),
                        pltpu.VMEM((bsz, n_state), F32), pltpu.VMEM((bsz, n_state), F32),
                        pltpu.VMEM((bsz, d), F32), pltpu.VMEM((pair_rows, d), F32),
                        pltpu.VMEM((pair_rows, d), F32)] + relayout_scr,
        compiler_params=_params("arbitrary"),
        name="s5_mixer",
    )(*args)


def _ffn_kernel(*refs, n_chunks, with_attn, out_batch_major):
    if out_batch_major:
        *refs, relayout_scr = refs
    if with_attn:
        (x_ref, o_in_ref, wo_ref, gate1_ref, shift_ref, scale_ref, gate2_ref, g_ref,
         wg_ref, wu_ref, wd_ref, out_ref, h_scr, acc_scr) = refs
        x = x_ref[...] + gate1_ref[0] * _dot(o_in_ref[...], wo_ref[...])
        shift, scale, gate2 = shift_ref[0], scale_ref[0], gate2_ref[0]
    else:
        (x_ref, shift_ref, scale_ref, gate2_ref, g_ref,
         wg_ref, wu_ref, wd_ref, out_ref, h_scr, acc_scr) = refs
        x = x_ref[...]
        shift, scale, gate2 = shift_ref[...], scale_ref[...], gate2_ref[...]

    h_scr[...] = _modulate(_rms(x, g_ref[...]), shift, scale).astype(BF16)
    acc_scr[...] = jnp.zeros_like(acc_scr)

    hb = h_scr[...]
    for f in range(n_chunks):
        cols = slice(f * FFN_CHUNK, (f + 1) * FFN_CHUNK)
        gt = _dot(hb, wg_ref[0, :, cols])
        up = _dot(hb, wu_ref[0, :, cols])
        act = (gt * jax.nn.sigmoid(gt) * up).astype(BF16)
        acc_scr[...] += _dot(act, wd_ref[0, cols, :])
    out = _gated_add(x, gate2, acc_scr[...])
    if out_batch_major:
        _rows_tb_to_bt(out, relayout_scr, out_ref)
    else:
        out_ref[...] = out


def _ffn(x, shift, scale, gate2, norm_g, weights, layer, attn=None, out_batch_major=False):
    t, d = x.shape
    wg, wu, wd = weights
    n_chunks = wg.shape[2] // FFN_CHUNK
    tm = ROW_TILE
    full = _resident
    of_layer = lambda w: pl.BlockSpec((1,) + w.shape[1:], lambda i: (layer, 0, 0), pipeline_mode=pl.Buffered(1))
    xspec = pl.BlockSpec((tm, d), lambda i: (i, 0))
    g2 = norm_g.reshape(1, d)
    if attn is None:
        args = (x, shift, scale, gate2, g2, wg, wu, wd)
        in_specs = [xspec] + [full(a) for a in args[1:5]] + [of_layer(wg), of_layer(wu), of_layer(wd)]
    else:
        o_in, w_o, gate1 = attn
        tiles_per_batch = (t // gate1.shape[0]) // tm
        per_b = pl.BlockSpec((1, 1, d), lambda i: (i // tiles_per_batch, 0, 0))
        args = (x, o_in, w_o, gate1, shift, scale, gate2, g2, wg, wu, wd)
        in_specs = [xspec, xspec, full(w_o), per_b, per_b, per_b, per_b, full(g2),
                    of_layer(wg), of_layer(wu), of_layer(wd)]
    scratch = [pltpu.VMEM((tm, d), BF16), pltpu.VMEM((tm, d), F32)]
    out_spec, out_shape = xspec, jax.ShapeDtypeStruct((t, d), F32)
    if out_batch_major:
        bsz = shift.shape[0]
        out_spec = pl.BlockSpec((bsz, tm // bsz, d), lambda i: (0, i, 0))
        out_shape = jax.ShapeDtypeStruct((bsz, t // bsz, d), F32)
        scratch.append(pltpu.VMEM((d // LANES, tm, LANES), F32))
    return pl.pallas_call(
        functools.partial(_ffn_kernel, n_chunks=n_chunks, with_attn=attn is not None,
                          out_batch_major=out_batch_major),
        grid=(t // tm,),
        in_specs=in_specs,
        out_specs=out_spec,
        out_shape=out_shape,
        scratch_shapes=scratch,
        compiler_params=_params("parallel"),
        name="ffn_attn_out" if attn is not None else "ffn",
    )(*args)


def _head_tile(nope, rope_at_64, h):
    blk = nope[:, (h // 2) * LANES:(h // 2 + 1) * LANES]
    if h % 2:
        blk = pltpu.roll(blk, QK_NOPE_DIM, 1)
    return jnp.where(_lane_iota(blk.shape) < QK_NOPE_DIM, blk, rope_at_64)


def _kv_kernel(x_ref, shift_ref, scale_ref, g_ref, wa_ref, ga_ref, wkn_ref, wv_ref, gkn_ref,
               gkr_ref, gkrs_ref, sum_ref, spread_ref, ct_ref, st_ref, k_ref, v_ref):
    rank = ga_ref.shape[1]
    ones_col = (_lane_iota((1, LANES)) == V_HEAD_DIM).astype(F32)

    def block(rows):
        x = x_ref[rows, :]
        hb = _modulate(_rms(x, g_ref[...]), shift_ref[0], scale_ref[0]).astype(BF16)
        yield
        kva = _dot(hb, wa_ref[...])
        yield
        cb = _rms(kva[:, :rank], ga_ref[...]).astype(BF16)
        a = kva[:, rank:rank + LANES]
        asw = kva[:, rank + LANES:rank + 2 * LANES]
        rstd = lax.rsqrt(jnp.sum(a * a, axis=-1, keepdims=True) * (1.0 / QK_ROPE_DIM) + EPS)
        sgn = _rope_sign(a.shape)
        kr = rstd * (a * gkr_ref[...] * ct_ref[rows, :] + asw * gkrs_ref[...] * (st_ref[rows, :] * sgn))
        kr64 = pltpu.roll(kr, QK_NOPE_DIM, 1)
        yield
        kn = _dot(cb, wkn_ref[...])
        v = _dot(cb, wv_ref[...])
        yield
        kn = kn * _head_rstd(kn, sum_ref, spread_ref, QK_NOPE_DIM) * gkn_ref[...]
        yield
        for h in range(N_HEADS):
            k_ref[0, h, rows, :] = _head_tile(kn, kr64, h).astype(BF16)
            v_ref[0, h, rows, :] = (v[:, h * LANES:(h + 1) * LANES] + ones_col).astype(BF16)

    _run_staggered(block(rows) for rows in _row_blocks(x_ref.shape[0]))


def _shared_kv(x, bsz, k_shift, k_scale, kv_norm_g, w_kv_a, kv_a_norm_g, w_kv_b, k_nope_g, k_rope_g,
               head_sum, head_spread, ct, st):
    t, d = x.shape
    s_len = t // bsz
    tm = PROJ_ROW_TILE
    tiles = s_len // tm
    rank = kv_a_norm_g.shape[0]
    half = QK_ROPE_DIM // 2
    pad = jnp.zeros((d, LANES - QK_ROPE_DIM), F32)
    w_rope = w_kv_a[:, rank:]
    w_rope_sw = jnp.concatenate([w_rope[:, half:], w_rope[:, :half]], axis=1)
    wa = jnp.concatenate([w_kv_a[:, :rank], w_rope, pad, w_rope_sw, pad], axis=1).astype(BF16)
    wkv = w_kv_b.reshape(rank, N_HEADS, QK_NOPE_DIM + V_HEAD_DIM)
    wkn = wkv[:, :, :QK_NOPE_DIM].reshape(rank, N_HEADS * QK_NOPE_DIM).astype(BF16)
    wv = jnp.pad(wkv[:, :, QK_NOPE_DIM:], ((0, 0), (0, 0), (0, HEAD_LANES - V_HEAD_DIM)))
    wv = wv.reshape(rank, N_HEADS * HEAD_LANES).astype(BF16)
    gkn = jnp.tile(k_nope_g, N_HEADS).reshape(1, -1)
    lane_pad = lambda g: jnp.pad(g, (0, LANES - g.shape[0])).reshape(1, LANES)
    gkr = lane_pad(k_rope_g)
    gkrs = lane_pad(jnp.concatenate([k_rope_g[half:], k_rope_g[:half]]))
    full = _resident
    xspec = pl.BlockSpec((tm, d), lambda b, i: (b * tiles + i, 0))
    tab = pl.BlockSpec((tm, LANES), lambda b, i: (b * tiles + i, 0))
    per_b = pl.BlockSpec((1, 1, d), lambda b, i: (b, 0, 0))
    hspec = pl.BlockSpec((1, N_HEADS, tm, HEAD_LANES), lambda b, i: (b, 0, i, 0))
    args = (x, k_shift, k_scale, kv_norm_g.reshape(1, d), wa, kv_a_norm_g.reshape(1, rank), wkn, wv,
            gkn, gkr, gkrs, head_sum, head_spread, ct, st)
    in_specs = [xspec, per_b, per_b] + [full(a) for a in args[3:13]] + [tab, tab]
    return pl.pallas_call(
        _kv_kernel,
        grid=(bsz, tiles),
        in_specs=in_specs,
        out_specs=[hspec, hspec],
        out_shape=[jax.ShapeDtypeStruct((bsz, N_HEADS, s_len, HEAD_LANES), BF16)] * 2,
        compiler_params=_params("parallel", "parallel"),
        name="mla_shared_kv",
    )(*args)


def _q_kernel(x_ref, shift_ref, scale_ref, g_ref, wdq_ref, gq_ref, wn_ref, wr_ref, wrs_ref,
              gn_ref, gr_ref, grs_ref, sum_ref, spread_ref, seg32_ref, ct_ref, st_ref, q_ref):
    heads_per_blk = LANES // QK_ROPE_DIM

    def block(rows):
        x = x_ref[rows, :]
        hb = _modulate(_rms(x, g_ref[...]), shift_ref[0], scale_ref[0]).astype(BF16)
        yield
        cq = _dot(hb, wdq_ref[...])
        yield
        cq = _rms(cq, gq_ref[...]).astype(BF16)
        yield
        qn = _dot(cq, wn_ref[...])
        a = _dot(cq, wr_ref[...])
        asw = _dot(cq, wrs_ref[...])
        yield
        qn = qn * _head_rstd(qn, sum_ref, spread_ref, QK_NOPE_DIM) * (gn_ref[...] * Q_SCALE)
        ssr = _dot((a * a).astype(BF16), seg32_ref[...])
        yield
        rstd = lax.rsqrt(ssr * (1.0 / QK_ROPE_DIM) + EPS) * Q_SCALE
        ct = ct_ref[rows, :]
        st = st_ref[rows, :] * _rope_sign(ct.shape)
        for blk in range(N_HEADS // heads_per_blk):
            sl = slice(blk * LANES, (blk + 1) * LANES)
            qr = rstd[:, sl] * (a[:, sl] * gr_ref[:, sl] * ct + asw[:, sl] * grs_ref[:, sl] * st)
            lane = _lane_iota(qr.shape)
            for j in range(heads_per_blk):
                h_idx = blk * heads_per_blk + j
                shift = (QK_NOPE_DIM - j * QK_ROPE_DIM) % LANES
                r = pltpu.roll(qr, shift, 1) if shift else qr
                r = jnp.where(lane < QK_NOPE_DIM + QK_ROPE_DIM, r, 0.0)
                q_ref[0, h_idx, rows, :] = _head_tile(qn, r, h_idx).astype(BF16)
            yield

    _run_staggered(block(rows) for rows in _row_blocks(x_ref.shape[0]))


def _queries(x, bsz, shift, scale, norm_g, w_dq, q_norm_g, w_uq, q_nope_g, q_rope_g, head_sum, head_spread,
             seg32, ct, st):
    t, d = x.shape
    s_len = t // bsz
    tm = PROJ_ROW_TILE
    tiles = s_len // tm
    rank = w_dq.shape[1]
    half = QK_ROPE_DIM // 2
    wq = w_uq.reshape(rank, N_HEADS, QK_NOPE_DIM + QK_ROPE_DIM)
    wn = wq[:, :, :QK_NOPE_DIM].reshape(rank, -1).astype(BF16)
    wr = wq[:, :, QK_NOPE_DIM:].reshape(rank, -1).astype(BF16)
    wrs = jnp.concatenate([wq[:, :, QK_NOPE_DIM + half:], wq[:, :, QK_NOPE_DIM:QK_NOPE_DIM + half]], axis=2)
    wrs = wrs.reshape(rank, -1).astype(BF16)
    gn = jnp.tile(q_nope_g, N_HEADS).reshape(1, -1)
    gr = jnp.tile(q_rope_g, N_HEADS).reshape(1, -1)
    grs = jnp.tile(jnp.concatenate([q_rope_g[half:], q_rope_g[:half]]), N_HEADS).reshape(1, -1)
    full = _resident
    xspec = pl.BlockSpec((tm, d), lambda b, i: (b * tiles + i, 0))
    tab = pl.BlockSpec((tm, LANES), lambda b, i: (b * tiles + i, 0))
    per_b = pl.BlockSpec((1, 1, d), lambda b, i: (b, 0, 0))
    hspec = pl.BlockSpec((1, N_HEADS, tm, HEAD_LANES), lambda b, i: (b, 0, i, 0))
    args = (x, shift, scale, norm_g.reshape(1, d), w_dq.astype(BF16), q_norm_g.reshape(1, rank), wn, wr, wrs,
            gn, gr, grs, head_sum, head_spread, seg32, ct, st)
    in_specs = [xspec, per_b, per_b] + [full(a) for a in args[3:15]] + [tab, tab]
    return pl.pallas_call(
        _q_kernel,
        grid=(bsz, tiles),
        in_specs=in_specs,
        out_specs=hspec,
        out_shape=jax.ShapeDtypeStruct((bsz, N_HEADS, s_len, HEAD_LANES), BF16),
        compiler_params=_params("parallel", "parallel"),
        name="mla_queries",
    )(*args)


def _chunk_offsets(tile):
    row_chunk = lax.broadcasted_iota(jnp.int32, (tile, tile), 0) // CHUNK
    col_chunk = lax.broadcasted_iota(jnp.int32, (tile, tile), 1) // CHUNK
    return row_chunk - col_chunk


def _merge_head_pair(accs):
    outs = [acc / acc[:, V_HEAD_DIM:V_HEAD_DIM + 1] for acc in accs]
    lane = _lane_iota(outs[0].shape)
    tiles = [jnp.where(lane < V_HEAD_DIM, outs[i], pltpu.roll(outs[i + 1], V_HEAD_DIM, 1))
             for i in range(0, len(outs), 2)]
    return tiles[0] if len(tiles) == 1 else jnp.concatenate(tiles, axis=1)


def _attn_online_kernel(q_ref, k_ref, v_ref, o_ref, m_scr, acc_scr, *, tile):
    i = pl.program_id(2)
    diag_mask = _chunk_offsets(tile) >= 0
    qs = [q_ref[0, hh] for hh in range(HEADS_PER_STEP)]

    def tile_update(off, diagonal):
        for hh in range(HEADS_PER_STEP):
            s = lax.dot_general(qs[hh], k_ref[0, hh, pl.ds(off, tile), :], _NT_DIMS,
                                preferred_element_type=F32)
            if diagonal:
                s = jnp.where(diag_mask, s, MASK_VALUE)
            row_max = jnp.max(s, axis=-1, keepdims=True)
            m = row_max if diagonal else m_scr[hh]
            m_new = row_max if diagonal else jnp.maximum(m, row_max)
            pv = _dot(jnp.exp2(s - m_new).astype(BF16), v_ref[0, hh, pl.ds(off, tile), :])
            acc_scr[hh] = pv if diagonal else acc_scr[hh] * jnp.exp2(m - m_new) + pv
            m_scr[hh] = m_new

    tile_update(pl.multiple_of(i * tile, tile), True)

    def below_diagonal(j, carry):
        tile_update(pl.multiple_of(j * tile, tile), False)
        return carry

    lax.fori_loop(0, i, below_diagonal, 0)
    o_ref[0] = _merge_head_pair([acc_scr[hh] for hh in range(HEADS_PER_STEP)]).astype(o_ref.dtype)


def _attn_bounded_kernel(q_ref, k_ref, v_ref, o_ref, *, tile, n_tiles):
    g = pl.program_id(2)
    lo, hi = g, n_tiles - 1 - g
    diag_mask = _chunk_offsets(tile) >= 0
    zero = jnp.zeros((tile, HEAD_LANES), F32)
    acc_lo = [zero] * HEADS_PER_STEP
    acc_hi = [zero] * HEADS_PER_STEP

    def rows(ref, hh, idx):
        return ref[0, hh, pl.ds(pl.multiple_of(idx * tile, tile), tile), :]

    def weighted_values(q, k, v, mask):
        s = lax.dot_general(q, k, _NT_DIMS, preferred_element_type=F32)
        if mask is not None:
            s = jnp.where(mask, s, MASK_VALUE)
        return _dot(jnp.exp2(s).astype(BF16), v)

    def diagonal(idx, accs):
        for hh in range(HEADS_PER_STEP):
            accs[hh] = accs[hh] + weighted_values(rows(q_ref, hh, idx), rows(k_ref, hh, idx),
                                                  rows(v_ref, hh, idx), diag_mask)

    diagonal(lo, acc_lo)
    diagonal(hi, acc_hi)
    max_lo = n_tiles // 2 - 1
    for slot in range(1, n_tiles):
        owner_known = slot > max_lo
        if owner_known:
            q_idx, k_idx = hi, slot - 1 - lo
        else:
            is_lo = slot <= lo
            q_idx = jnp.where(is_lo, lo, hi)
            k_idx = jnp.where(is_lo, slot - 1, slot - 1 - lo)
        for hh in range(HEADS_PER_STEP):
            pv = weighted_values(rows(q_ref, hh, q_idx), rows(k_ref, hh, k_idx), rows(v_ref, hh, k_idx), None)
            if owner_known:
                acc_hi[hh] = acc_hi[hh] + pv
            else:
                acc_lo[hh] = acc_lo[hh] + jnp.where(is_lo, pv, 0.0)
                acc_hi[hh] = acc_hi[hh] + jnp.where(is_lo, 0.0, pv)
    o_ref[0, pl.ds(pl.multiple_of(lo * tile, tile), tile), :] = _merge_head_pair(acc_lo).astype(o_ref.dtype)
    o_ref[0, pl.ds(pl.multiple_of(hi * tile, tile), tile), :] = _merge_head_pair(acc_hi).astype(o_ref.dtype)


def _score_bound(q_nope_g, q_rope_g, k_nope_g, k_rope_g):
    def sq_len(g_nope, g_rope):
        return QK_NOPE_DIM * jnp.max(g_nope * g_nope) + QK_ROPE_DIM * jnp.max(g_rope * g_rope)
    return Q_SCALE * jnp.sqrt(sq_len(q_nope_g, q_rope_g) * sq_len(k_nope_g, k_rope_g)) * BOUND_SLACK


def _attention(q, k, v, bound):
    return lax.cond(bound <= MAX_UNSHIFTED_BOUND, _attention_bounded, _attention_online, q, k, v)


def _attention_online(q, k, v):
    bsz, nh, s_len, hl = q.shape
    tile = ATTN_TILE
    hp = HEADS_PER_STEP
    whole = pl.BlockSpec((1, hp, s_len, hl), lambda b, h, i: (b, h, 0, 0))
    return pl.pallas_call(
        functools.partial(_attn_online_kernel, tile=tile),
        grid=(bsz, nh // hp, s_len // tile),
        in_specs=[pl.BlockSpec((1, hp, tile, hl), lambda b, h, i: (b, h, i, 0)), whole, whole],
        out_specs=pl.BlockSpec((1, tile, hp * V_HEAD_DIM), lambda b, h, i: (b, i, h)),
        out_shape=jax.ShapeDtypeStruct((bsz, s_len, nh * V_HEAD_DIM), BF16),
        scratch_shapes=[pltpu.VMEM((hp, tile, 1), F32), pltpu.VMEM((hp, tile, hl), F32)],
        compiler_params=_params("parallel", "parallel", "arbitrary"),
        name="mla_attention_online",
    )(q, k, v)


def _attention_bounded(q, k, v):
    bsz, nh, s_len, hl = q.shape
    tile = ATTN_TILE
    hp = HEADS_PER_STEP
    n_tiles = s_len // tile
    assert n_tiles % 2 == 0
    whole = pl.BlockSpec((1, hp, s_len, hl), lambda b, h, g: (b, h, 0, 0))
    return pl.pallas_call(
        functools.partial(_attn_bounded_kernel, tile=tile, n_tiles=n_tiles),
        grid=(bsz, nh // hp, n_tiles // 2),
        in_specs=[whole, whole, whole],
        out_specs=pl.BlockSpec((1, s_len, hp * V_HEAD_DIM), lambda b, h, g: (b, 0, h)),
        out_shape=jax.ShapeDtypeStruct((bsz, s_len, nh * V_HEAD_DIM), BF16),
        compiler_params=_params("parallel", "parallel", "arbitrary"),
        name="mla_attention_bounded",
    )(q, k, v)


def _seg_ones(n, seg):
    idx = jnp.arange(n) // seg
    return (idx[:, None] == idx[None, :]).astype(BF16)


def kernel(x, c, positions, ada_w, ada_b, norm1_g, norm2_g, ffn_w_gate, ffn_w_up, ffn_w_down, s5_lam_re, s5_lam_im, s5_log_dt, s5_b_re, s5_b_im, s5_c_re, s5_c_im, s5_d, s5_w_glu, s5_b_glu, kv_ada_w, kv_ada_b, kv_norm_g, w_kv_a, kv_a_norm_g, w_kv_b, k_nope_norm_g, k_rope_norm_g, mla_w_dq, mla_q_norm_g, mla_w_uq, mla_q_nope_norm_g, mla_q_rope_norm_g, mla_w_o):
    bsz, s_len, d = x.shape
    depth = ada_w.shape[0]
    n_a = s5_lam_re.shape[0]
    t = bsz * s_len

    mods = _mods(c, ada_w, ada_b).reshape(depth, bsz, 6, d)
    kv_mods = _mods(c, kv_ada_w[None], kv_ada_b[None]).reshape(bsz, 2, d)
    cos_t, sin_t = _rope_tables(positions)
    ffn_w = tuple(w.astype(BF16) for w in (ffn_w_gate, ffn_w_up, ffn_w_down))

    a2_re, a2_im, *s5_mats = _s5_discretise(s5_lam_re, s5_lam_im, s5_log_dt, s5_b_re, s5_b_im, s5_c_re, s5_c_im)

    xt = x
    for l in range(n_a):
        m = mods[l]
        bb_re, bb_im, abb_re, abb_im, ca_re, ca_im, cb = (w[l] for w in s5_mats)
        slab_weights = _s5_slab_weights(bb_re, bb_im, abb_re, abb_im, s5_c_re[l], s5_c_im[l], ca_re, ca_im, cb)
        xt = _s5_mixer(xt, m[:, 0], m[:, 1], m[:, 2], norm1_g[l], slab_weights, a2_re[l], a2_im[l], s5_d[l],
                       s5_w_glu[l].astype(BF16), s5_b_glu[l], bsz)
        xt = _ffn(xt, m[:, 3], m[:, 4], m[:, 5], norm2_g[l], ffn_w, l, out_batch_major=l == n_a - 1)
    xb = xt.reshape(t, d)

    head_of_lane = jnp.arange(N_HEADS * QK_NOPE_DIM) // QK_NOPE_DIM
    head_sum = (head_of_lane[:, None] == jnp.arange(LANES)[None, :]).astype(BF16)
    head_spread = jnp.concatenate([head_sum.T, head_sum.T], axis=0)
    seg32 = _seg_ones(N_HEADS * QK_ROPE_DIM, QK_ROPE_DIM)
    per_b = lambda v: v.reshape(bsz, 1, d)
    k_all, v_all = _shared_kv(xb, bsz, per_b(kv_mods[:, 0]), per_b(kv_mods[:, 1]), kv_norm_g, w_kv_a,
                              kv_a_norm_g, w_kv_b, k_nope_norm_g, k_rope_norm_g, head_sum, head_spread,
                              cos_t, sin_t)
    for l in range(n_a, depth):
        j = l - n_a
        m = mods[l]
        bound = _score_bound(mla_q_nope_norm_g[j], mla_q_rope_norm_g[j], k_nope_norm_g, k_rope_norm_g)
        q = _queries(xb, bsz, per_b(m[:, 0]), per_b(m[:, 1]), norm1_g[l], mla_w_dq[j], mla_q_norm_g[j],
                     mla_w_uq[j], mla_q_nope_norm_g[j], mla_q_rope_norm_g[j], head_sum, head_spread, seg32,
                     cos_t, sin_t)
        o = _attention(q, k_all, v_all, bound).reshape(t, N_HEADS * V_HEAD_DIM)
        xb = _ffn(xb, per_b(m[:, 3]), per_b(m[:, 4]), per_b(m[:, 5]), norm2_g[l], ffn_w, l,
                  attn=(o, mla_w_o[j].astype(BF16), per_b(m[:, 2])))
    return xb.reshape(bsz, s_len, d)
```
